```python
import numpy as np
import jax
import jax.numpy as jnp
from jax import lax

D_MODEL = 1024
BATCH = 16
SEQ = 2048
DEPTH = 2

HEAD_DIM = 64
N_HEADS = D_MODEL // HEAD_DIM
SWA_Q_HEADS = 3 * N_HEADS // 8
NSA_Q_HEADS = SWA_Q_HEADS
FOX_HEADS = N_HEADS - SWA_Q_HEADS - NSA_Q_HEADS
KV_HEADS = 2
GQA_GROUP = SWA_Q_HEADS // KV_HEADS
Q_BLOCK = 128
SWA_WINDOW = 128
CMP_BLOCK = 32
CMP_STRIDE = 16
SEL_BLOCK = 64
SEL_TOP_N = 8
NSA_WINDOW = 512
SEL_Q_CHUNK = 32
SEL_FORCE = 1.0e4
FORGET_BIAS_INIT = 3.0
N_EXPERTS = 32
TOP_K = 4
D_EXPERT = D_MODEL
SWIGLU_LIMIT = 7.0
SWIGLU_ALPHA = 1.702
MOE_BLOCK = 128
LN_EPS = 1e-5

FOX_W = FOX_HEADS * HEAD_DIM
SWA_QW = SWA_Q_HEADS * HEAD_DIM
NSA_QW = NSA_Q_HEADS * HEAD_DIM
KV_W = KV_HEADS * HEAD_DIM
COL_SIZES = (FOX_W, FOX_W, FOX_W, FOX_HEADS,
             SWA_QW, KV_W, KV_W,
             NSA_QW, KV_W, KV_W, KV_W, KV_W, KV_W, KV_W, 3 * NSA_Q_HEADS)
N_IN_COLS = sum(COL_SIZES)
FOX_F_OFFSET = 3 * FOX_W

kernel_name = "hybrid_fox_swa_nsa_moe_deepnorm"


def _layer_norm(x, g, b):
    xf = x.astype(jnp.float32)
    mu = jnp.mean(xf, axis=-1, keepdims=True)
    var = jnp.mean(jnp.square(xf - mu), axis=-1, keepdims=True)
    return ((xf - mu) * lax.rsqrt(var + LN_EPS) * g + b).astype(x.dtype)


def _masked_softmax(s, mask):
    m = jnp.max(jnp.where(mask, s, -jnp.inf), axis=-1, keepdims=True)
    m = jnp.where(jnp.isfinite(m), m, 0.0)
    e = jnp.where(mask, jnp.exp(s - m), 0.0)
    d = jnp.sum(e, axis=-1, keepdims=True)
    return e / jnp.where(d > 0, d, 1.0)


def _alibi_slopes():
    n = SWA_Q_HEADS + NSA_Q_HEADS
    s = 2.0 ** (-8.0 * np.arange(1, n + 1) / n)
    swa = jnp.asarray(s[0::2].reshape(KV_HEADS, GQA_GROUP), jnp.float32)
    nsa = jnp.asarray(s[1::2].reshape(KV_HEADS, GQA_GROUP), jnp.float32)
    return swa, nsa


def _heads(t, *heads):
    return t.reshape(t.shape[0], t.shape[1], *heads, HEAD_DIM)


def _forgetting_attention(q, k, v, f_logit):
    S = q.shape[1]
    scale = HEAD_DIM ** -0.5
    c = jnp.cumsum(jax.nn.log_sigmoid(f_logit.astype(jnp.float32)), axis=1)
    c = jnp.transpose(c, (0, 2, 1))
    outs = []
    for i in range(S // Q_BLOCK):
        q0, q1 = i * Q_BLOCK, (i + 1) * Q_BLOCK
        s = jnp.einsum('bqhd,bkhd->bhqk', q[:, q0:q1], k[:, :q1],
                       preferred_element_type=jnp.float32) * scale
        s = s + c[:, :, q0:q1, None] - c[:, :, None, :q1]
        causal = jnp.arange(q1)[None, :] <= jnp.arange(q0, q1)[:, None]
        p = jax.nn.softmax(jnp.where(causal, s, -jnp.inf), axis=-1)
        outs.append(jnp.einsum('bhqk,bkhd->bqhd', p.astype(v.dtype), v[:, :q1]))
    return jnp.concatenate(outs, axis=1)


def _banded_gqa(q, k, v, window, slopes, sinks=None):
    B_, S, KV, G, Dh = q.shape
    scale = Dh ** -0.5
    span = window + Q_BLOCK
    pad = ((0, 0), (window, 0), (0, 0), (0, 0))
    kp = jnp.pad(k, pad)
    vp = jnp.pad(v, pad)
    qi = jnp.arange(Q_BLOCK)[:, None]
    kj = jnp.arange(span)[None, :]
    dist = qi + window - kj
    in_band = (dist >= 0) & (dist < window)
    bias = -slopes[:, :, None, None] * dist.astype(jnp.float32)

    def block(i):
        qb = lax.dynamic_slice_in_dim(q, i * Q_BLOCK, Q_BLOCK, axis=1)
        kb = lax.dynamic_slice_in_dim(kp, i * Q_BLOCK, span, axis=1)
        vb = lax.dynamic_slice_in_dim(vp, i * Q_BLOCK, span, axis=1)
        valid = in_band & (kj + i * Q_BLOCK - window >= 0)
        s = jnp.einsum('bqhgd,bkhd->bhgqk', qb, kb,
                       preferred_element_type=jnp.float32) * scale + bias
        if sinks is None:
            p = _masked_softmax(s, valid)
        else:
            s = jnp.where(valid, s, -jnp.inf)
            sink = sinks.astype(jnp.float32)[None, :, :, None, None]
            m = jnp.maximum(jnp.max(s, axis=-1, keepdims=True), sink)
            e = jnp.exp(s - m)
            p = e / (jnp.sum(e, axis=-1, keepdims=True) + jnp.exp(sink - m))
        return jnp.einsum('bhgqk,bkhd->bqhgd', p.astype(v.dtype), vb)

    out = lax.map(block, jnp.arange(S // Q_BLOCK))
    return jnp.moveaxis(out, 0, 1).reshape(B_, S, KV, G, Dh)


def _compress(kv, pe, w1, w2):
    B_, S, KV, Dh = kv.shape
    nc = (S - CMP_BLOCK) // CMP_STRIDE + 1
    idx = np.arange(nc)[:, None] * CMP_STRIDE + np.arange(CMP_BLOCK)[None, :]
    blocks = kv[:, idx] + pe[None, None, :, None, :]
    blocks = jnp.moveaxis(blocks, 3, 2).reshape(B_, nc, KV, CMP_BLOCK * Dh)
    return jax.nn.gelu(blocks @ w1) @ w2


def _nsa(q, k_cmp, v_cmp, k_sel, v_sel, k_win, v_win, gate_logit, slopes, pe, w1, w2):
    B_, S, KV, G, Dh = q.shape
    scale = Dh ** -0.5
    t = jnp.arange(S)

    kc = _compress(k_cmp, pe[0], w1[0], w2[0])
    vc = _compress(v_cmp, pe[1], w1[1], w2[1])
    nc = kc.shape[1]
    cmp_start = np.arange(nc) * CMP_STRIDE
    cmp_end = cmp_start + CMP_BLOCK - 1
    cdist = (t[:, None] - cmp_end[None, :]).astype(jnp.float32)
    s = jnp.einsum('bthgd,bchd->bthgc', q, kc, preferred_element_type=jnp.float32) * scale
    s = s - slopes[None, None, :, :, None] * cdist[None, :, None, None, :]
    cmask = (cmp_end[None, :] <= t[:, None])[None, :, None, None, :]
    p_cmp = _masked_softmax(s, cmask)
    o_cmp = jnp.einsum('bthgc,bchd->bthgd', p_cmp.astype(vc.dtype), vc)

    ns = S // SEL_BLOCK
    sel_start = np.arange(ns) * SEL_BLOCK
    ov = np.clip(np.minimum(cmp_start[:, None] + CMP_BLOCK, sel_start[None, :] + SEL_BLOCK)
                 - np.maximum(cmp_start[:, None], sel_start[None, :]), 0, None) / CMP_BLOCK
    p_slc = jnp.einsum('bthgc,cn->bthn', p_cmp, jnp.asarray(ov, jnp.float32))
    cur = t // SEL_BLOCK
    j = jnp.arange(ns)
    forced = (j[None, :] == 0) | (j[None, :] == cur[:, None]) | (j[None, :] == cur[:, None] - 1)
    future = j[None, :] > cur[:, None]
    score = jnp.where(future[:, None, :], -1.0,
                      jnp.where(forced[:, None, :], SEL_FORCE, p_slc))
    n_sel = min(SEL_TOP_N, ns)
    _, sel_idx = lax.top_k(score, n_sel)

    ksb = jnp.transpose(k_sel.reshape(B_, ns, SEL_BLOCK, KV, Dh), (0, 3, 1, 2, 4))
    vsb = jnp.transpose(v_sel.reshape(B_, ns, SEL_BLOCK, KV, Dh), (0, 3, 1, 2, 4))
    nch = S // SEL_Q_CHUNK
    qc = jnp.transpose(q.reshape(B_, nch, SEL_Q_CHUNK, KV, G, Dh), (1, 0, 3, 2, 4, 5))
    ic = jnp.transpose(sel_idx.reshape(B_, nch, SEL_Q_CHUNK, KV, n_sel), (1, 0, 3, 2, 4))
    bi = jnp.arange(B_)[:, None, None, None]
    hi = jnp.arange(KV)[None, :, None, None]

    def chunk(args):
        qx, ix, c0 = args
        kg = ksb[bi, hi, ix]
        vg = vsb[bi, hi, ix]
        s = jnp.einsum('bhqgd,bhqnkd->bhqgnk', qx, kg,
                       preferred_element_type=jnp.float32) * scale
        tpos = c0 + jnp.arange(SEL_Q_CHUNK)
        spos = ix[..., None] * SEL_BLOCK + jnp.arange(SEL_BLOCK)
        dist = tpos[None, None, :, None, None] - spos
        s = s - slopes[None, :, None, :, None, None] * dist[:, :, :, None].astype(jnp.float32)
        s = s.reshape(B_, KV, SEL_Q_CHUNK, G, n_sel * SEL_BLOCK)
        valid = (dist >= 0).reshape(B_, KV, SEL_Q_CHUNK, 1, n_sel * SEL_BLOCK)
        p = _masked_softmax(s, valid)
        return jnp.einsum('bhqgm,bhqmd->bhqgd', p.astype(vg.dtype),
                          vg.reshape(B_, KV, SEL_Q_CHUNK, n_sel * SEL_BLOCK, Dh))

    o_sel = lax.map(chunk, (qc, ic, jnp.arange(nch) * SEL_Q_CHUNK))
    o_sel = jnp.transpose(o_sel, (1, 0, 3, 2, 4, 5)).reshape(B_, S, KV, G, Dh)

    o_win = _banded_gqa(q, k_win, v_win, NSA_WINDOW, slopes)

    g = jax.nn.sigmoid(gate_logit.reshape(B_, S, KV, G, 3))
    return g[..., 0:1] * o_cmp + g[..., 1:2] * o_sel + g[..., 2:3] * o_win


def _moe(x, router_w, router_b, w_gu, b_gu, w_dn, b_dn):
    B_, S, D = x.shape
    N = B_ * S
    NK = N * TOP_K
    xf = x.reshape(N, D)
    logits = (xf @ router_w + router_b).astype(jnp.float32)
    top_v, top_e = lax.top_k(logits, TOP_K)
    gate_w = jax.nn.softmax(top_v, axis=-1).astype(x.dtype)
    e_flat = top_e.reshape(NK)
    w_flat = gate_w.reshape(NK)
    tok = jnp.arange(NK, dtype=jnp.int32) // TOP_K
    counts = jnp.bincount(e_flat, length=N_EXPERTS)
    padded = (counts + MOE_BLOCK - 1) // MOE_BLOCK * MOE_BLOCK
    pad_end = jnp.cumsum(padded)
    pad_start = pad_end - padded
    grp_start = jnp.cumsum(counts) - counts
    order = jnp.argsort(e_flat)
    se = e_flat[order]
    dest = pad_start[se] + jnp.arange(NK) - grp_start[se]
    nblk = -(-NK // MOE_BLOCK) + N_EXPERTS
    P = nblk * MOE_BLOCK
    buf_tok = jnp.full((P,), N, jnp.int32).at[dest].set(tok[order])
    buf_w = jnp.zeros((P,), x.dtype).at[dest].set(w_flat[order])
    blk_e = jnp.minimum(jnp.searchsorted(pad_end, jnp.arange(nblk) * MOE_BLOCK, side='right'),
                        N_EXPERTS - 1)
    x_pad = jnp.concatenate([xf, jnp.zeros((1, D), x.dtype)], axis=0)

    def expert_block(args):
        e, tk = args
        h = x_pad[tk] @ w_gu[e] + b_gu[e]
        gate, up = h[:, 0::2], h[:, 1::2]
        gate = jnp.minimum(gate, SWIGLU_LIMIT)
        up = jnp.clip(up, -SWIGLU_LIMIT, SWIGLU_LIMIT)
        glu = gate * jax.nn.sigmoid(gate * SWIGLU_ALPHA)
        return ((up + 1.0) * glu) @ w_dn[e] + b_dn[e]

    y = lax.map(expert_block, (blk_e, buf_tok.reshape(nblk, MOE_BLOCK)))
    y = y.reshape(P, D) * buf_w[:, None]
    out = jnp.zeros((N + 1, D), x.dtype).at[buf_tok].add(y)[:N]
    return out.reshape(B_, S, D)


def setup_inputs(seed: int = 0) -> dict:
    key = jax.random.key(seed)
    ks = jax.random.split(key, 18)
    beta = (8.0 * DEPTH) ** -0.25
    L, D = DEPTH, D_MODEL

    def nrm(k, shape, scale):
        return scale * jax.random.normal(k, shape, jnp.float32)

    x = nrm(ks[0], (BATCH, SEQ, D), 1.0)
    w_in = nrm(ks[1], (L, D, N_IN_COLS), D ** -0.5)
    b_in = nrm(ks[2], (L, N_IN_COLS), 0.01).at[:, FOX_F_OFFSET:FOX_F_OFFSET + FOX_HEADS].add(FORGET_BIAS_INIT)
    sinks = nrm(ks[3], (L, SWA_Q_HEADS), 0.5)
    cmp_pe = nrm(ks[4], (L, 2, CMP_BLOCK, HEAD_DIM), 0.02)
    cmp_w1 = nrm(ks[5], (L, 2, CMP_BLOCK * HEAD_DIM, HEAD_DIM), (CMP_BLOCK * HEAD_DIM) ** -0.5)
    cmp_w2 = nrm(ks[6], (L, 2, HEAD_DIM, HEAD_DIM), HEAD_DIM ** -0.5)
    w_out = nrm(ks[7], (L, D, D), beta * D ** -0.5)
    ln1_g = 1.0 + nrm(ks[8], (L, D), 0.02)
    ln1_b = nrm(ks[9], (L, D), 0.02)
    router_w = nrm(ks[10], (L, D, N_EXPERTS), D ** -0.5)
    router_b = nrm(ks[11], (L, N_EXPERTS), 0.01)
    w_gate_up = nrm(ks[12], (L, N_EXPERTS, D, 2 * D_EXPERT), D ** -0.5)
    b_gate_up = nrm(ks[13], (L, N_EXPERTS, 2 * D_EXPERT), 0.01)
    w_down = nrm(ks[14], (L, N_EXPERTS, D_EXPERT, D), beta * D_EXPERT ** -0.5)
    b_down = nrm(ks[15], (L, N_EXPERTS, D), 0.01)
    ln2_g = 1.0 + nrm(ks[16], (L, D), 0.02)
    ln2_b = nrm(ks[17], (L, D), 0.02)
    return {"x": x, "w_in": w_in, "b_in": b_in, "sinks": sinks,
            "cmp_pe": cmp_pe, "cmp_w1": cmp_w1, "cmp_w2": cmp_w2, "w_out": w_out,
            "ln1_g": ln1_g, "ln1_b": ln1_b, "router_w": router_w, "router_b": router_b,
            "w_gate_up": w_gate_up, "b_gate_up": b_gate_up, "w_down": w_down,
            "b_down": b_down, "ln2_g": ln2_g, "ln2_b": ln2_b}


def reference(x, w_in, b_in, sinks, cmp_pe, cmp_w1, cmp_w2, w_out, ln1_g, ln1_b,
              router_w, router_b, w_gate_up, b_gate_up, w_down, b_down, ln2_g, ln2_b):
    alpha = (2.0 * DEPTH) ** 0.25
    B_, S, _ = x.shape
    slopes_swa, slopes_nsa = _alibi_slopes()
    splits = np.cumsum(COL_SIZES)[:-1].tolist()
    for l in range(DEPTH):
        h = x @ w_in[l] + b_in[l]
        (fq, fk, fv, ff, sq, sk, sv, nq, nkc, nvc, nks, nvs, nkw, nvw, ng) = jnp.split(h, splits, axis=-1)
        o_fox = _forgetting_attention(_heads(fq, FOX_HEADS), _heads(fk, FOX_HEADS),
                                      _heads(fv, FOX_HEADS), ff)
        o_swa = _banded_gqa(_heads(sq, KV_HEADS, GQA_GROUP), _heads(sk, KV_HEADS),
                            _heads(sv, KV_HEADS), SWA_WINDOW, slopes_swa,
                            sinks[l].reshape(KV_HEADS, GQA_GROUP))
        o_nsa = _nsa(_heads(nq, KV_HEADS, GQA_GROUP), _heads(nkc, KV_HEADS), _heads(nvc, KV_HEADS),
                     _heads(nks, KV_HEADS), _heads(nvs, KV_HEADS), _heads(nkw, KV_HEADS),
                     _heads(nvw, KV_HEADS), ng, slopes_nsa, cmp_pe[l], cmp_w1[l], cmp_w2[l])
        mix = jnp.concatenate([o_fox.reshape(B_, S, FOX_W), o_swa.reshape(B_, S, SWA_QW),
                               o_nsa.reshape(B_, S, NSA_QW)], axis=-1) @ w_out[l]
        x = _layer_norm(alpha * x + mix, ln1_g[l], ln1_b[l])
        ffn = _moe(x, router_w[l], router_b[l], w_gate_up[l], b_gate_up[l], w_down[l], b_down[l])
        x = _layer_norm(alpha * x + ffn, ln2_g[l], ln2_b[l])
    return x
```

```python
import functools

import numpy as np
import jax
import jax.numpy as jnp
from jax import lax
from jax.experimental import pallas as pl
from jax.experimental.pallas import tpu as pltpu

F32 = jnp.float32
BF16 = jnp.bfloat16
I32 = jnp.int32

HEAD_DIM = 64
LANES = 128
N_HEADS = 16
SWA_Q_HEADS = 6
NSA_Q_HEADS = 6
FOX_HEADS = 4
KV_HEADS = 2
GQA_GROUP = 3
SWA_WINDOW = 128
CMP_BLOCK = 32
CMP_STRIDE = 16
SEL_BLOCK = 64
SEL_TOP_N = 8
NSA_WINDOW = 512
SEL_FORCE = 1.0e4
N_EXPERTS = 32
TOP_K = 4
SWIGLU_LIMIT = 7.0
SWIGLU_ALPHA = 1.702
LN_EPS = 1e-5
NEG = -1.0e30

Q_TILE = 128
FOX_TILE = 256
SEL_KEYS = 512
MOE_ROWS = 512
VMEM_LIMIT = 56 * 1024 * 1024

L_BLK = 0
L_QT = 32
L_KH = 35
L_KL = 38
L_PAD = 41

C_SQ, C_NQ, C_FQ, C_FK, C_FV = 0, 384, 768, 1024, 1280
C_SK, C_SV, C_NKS, C_NVS, C_NKW, C_NVW = 1536, 1664, 1792, 1920, 2048, 2176
CB = 2304
CF = 384


def _alibi():
    n = SWA_Q_HEADS + NSA_Q_HEADS
    s = (2.0 ** (-8.0 * np.arange(1, n + 1) / n)).astype(np.float32)
    return s[0::2].reshape(KV_HEADS, GQA_GROUP), s[1::2].reshape(KV_HEADS, GQA_GROUP)


def _bf16_parts(v):
    v = np.float32(v)
    out = []
    for _ in range(3):
        p = np.float32(np.asarray(v, dtype=BF16))
        out.append(float(p))
        v = np.float32(v - p)
    return out


def _split3(x):
    hi = x.astype(BF16).astype(F32)
    r = x - hi
    mid = r.astype(BF16).astype(F32)
    lo = (r - mid).astype(BF16).astype(F32)
    return hi, mid, lo


def _lane_iota():
    return lax.broadcasted_iota(I32, (1, LANES), 1)


def _spare(kv_half):
    return HEAD_DIM if kv_half == 0 else 0


def _dot_nt(a, b):
    return lax.dot_general(a, b, (((1,), (1,)), ((), ())), preferred_element_type=F32)


def _params(*sem):
    return pltpu.CompilerParams(dimension_semantics=sem, vmem_limit_bytes=VMEM_LIMIT)


def _inproj_kernel(x_ref, w_ref, b_ref, hb_ref, hf_ref):
    acc = jnp.dot(x_ref[...].astype(BF16), w_ref[...], preferred_element_type=F32) + b_ref[...]
    hb_ref[...] = acc[:, :CB].astype(BF16)
    hf_ref[...] = acc[:, CB:]


def _inproj(x2, w, b, tm):
    n, d = x2.shape
    ct = w.shape[1]
    return pl.pallas_call(
        _inproj_kernel,
        grid=(n // tm,),
        in_specs=[pl.BlockSpec((tm, d), lambda i: (i, 0)),
                  pl.BlockSpec((d, ct), lambda i: (0, 0)),
                  pl.BlockSpec((1, ct), lambda i: (0, 0))],
        out_specs=[pl.BlockSpec((tm, CB), lambda i: (i, 0)),
                   pl.BlockSpec((tm, CF), lambda i: (i, 0))],
        out_shape=[jax.ShapeDtypeStruct((n, CB), BF16), jax.ShapeDtypeStruct((n, CF), F32)],
        compiler_params=_params("arbitrary"),
        name="inproj",
    )(x2, w, b)


def _gates_kernel(g_ref, o_ref, *, chunk):
    s = g_ref.shape[1]
    lane = _lane_iota()
    r = lax.broadcasted_iota(I32, (chunk, chunk), 0)
    c = lax.broadcasted_iota(I32, (chunk, chunk), 1)
    tri = jnp.where(c <= r, 1.0, 0.0).astype(BF16)

    def body(j, carry):
        x = g_ref[0, pl.ds(pl.multiple_of(j * chunk, chunk), chunk), :]
        ls = jnp.minimum(x, 0.0) - jnp.log1p(jnp.exp(-jnp.abs(x)))
        cs = carry
        for part in _split3(ls):
            cs = cs + jnp.dot(tri, part.astype(BF16), preferred_element_type=F32)
        sg = jax.nn.sigmoid(x)
        o_ref[0, pl.ds(pl.multiple_of(j * chunk, chunk), chunk), :] = jnp.where(lane < FOX_HEADS, cs, sg)
        return cs[chunk - 1:chunk, :]

    lax.fori_loop(0, s // chunk, body, jnp.zeros((1, LANES), F32))


def _gates(hf3):
    b, s, _ = hf3.shape
    chunk = min(256, s)
    return pl.pallas_call(
        functools.partial(_gates_kernel, chunk=chunk),
        grid=(b,),
        in_specs=[pl.BlockSpec((1, s, LANES), lambda i: (i, 0, 2))],
        out_specs=pl.BlockSpec((1, s, LANES), lambda i: (i, 0, 0)),
        out_shape=jax.ShapeDtypeStruct((b, s, LANES), F32),
        compiler_params=_params("arbitrary"),
        name="gates",
    )(hf3)


def _fox_kernel(q_ref, k_ref, v_ref, cq_ref, ck_ref, o_ref, ka_ref, *, tile):
    i = pl.program_id(1)
    s_len = k_ref.shape[1]
    lane = _lane_iota()

    @pl.when(i == 0)
    def _fill():
        for h in range(FOX_HEADS):
            p, half = divmod(h, 2)
            a0 = _spare(half)
            k = k_ref[0, :, p * LANES:(p + 1) * LANES].astype(F32)
            hi, mid, lo = _split3(-ck_ref[0, :, h:h + 1])
            ka = jnp.where((lane >= a0 + L_QT) & (lane < a0 + L_QT + 3), 1.0, k)
            ka = jnp.where(lane == a0 + L_KH, hi, ka)
            ka = jnp.where(lane == a0 + L_KH + 1, mid, ka)
            ka = jnp.where(lane == a0 + L_KH + 2, lo, ka)
            ka_ref[h] = ka.astype(BF16)

    row = lax.broadcasted_iota(I32, (tile, tile), 0)
    col = lax.broadcasted_iota(I32, (tile, tile), 1)
    causal = jnp.where(col <= row, 0.0, NEG)
    outs = []
    for h in range(FOX_HEADS):
        p, half = divmod(h, 2)
        a0 = _spare(half)
        q = q_ref[0, :, p * LANES:(p + 1) * LANES].astype(F32)
        in_half = (lane < HEAD_DIM) if half == 0 else (lane >= HEAD_DIM)
        hi, mid, lo = _split3(cq_ref[0, :, h:h + 1])
        qa = jnp.where(in_half, q, 0.0)
        qa = jnp.where(lane == a0 + L_QT, hi, qa)
        qa = jnp.where(lane == a0 + L_QT + 1, mid, qa)
        qa = jnp.where(lane == a0 + L_QT + 2, lo, qa)
        qa = jnp.where((lane >= a0 + L_KH) & (lane < a0 + L_KH + 3), 1.0, qa).astype(BF16)

        def step(j, carry, masked, h=h, p=p, qa=qa):
            m, l, acc = carry
            start = pl.multiple_of(j * tile, tile)
            sc = _dot_nt(qa, ka_ref[h, pl.ds(start, tile), :])
            if masked:
                sc = sc + causal
            m_new = jnp.maximum(m, jnp.max(sc, axis=-1, keepdims=True))
            alpha = jnp.exp(m - m_new)
            pr = jnp.exp(sc - m_new)
            l = alpha * l + jnp.sum(pr, axis=-1, keepdims=True)
            vt = v_ref[0, pl.ds(start, tile), p * LANES:(p + 1) * LANES]
            acc = alpha * acc + jnp.dot(pr.astype(BF16), vt, preferred_element_type=F32)
            return m_new, l, acc

        init = (jnp.full((tile, 1), NEG, F32), jnp.zeros((tile, 1), F32), jnp.zeros((tile, LANES), F32))
        carry = lax.fori_loop(0, i, functools.partial(step, masked=False), init)
        m, l, acc = step(i, carry, True)
        outs.append(acc / l)
    for p in range(FOX_HEADS // 2):
        o_ref[0, :, p * LANES:(p + 1) * LANES] = jnp.where(lane < HEAD_DIM, outs[2 * p], outs[2 * p + 1]).astype(BF16)


def _fox(hb3, cg):
    b, s, _ = hb3.shape
    tile = min(FOX_TILE, s)
    w = FOX_HEADS * HEAD_DIM
    return pl.pallas_call(
        functools.partial(_fox_kernel, tile=tile),
        grid=(b, s // tile),
        in_specs=[pl.BlockSpec((1, tile, w), lambda bi, i: (bi, i, C_FQ // w)),
                  pl.BlockSpec((1, s, w), lambda bi, i: (bi, 0, C_FK // w)),
                  pl.BlockSpec((1, s, w), lambda bi, i: (bi, 0, C_FV // w)),
                  pl.BlockSpec((1, tile, LANES), lambda bi, i: (bi, i, 0)),
                  pl.BlockSpec((1, s, LANES), lambda bi, i: (bi, 0, 0))],
        out_specs=pl.BlockSpec((1, tile, w), lambda bi, i: (bi, i, 0)),
        out_shape=jax.ShapeDtypeStruct((b, s, w), BF16),
        scratch_shapes=[pltpu.VMEM((FOX_HEADS, s, LANES), BF16)],
        compiler_params=_params("arbitrary", "arbitrary"),
        name="fox",
    )(hb3, hb3, hb3, cg, cg)


def _fill_banded(k_ref, v_ref, kp_ref, vp_ref, window):
    s_len = k_ref.shape[1]
    lane = _lane_iota()
    k = k_ref[0].astype(F32)
    pos = lax.broadcasted_iota(I32, (s_len, 1), 0) + window
    pos_hi = ((pos >> 8) << 8).astype(F32)
    pos_lo = (pos & 255).astype(F32)
    for kv in range(KV_HEADS):
        a0 = _spare(kv)
        ka = jnp.where((lane >= a0 + L_QT) & (lane < a0 + L_QT + 3), 1.0, k)
        ka = jnp.where((lane >= a0 + L_KH) & (lane < a0 + L_KH + 3), pos_hi, ka)
        ka = jnp.where((lane >= a0 + L_KL) & (lane < a0 + L_KL + 3), pos_lo, ka)
        ka = jnp.where(lane == a0 + L_PAD, 0.0, ka)
        kp_ref[kv, pl.ds(window, s_len), :] = ka.astype(BF16)
        pad = jnp.where(lane == a0 + L_PAD, NEG, 0.0) + jnp.zeros((window, LANES), F32)
        kp_ref[kv, pl.ds(0, window), :] = pad.astype(BF16)
    vp_ref[pl.ds(0, window), :] = jnp.zeros((window, LANES), BF16)
    vp_ref[pl.ds(window, s_len), :] = v_ref[0]


def _alibi_query(q, kv, g, t_pos, slopes, lane, pad_lane):
    a0 = _spare(kv)
    in_half = (lane < HEAD_DIM) if kv == 0 else (lane >= HEAD_DIM)
    sl = float(slopes[kv][g])
    hi, mid, lo = _split3(t_pos * (-sl))
    qa = jnp.where(in_half, q.astype(F32), 0.0)
    qa = jnp.where(lane == a0 + L_QT, hi, qa)
    qa = jnp.where(lane == a0 + L_QT + 1, mid, qa)
    qa = jnp.where(lane == a0 + L_QT + 2, lo, qa)
    for c, part in enumerate(_bf16_parts(sl)):
        qa = jnp.where((lane == a0 + L_KH + c) | (lane == a0 + L_KL + c), part, qa)
    if pad_lane:
        qa = jnp.where(lane == a0 + L_PAD, 1.0, qa)
    return qa


def _banded_attend(q, kp_ref, vp_ref, i, window, slopes, sink_ref):
    tq = q.shape[0]
    span = window + tq
    lane = _lane_iota()
    start = pl.multiple_of(i * tq, tq)
    t_pad = (lax.broadcasted_iota(I32, (tq, 1), 0) + i * tq + window).astype(F32)
    rr = lax.broadcasted_iota(I32, (tq, span), 0)
    cc = lax.broadcasted_iota(I32, (tq, span), 1)
    band = jnp.where((cc > rr) & (cc <= rr + window), 0.0, NEG)
    outs = []
    for kv in range(KV_HEADS):
        qs = [_alibi_query(q[:, g * LANES:(g + 1) * LANES], kv, g, t_pad, slopes, lane, True).astype(BF16)
              for g in range(GQA_GROUP)]
        qst = jnp.concatenate(qs, axis=0)
        sc = _dot_nt(qst, kp_ref[kv, pl.ds(start, span), :])
        sc = (sc.reshape(GQA_GROUP, tq, span) + band[None]).reshape(GQA_GROUP * tq, span)
        m = jnp.max(sc, axis=-1, keepdims=True)
        if sink_ref is not None:
            sink = jnp.concatenate([jnp.full((tq, 1), sink_ref[kv * GQA_GROUP + g], F32)
                                    for g in range(GQA_GROUP)], axis=0)
            m = jnp.maximum(m, sink)
        pr = jnp.exp(sc - m)
        den = jnp.sum(pr, axis=-1, keepdims=True)
        if sink_ref is not None:
            den = den + jnp.exp(sink - m)
        o = jnp.dot(pr.astype(BF16), vp_ref[pl.ds(start, span), :], preferred_element_type=F32)
        outs.append(o / den)
    return outs


def _merge_kv(outs, g, tq, lane):
    return jnp.where(lane < HEAD_DIM, outs[0][g * tq:(g + 1) * tq], outs[1][g * tq:(g + 1) * tq])


def _swa_kernel(sink_ref, q_ref, k_ref, v_ref, o_ref, kp_ref, vp_ref, *, slopes):
    i = pl.program_id(1)
    tq = q_ref.shape[1]

    @pl.when(i == 0)
    def _fill():
        _fill_banded(k_ref, v_ref, kp_ref, vp_ref, SWA_WINDOW)

    outs = _banded_attend(q_ref[0], kp_ref, vp_ref, i, SWA_WINDOW, slopes, sink_ref)
    lane = _lane_iota()
    for g in range(GQA_GROUP):
        o_ref[0, :, g * LANES:(g + 1) * LANES] = _merge_kv(outs, g, tq, lane).astype(BF16)


def _swa(hb3, sinks, slopes):
    b, s, _ = hb3.shape
    w = GQA_GROUP * LANES
    return pl.pallas_call(
        functools.partial(_swa_kernel, slopes=slopes),
        grid=(b, s // Q_TILE),
        in_specs=[pl.BlockSpec(memory_space=pltpu.SMEM),
                  pl.BlockSpec((1, Q_TILE, w), lambda bi, i: (bi, i, C_SQ // w)),
                  pl.BlockSpec((1, s, LANES), lambda bi, i: (bi, 0, C_SK // LANES)),
                  pl.BlockSpec((1, s, LANES), lambda bi, i: (bi, 0, C_SV // LANES))],
        out_specs=pl.BlockSpec((1, Q_TILE, w), lambda bi, i: (bi, i, 0)),
        out_shape=jax.ShapeDtypeStruct((b, s, w), BF16),
        scratch_shapes=[pltpu.VMEM((KV_HEADS, s + SWA_WINDOW, LANES), BF16),
                        pltpu.VMEM((s + SWA_WINDOW, LANES), BF16)],
        compiler_params=_params("arbitrary", "arbitrary"),
        name="swa",
    )(sinks, hb3, hb3, hb3)


def _compress_kernel(g_ref, pe_ref, w1_ref, w2_ref, kc_ref, vc_ref):
    nc = g_ref.shape[1]
    for which, out_ref in ((0, kc_ref), (1, vc_ref)):
        ha = jnp.zeros((nc, LANES), F32)
        hb = jnp.zeros((nc, LANES), F32)
        for l in range(CMP_STRIDE):
            x = g_ref[0, :, (2 * l + which) * LANES:(2 * l + which + 1) * LANES]
            xa = (x + pe_ref[which, l:l + 1, :]).astype(BF16)
            xb = (x + pe_ref[which, CMP_STRIDE + l:CMP_STRIDE + l + 1, :]).astype(BF16)
            ha = ha + jnp.dot(xa, w1_ref[which, l], preferred_element_type=F32)
            hb = hb + jnp.dot(xb, w1_ref[which, CMP_STRIDE + l], preferred_element_type=F32)
        hid = ha + pltpu.roll(hb, nc - 1, 0)
        act = 0.5 * hid * (1.0 + jnp.tanh(np.sqrt(2.0 / np.pi).astype(np.float32) * (hid + 0.044715 * (hid * hid * hid))))
        out_ref[0] = jnp.dot(act.astype(BF16), w2_ref[which], preferred_element_type=F32).astype(BF16)


def _compress(g, pe2, w1b, w2b):
    b, nc, gw = g.shape
    return pl.pallas_call(
        _compress_kernel,
        grid=(b,),
        in_specs=[pl.BlockSpec((1, nc, gw), lambda i: (i, 0, 0)),
                  pl.BlockSpec(pe2.shape, lambda i: (0, 0, 0)),
                  pl.BlockSpec(w1b.shape, lambda i: (0, 0, 0, 0)),
                  pl.BlockSpec(w2b.shape, lambda i: (0, 0, 0))],
        out_specs=[pl.BlockSpec((1, nc, LANES), lambda i: (i, 0, 0)),
                   pl.BlockSpec((1, nc, LANES), lambda i: (i, 0, 0))],
        out_shape=[jax.ShapeDtypeStruct((b, nc, LANES), BF16), jax.ShapeDtypeStruct((b, nc, LANES), BF16)],
        compiler_params=_params("arbitrary"),
        name="compress",
    )(g, pe2, w1b, w2b)


def _nsa_kernel(q_ref, ks_ref, vs_ref, kw_ref, vw_ref, kc_ref, vc_ref, cg_ref, ov_ref, o_ref,
                ksa_ref, kwp_ref, vwp_ref, kca_ref, *, slopes, n_sel):
    i = pl.program_id(1)
    tq = q_ref.shape[1]
    s_len = ks_ref.shape[1]
    nc = kc_ref.shape[1]
    ns = s_len // SEL_BLOCK
    lane = _lane_iota()

    @pl.when(i == 0)
    def _fill():
        _fill_banded(kw_ref, vw_ref, kwp_ref, vwp_ref, NSA_WINDOW)
        ks = ks_ref[0].astype(F32)
        pos = lax.broadcasted_iota(I32, (s_len, 1), 0)
        pos_hi = ((pos >> 8) << 8).astype(F32)
        pos_lo = (pos & 255).astype(F32)
        blk = pos // SEL_BLOCK
        kc = kc_ref[0].astype(F32)
        cend = lax.broadcasted_iota(I32, (nc, 1), 0) * CMP_STRIDE + (CMP_BLOCK - 1)
        cend_hi = ((cend >> 8) << 8).astype(F32)
        cend_lo = (cend & 255).astype(F32)
        for kv in range(KV_HEADS):
            a0 = _spare(kv)
            ka = jnp.where((lane >= a0 + L_BLK) & (lane < a0 + L_BLK + 32),
                           jnp.where(lane - (a0 + L_BLK) == blk, 1.0, 0.0), ks)
            ka = jnp.where((lane >= a0 + L_QT) & (lane < a0 + L_QT + 3), 1.0, ka)
            ka = jnp.where((lane >= a0 + L_KH) & (lane < a0 + L_KH + 3), pos_hi, ka)
            ka = jnp.where((lane >= a0 + L_KL) & (lane < a0 + L_KL + 3), pos_lo, ka)
            ksa_ref[kv] = ka.astype(BF16)
            ca = jnp.where((lane >= a0 + L_QT) & (lane < a0 + L_QT + 3), 1.0, kc)
            ca = jnp.where((lane >= a0 + L_KH) & (lane < a0 + L_KH + 3), cend_hi, ca)
            ca = jnp.where((lane >= a0 + L_KL) & (lane < a0 + L_KL + 3), cend_lo, ca)
            kca_ref[kv] = ca.astype(BF16)

    q = q_ref[0]
    t_i = lax.broadcasted_iota(I32, (tq, 1), 0) + i * tq
    t_f = t_i.astype(F32)
    cur = t_i // SEL_BLOCK
    cend_row = lax.broadcasted_iota(I32, (tq, nc), 1) * CMP_STRIDE + (CMP_BLOCK - 1)
    cvalid = cend_row <= t_i
    nfull = (i * tq) // SEL_KEYS
    dstart = pl.multiple_of(nfull * SEL_KEYS, SEL_KEYS)
    drow = lax.broadcasted_iota(I32, (tq, SEL_KEYS), 0) + i * tq
    dcol = lax.broadcasted_iota(I32, (tq, SEL_KEYS), 1) + nfull * SEL_KEYS
    dcausal = jnp.where(dcol <= drow, 0.0, NEG)

    o_cmp, o_sel = [], []
    for kv in range(KV_HEADS):
        a0 = _spare(kv)
        qbase = [_alibi_query(q[:, g * LANES:(g + 1) * LANES], kv, g, t_f, slopes, lane, False)
                 for g in range(GQA_GROUP)]

        qst = jnp.concatenate([x.astype(BF16) for x in qbase], axis=0)
        sc = _dot_nt(qst, kca_ref[kv]).reshape(GQA_GROUP, tq, nc)
        sc = jnp.where(cvalid[None], sc, NEG)
        m = jnp.max(sc, axis=-1, keepdims=True)
        m = jnp.where(m > 0.5 * NEG, m, 0.0)
        e = jnp.where(cvalid[None], jnp.exp(sc - m), 0.0)
        den = jnp.sum(e, axis=-1, keepdims=True)
        pc = (e / jnp.where(den > 0.0, den, 1.0)).astype(BF16)
        o_cmp.append(jnp.dot(pc.reshape(GQA_GROUP * tq, nc), vc_ref[0], preferred_element_type=F32))

        p_slc = jnp.zeros((tq, LANES), F32)
        for g in range(GQA_GROUP):
            p_slc = p_slc + jnp.dot(pc[g], ov_ref[kv], preferred_element_type=F32)
        jb = lane - (a0 + L_BLK)
        in_blk = (jb >= 0) & (jb < ns)
        forced = (jb == 0) | (jb == cur) | (jb == cur - 1)
        score = jnp.where(jb > cur, -1.0, jnp.where(forced, SEL_FORCE, p_slc))
        score = jnp.where(in_blk, score, -2.0)
        rank = jnp.zeros((tq, LANES), F32)
        for b2 in range(ns):
            colv = score[:, a0 + L_BLK + b2:a0 + L_BLK + b2 + 1]
            beats = (colv > score) | ((colv == score) & (jb > b2))
            rank = rank + jnp.where(beats, 1.0, 0.0)
        selq = jnp.where(rank < float(n_sel), 0.0, NEG)

        qsel = jnp.concatenate([jnp.where(in_blk, selq, x).astype(BF16) for x in qbase], axis=0)

        def step(start, carry, bias, kv=kv, qsel=qsel):
            m, l, acc = carry
            sc = _dot_nt(qsel, ksa_ref[kv, pl.ds(start, SEL_KEYS), :])
            if bias is not None:
                sc = (sc.reshape(GQA_GROUP, tq, SEL_KEYS) + bias[None]).reshape(GQA_GROUP * tq, SEL_KEYS)
            m_new = jnp.maximum(m, jnp.max(sc, axis=-1, keepdims=True))
            alpha = jnp.exp(m - m_new)
            pr = jnp.exp(sc - m_new)
            l = alpha * l + jnp.sum(pr, axis=-1, keepdims=True)
            acc = alpha * acc + jnp.dot(pr.astype(BF16), vs_ref[0, pl.ds(start, SEL_KEYS), :],
                                        preferred_element_type=F32)
            return m_new, l, acc

        rows = GQA_GROUP * tq
        init = (jnp.full((rows, 1), NEG, F32), jnp.zeros((rows, 1), F32), jnp.zeros((rows, LANES), F32))
        carry = lax.fori_loop(
            0, nfull, lambda j, c: step(pl.multiple_of(j * SEL_KEYS, SEL_KEYS), c, None), init)
        m, l, acc = step(dstart, carry, dcausal)
        o_sel.append(acc / l)

    o_win = _banded_attend(q, kwp_ref, vwp_ref, i, NSA_WINDOW, slopes, None)

    gates = cg_ref[0]
    for g in range(GQA_GROUP):
        total = jnp.zeros((tq, LANES), F32)
        for r, branch in enumerate((o_cmp, o_sel, o_win)):
            c0 = FOX_HEADS + (r * GQA_GROUP + g) * KV_HEADS
            gate = jnp.where(lane < HEAD_DIM, gates[:, c0:c0 + 1], gates[:, c0 + 1:c0 + 2])
            total = total + gate * _merge_kv(branch, g, tq, lane)
        o_ref[0, :, g * LANES:(g + 1) * LANES] = total.astype(BF16)


def _nsa(hb3, kc, vc, cg, ov, slopes):
    b, s, _ = hb3.shape
    nc = kc.shape[1]
    w = GQA_GROUP * LANES
    n_sel = min(SEL_TOP_N, s // SEL_BLOCK)
    kvspec = lambda c: pl.BlockSpec((1, s, LANES), lambda bi, i: (bi, 0, c // LANES))
    return pl.pallas_call(
        functools.partial(_nsa_kernel, slopes=slopes, n_sel=n_sel),
        grid=(b, s // Q_TILE),
        in_specs=[pl.BlockSpec((1, Q_TILE, w), lambda bi, i: (bi, i, C_NQ // w)),
                  kvspec(C_NKS), kvspec(C_NVS), kvspec(C_NKW), kvspec(C_NVW),
                  pl.BlockSpec((1, nc, LANES), lambda bi, i: (bi, 0, 0)),
                  pl.BlockSpec((1, nc, LANES), lambda bi, i: (bi, 0, 0)),
                  pl.BlockSpec((1, Q_TILE, LANES), lambda bi, i: (bi, i, 0)),
                  pl.BlockSpec(ov.shape, lambda bi, i: (0, 0, 0))],
        out_specs=pl.BlockSpec((1, Q_TILE, w), lambda bi, i: (bi, i, 0)),
        out_shape=jax.ShapeDtypeStruct((b, s, w), BF16),
        scratch_shapes=[pltpu.VMEM((KV_HEADS, s, LANES), BF16),
                        pltpu.VMEM((KV_HEADS, s + NSA_WINDOW, LANES), BF16),
                        pltpu.VMEM((s + NSA_WINDOW, LANES), BF16),
                        pltpu.VMEM((KV_HEADS, nc, LANES), BF16)],
        compiler_params=_params("arbitrary", "arbitrary"),
        name="nsa",
    )(hb3, hb3, hb3, hb3, hb3, kc, vc, cg, ov)


def _layer_norm(y, g, b):
    mu = jnp.mean(y, axis=-1, keepdims=True)
    var = jnp.mean(jnp.square(y - mu), axis=-1, keepdims=True)
    return (y - mu) * lax.rsqrt(var + LN_EPS) * g + b


def _mix_kernel(x_ref, of_ref, os_ref, on_ref, wf_ref, ws_ref, wn_ref, g_ref, b_ref, rw_ref, rb_ref,
                x1_ref, ri_ref, rwt_ref, cnt_ref, carry_ref, *, alpha):
    i = pl.program_id(0)
    tm = x_ref.shape[0]
    lane = _lane_iota()

    @pl.when(i == 0)
    def _init():
        carry_ref[...] = jnp.zeros_like(carry_ref)

    mix = jnp.dot(of_ref[...], wf_ref[...], preferred_element_type=F32)
    mix = mix + jnp.dot(os_ref[...], ws_ref[...], preferred_element_type=F32)
    mix = mix + jnp.dot(on_ref[...], wn_ref[...], preferred_element_type=F32)
    x1 = _layer_norm(alpha * x_ref[...] + mix, g_ref[...], b_ref[...])
    x1_ref[...] = x1

    logits = jnp.dot(x1.astype(BF16), rw_ref[...], preferred_element_type=F32) + rb_ref[...]
    v = jnp.where(lane < N_EXPERTS, logits, NEG)
    tops, hots = [], []
    for _ in range(TOP_K):
        mx = jnp.max(v, axis=-1, keepdims=True)
        idx = jnp.min(jnp.where(v == mx, lane, LANES), axis=-1, keepdims=True)
        hot = lane == idx
        tops.append((mx, idx))
        hots.append(hot)
        v = jnp.where(hot, 2.0 * NEG, v)
    es = [jnp.exp(mx - tops[0][0]) for mx, _ in tops]
    den = es[0] + es[1] + es[2] + es[3]

    member = jnp.zeros((tm, LANES), F32)
    for hot in hots:
        member = member + jnp.where(hot, 1.0, 0.0)
    r = lax.broadcasted_iota(I32, (tm, tm), 0)
    c = lax.broadcasted_iota(I32, (tm, tm), 1)
    strict = jnp.where(c < r, 1.0, 0.0).astype(BF16)
    before = jnp.dot(strict, member.astype(BF16), preferred_element_type=F32) + carry_ref[0:1, :]
    ri = jnp.zeros((tm, LANES), F32)
    rwt = jnp.zeros((tm, LANES), F32)
    for k in range(TOP_K):
        rank = jnp.sum(jnp.where(hots[k], before, 0.0), axis=-1, keepdims=True)
        ri = jnp.where(lane == k, tops[k][1].astype(F32), ri)
        ri = jnp.where(lane == TOP_K + k, rank, ri)
        rwt = jnp.where(lane == k, es[k] / den, rwt)
    ri_ref[...] = ri.astype(I32)
    rwt_ref[...] = rwt
    new = carry_ref[0:1, :] + jnp.sum(member, axis=0, keepdims=True)
    carry_ref[...] = jnp.zeros_like(carry_ref) + new
    cnt_ref[...] = jnp.zeros_like(cnt_ref) + new


def _mix(x2, of, osw, on, wf, ws, wn, g, b, rw, rb, alpha, tm):
    n, d = x2.shape
    full = lambda a: pl.BlockSpec(a.shape, lambda i: (0,) * a.ndim)
    rows = lambda a: pl.BlockSpec((tm, a.shape[1]), lambda i: (i, 0))
    return pl.pallas_call(
        functools.partial(_mix_kernel, alpha=alpha),
        grid=(n // tm,),
        in_specs=[rows(x2), rows(of), rows(osw), rows(on), full(wf), full(ws), full(wn),
                  full(g), full(b), full(rw), full(rb)],
        out_specs=[pl.BlockSpec((tm, d), lambda i: (i, 0)),
                   pl.BlockSpec((tm, LANES), lambda i: (i, 0)),
                   pl.BlockSpec((tm, LANES), lambda i: (i, 0)),
                   pl.BlockSpec((8, LANES), lambda i: (0, 0))],
        out_shape=[jax.ShapeDtypeStruct((n, d), F32), jax.ShapeDtypeStruct((n, LANES), I32),
                   jax.ShapeDtypeStruct((n, LANES), F32), jax.ShapeDtypeStruct((8, LANES), F32)],
        scratch_shapes=[pltpu.VMEM((8, LANES), F32)],
        compiler_params=_params("arbitrary"),
        name="mix_ln_router",
    )(x2, of, osw, on, wf, ws, wn, g, b, rw, rb)


def _dispatch_kernel(dest_ref, x_hbm, zeros_hbm, xs_hbm, sem, *, tokens):
    del zeros_hbm
    base = pl.program_id(0) * tokens

    def issue(t, _):
        for k in range(TOP_K):
            pltpu.make_async_copy(x_hbm.at[pl.ds(base + t, 1)],
                                  xs_hbm.at[pl.ds(dest_ref[t * TOP_K + k], 1)], sem).start()
        return 0

    lax.fori_loop(0, tokens, issue, 0)
    pltpu.make_async_copy(x_hbm.at[pl.ds(0, tokens * TOP_K)], xs_hbm.at[pl.ds(0, tokens * TOP_K)], sem).wait()


def _dispatch(dest_flat, x1, rows_total, tokens):
    n, d = x1.shape
    zeros = jnp.zeros((rows_total, d), F32)
    return pl.pallas_call(
        functools.partial(_dispatch_kernel, tokens=tokens),
        grid=(n // tokens,),
        in_specs=[pl.BlockSpec((tokens * TOP_K,), lambda i: (i,), memory_space=pltpu.SMEM),
                  pl.BlockSpec(memory_space=pl.ANY),
                  pl.BlockSpec(memory_space=pl.ANY)],
        out_specs=pl.BlockSpec(memory_space=pl.ANY),
        out_shape=jax.ShapeDtypeStruct((rows_total, d), F32),
        scratch_shapes=[pltpu.SemaphoreType.DMA(())],
        input_output_aliases={2: 0},
        compiler_params=_params("arbitrary"),
        name="moe_dispatch",
    )(dest_flat, x1, zeros)


def _ffn_kernel(be_ref, nu_ref, x_ref, wg_ref, wu_ref, bg_ref, bu_ref, wd_ref, bd_ref, y_ref):
    i = pl.program_id(0)

    @pl.when(i < nu_ref[0])
    def _run():
        x = x_ref[...].astype(BF16)
        gate = jnp.dot(x, wg_ref[0], preferred_element_type=F32) + bg_ref[0]
        up = jnp.dot(x, wu_ref[0], preferred_element_type=F32) + bu_ref[0]
        gate = jnp.minimum(gate, SWIGLU_LIMIT)
        up = jnp.clip(up, -SWIGLU_LIMIT, SWIGLU_LIMIT)
        glu = gate * jax.nn.sigmoid(gate * SWIGLU_ALPHA)
        act = ((up + 1.0) * glu).astype(BF16)
        y_ref[...] = jnp.dot(act, wd_ref[0], preferred_element_type=F32) + bd_ref[0]

    @pl.when(i >= nu_ref[0])
    def _skip():
        y_ref[...] = jnp.zeros_like(y_ref)


def _ffn(blk_e, n_used, xs, wg, wu, bg, bu, wd, bd, rows):
    p, d = xs.shape
    f = wg.shape[2]
    wspec = lambda a: pl.BlockSpec((1,) + a.shape[1:], lambda i, be, nu: (be[i], 0, 0))
    grid_spec = pltpu.PrefetchScalarGridSpec(
        num_scalar_prefetch=2,
        grid=(p // rows,),
        in_specs=[pl.BlockSpec((rows, d), lambda i, be, nu: (i, 0)),
                  wspec(wg), wspec(wu), wspec(bg), wspec(bu), wspec(wd), wspec(bd)],
        out_specs=pl.BlockSpec((rows, d), lambda i, be, nu: (i, 0)),
    )
    return pl.pallas_call(
        _ffn_kernel,
        grid_spec=grid_spec,
        out_shape=jax.ShapeDtypeStruct((p, d), F32),
        compiler_params=_params("arbitrary"),
        name="moe_ffn",
    )(blk_e, n_used, xs, wg, wu, bg, bu, wd, bd)


def _combine_kernel(dest_ref, x1_ref, w_ref, g_ref, b_ref, y_hbm, o_ref, buf_ref, sem, *, alpha):
    tokens = x1_ref.shape[0]

    def issue(t, _):
        for k in range(TOP_K):
            pltpu.make_async_copy(y_hbm.at[pl.ds(dest_ref[t * TOP_K + k], 1)],
                                  buf_ref.at[k, pl.ds(t, 1)], sem).start()
        return 0

    lax.fori_loop(0, tokens, issue, 0)
    for k in range(TOP_K):
        pltpu.make_async_copy(y_hbm.at[pl.ds(0, tokens)], buf_ref.at[k], sem).wait()
    w = w_ref[...]
    ffn = jnp.zeros(x1_ref.shape, F32)
    for k in range(TOP_K):
        ffn = ffn + buf_ref[k] * w[:, k:k + 1]
    o_ref[...] = _layer_norm(alpha * x1_ref[...] + ffn, g_ref[...], b_ref[...])


def _combine(dest_flat, x1, rwt, g, b, y, alpha, tokens):
    n, d = x1.shape
    full = lambda a: pl.BlockSpec(a.shape, lambda i: (0,) * a.ndim)
    return pl.pallas_call(
        functools.partial(_combine_kernel, alpha=alpha),
        grid=(n // tokens,),
        in_specs=[pl.BlockSpec((tokens * TOP_K,), lambda i: (i,), memory_space=pltpu.SMEM),
                  pl.BlockSpec((tokens, d), lambda i: (i, 0)),
                  pl.BlockSpec((tokens, LANES), lambda i: (i, 0)),
                  full(g), full(b),
                  pl.BlockSpec(memory_space=pl.ANY)],
        out_specs=pl.BlockSpec((tokens, d), lambda i: (i, 0)),
        out_shape=jax.ShapeDtypeStruct((n, d), F32),
        scratch_shapes=[pltpu.VMEM((TOP_K, tokens, d), F32), pltpu.SemaphoreType.DMA(())],
        compiler_params=_params("arbitrary"),
        name="moe_combine_ln",
    )(dest_flat, x1, rwt, g, b, y)


def _gqa_cols(base):
    idx = np.empty(GQA_GROUP * KV_HEADS * HEAD_DIM, np.int64)
    for g in range(GQA_GROUP):
        for kv in range(KV_HEADS):
            dst = (g * KV_HEADS + kv) * HEAD_DIM
            src = (kv * GQA_GROUP + g) * HEAD_DIM
            idx[dst:dst + HEAD_DIM] = base + src + np.arange(HEAD_DIM)
    return idx


def _inproj_layout():
    fw = FOX_HEADS * HEAD_DIM
    qw = SWA_Q_HEADS * HEAD_DIM
    kw = KV_HEADS * HEAD_DIM
    sizes = (fw, fw, fw, FOX_HEADS, qw, kw, kw, qw, kw, kw, kw, kw, kw, kw, 3 * NSA_Q_HEADS)
    off = np.concatenate([[0], np.cumsum(sizes)])
    (o_fq, o_fk, o_fv, o_ff, o_sq, o_sk, o_sv, o_nq, o_nkc, o_nvc, o_nks, o_nvs, o_nkw, o_nvw, o_ng, total) = off
    ar = np.arange
    cols = np.concatenate([
        _gqa_cols(o_sq), _gqa_cols(o_nq), o_fq + ar(fw), o_fk + ar(fw), o_fv + ar(fw),
        o_sk + ar(kw), o_sv + ar(kw), o_nks + ar(kw), o_nvs + ar(kw), o_nkw + ar(kw), o_nvw + ar(kw),
        o_nkc + ar(kw), o_nvc + ar(kw)])
    gate = np.full(LANES, total, np.int64)
    gate[:FOX_HEADS] = o_ff + ar(FOX_HEADS)
    for r in range(3):
        for g in range(GQA_GROUP):
            for kv in range(KV_HEADS):
                gate[FOX_HEADS + (r * GQA_GROUP + g) * KV_HEADS + kv] = o_ng + (kv * GQA_GROUP + g) * 3 + r
    cols = np.concatenate([cols, gate])
    scale = np.ones(cols.shape[0], np.float32)
    scale[C_SQ:C_SQ + qw] = HEAD_DIM ** -0.5
    scale[C_NQ:C_NQ + qw] = HEAD_DIM ** -0.5
    scale[C_FQ:C_FQ + fw] = HEAD_DIM ** -0.5
    return cols, scale


def _outproj_rows():
    fw = FOX_HEADS * HEAD_DIM
    qw = SWA_Q_HEADS * HEAD_DIM
    return np.arange(fw), _gqa_cols(fw), _gqa_cols(fw + qw)


def _overlap_matrix(s_len, nc):
    ns = s_len // SEL_BLOCK
    cmp_start = np.arange(nc) * CMP_STRIDE
    sel_start = np.arange(ns) * SEL_BLOCK
    ov = np.clip(np.minimum(cmp_start[:, None] + CMP_BLOCK, sel_start[None, :] + SEL_BLOCK)
                 - np.maximum(cmp_start[:, None], sel_start[None, :]), 0, None) / CMP_BLOCK
    ov[(s_len - CMP_BLOCK) // CMP_STRIDE + 1:] = 0.0
    out = np.zeros((KV_HEADS, nc, LANES), np.float32)
    for kv in range(KV_HEADS):
        a0 = _spare(kv) + L_BLK
        out[kv, :, a0:a0 + ns] = ov
    return out


def _compress_weights(pe, w1, w2):
    pe2 = jnp.concatenate([pe, pe], axis=-1)
    w1r = w1.reshape(2, CMP_BLOCK, HEAD_DIM, HEAD_DIM)
    z = jnp.zeros_like(w1r)
    w1b = jnp.concatenate([jnp.concatenate([w1r, z], axis=-1), jnp.concatenate([z, w1r], axis=-1)], axis=-2)
    z2 = jnp.zeros_like(w2)
    w2b = jnp.concatenate([jnp.concatenate([w2, z2], axis=-1), jnp.concatenate([z2, w2], axis=-1)], axis=-2)
    return pe2, w1b.astype(BF16), w2b.astype(BF16)


def kernel(x, w_in, b_in, sinks, cmp_pe, cmp_w1, cmp_w2, w_out, ln1_g, ln1_b, router_w, router_b,
           w_gate_up, b_gate_up, w_down, b_down, ln2_g, ln2_b):
    depth = w_in.shape[0]
    bsz, s_len, d = x.shape
    n = bsz * s_len
    nk = n * TOP_K
    alpha = float((2.0 * depth) ** 0.25)
    assert s_len % SEL_KEYS == 0 and s_len // SEL_BLOCK <= 32 and n % 512 == 0
    slopes_swa, slopes_nsa = _alibi()
    cols, scale = _inproj_layout()
    rows_f, rows_s, rows_n = _outproj_rows()
    nc = s_len // CMP_STRIDE
    ov = jnp.asarray(_overlap_matrix(s_len, nc), BF16)
    nblk = -(-nk // MOE_ROWS) + N_EXPERTS
    p_rows = nblk * MOE_ROWS
    f = w_down.shape[2]

    x2 = x.reshape(n, d)
    for l in range(depth):
        w_ext = jnp.concatenate([w_in[l], jnp.zeros((d, 1), F32)], axis=1)
        b_ext = jnp.concatenate([b_in[l], jnp.zeros((1,), F32)])
        w_l = (jnp.take(w_ext, cols, axis=1) * scale).astype(BF16)
        b_l = (jnp.take(b_ext, cols) * scale).reshape(1, -1)
        hb, hf = _inproj(x2, w_l, b_l, 512)
        hb3 = hb.reshape(bsz, s_len, CB)
        hf3 = hf.reshape(bsz, s_len, CF)
        cg = _gates(hf3)
        o_fox = _fox(hb3, cg)
        o_swa = _swa(hb3, sinks[l], slopes_swa)
        grp = hf3[:, :, :2 * LANES].reshape(bsz, nc, CMP_STRIDE * 2 * LANES)
        pe2, w1b, w2b = _compress_weights(cmp_pe[l], cmp_w1[l], cmp_w2[l])
        kc, vc = _compress(grp, pe2, w1b, w2b)
        o_nsa = _nsa(hb3, kc, vc, cg, ov, slopes_nsa)

        wo = w_out[l].astype(BF16)
        rw = jnp.concatenate([router_w[l], jnp.zeros((d, LANES - N_EXPERTS), F32)], axis=1).astype(BF16)
        rb = jnp.concatenate([router_b[l], jnp.zeros((LANES - N_EXPERTS,), F32)]).reshape(1, LANES)
        x1, ri, rwt, cnt = _mix(
            x2, o_fox.reshape(n, -1), o_swa.reshape(n, -1), o_nsa.reshape(n, -1),
            jnp.take(wo, rows_f, axis=0), jnp.take(wo, rows_s, axis=0), jnp.take(wo, rows_n, axis=0),
            ln1_g[l].reshape(1, d), ln1_b[l].reshape(1, d), rw, rb, alpha, 256)

        counts = cnt[0, :N_EXPERTS].astype(I32)
        padded = (counts + MOE_ROWS - 1) // MOE_ROWS * MOE_ROWS
        pad_end = jnp.cumsum(padded)
        pad_start = pad_end - padded
        dest = (jnp.take(pad_start, ri[:, :TOP_K]) + ri[:, TOP_K:2 * TOP_K]).reshape(nk)
        blk_e = jnp.minimum(jnp.searchsorted(pad_end, jnp.arange(nblk, dtype=I32) * MOE_ROWS, side='right'),
                            N_EXPERTS - 1).astype(I32)
        n_used = (pad_end[-1:] // MOE_ROWS).astype(I32)

        xs = _dispatch(dest, x1, p_rows, 256)
        wgu = w_gate_up[l].reshape(N_EXPERTS, d, f, 2)
        bgu = b_gate_up[l].reshape(N_EXPERTS, 1, f, 2)
        y = _ffn(blk_e, n_used, xs, wgu[..., 0].astype(BF16), wgu[..., 1].astype(BF16),
                 bgu[..., 0], bgu[..., 1], w_down[l].astype(BF16), b_down[l].reshape(N_EXPERTS, 1, d),
                 MOE_ROWS)
        x2 = _combine(dest, x1, rwt, ln2_g[l].reshape(1, d), ln2_b[l].reshape(1, d), y, alpha, 128)
    return x2.reshape(bsz, s_len, d)
```

```python
import functools

import numpy as np
import jax
import jax.numpy as jnp
from jax import lax
from jax.experimental import pallas as pl
from jax.experimental.pallas import tpu as pltpu

F32 = jnp.float32
BF16 = jnp.bfloat16
I32 = jnp.int32

HEAD_DIM = 64
LANES = 128
N_HEADS = 16
SWA_Q_HEADS = 6
NSA_Q_HEADS = 6
FOX_HEADS = 4
KV_HEADS = 2
GQA_GROUP = 3
SWA_WINDOW = 128
CMP_BLOCK = 32
CMP_STRIDE = 16
SEL_BLOCK = 64
SEL_TOP_N = 8
NSA_WINDOW = 512
SEL_FORCE = 1.0e4
N_EXPERTS = 32
TOP_K = 4
SWIGLU_LIMIT = 7.0
SWIGLU_ALPHA = 1.702
LN_EPS = 1e-5
NEG = -1.0e30

Q_TILE = 128
FOX_TILE = 256
SEL_KEYS = 512
MAX_SEL_BLOCKS = 32
MOE_ROWS = 512
VMEM_LIMIT = 56 * 1024 * 1024

L_BLK = 0
L_QT = 32
L_KH = 35
L_KL = 38
L_PAD = 41

C_SQ, C_NQ, C_FQ, C_FK, C_FV = 0, 384, 768, 1024, 1280
C_SK, C_SV, C_NKS, C_NVS, C_NKW, C_NVW = 1536, 1664, 1792, 1920, 2048, 2176
CB = 2304
CF = 384


def _alibi():
    n = SWA_Q_HEADS + NSA_Q_HEADS
    s = (2.0 ** (-8.0 * np.arange(1, n + 1) / n)).astype(np.float32)
    return s[0::2].reshape(KV_HEADS, GQA_GROUP), s[1::2].reshape(KV_HEADS, GQA_GROUP)


def _bf16_parts(v):
    v = np.float32(v)
    out = []
    for _ in range(3):
        p = np.float32(np.asarray(v, dtype=BF16))
        out.append(float(p))
        v = np.float32(v - p)
    return out


def _split3(x):
    hi = x.astype(BF16).astype(F32)
    r = x - hi
    mid = r.astype(BF16).astype(F32)
    lo = (r - mid).astype(BF16).astype(F32)
    return hi, mid, lo


def _lane_iota():
    return lax.broadcasted_iota(I32, (1, LANES), 1)


def _row_iota():
    return lax.broadcasted_iota(I32, (LANES, 1), 0)


def _spare(kv_half):
    return HEAD_DIM if kv_half == 0 else 0


def _dot_nt(a, b):
    return lax.dot_general(a, b, (((1,), (1,)), ((), ())), preferred_element_type=F32)


def _transposed(x):
    return x.astype(F32).T.astype(BF16)


def _params(*sem):
    return pltpu.CompilerParams(dimension_semantics=sem, vmem_limit_bytes=VMEM_LIMIT)


def _inproj_kernel(x_ref, w_ref, b_ref, hb_ref, hf_ref):
    acc = jnp.dot(x_ref[...].astype(BF16), w_ref[...], preferred_element_type=F32) + b_ref[...]
    hb_ref[...] = acc[:, :CB].astype(BF16)
    hf_ref[...] = acc[:, CB:]


def _inproj(x2, w, b, tm):
    n, d = x2.shape
    ct = w.shape[1]
    return pl.pallas_call(
        _inproj_kernel,
        grid=(n // tm,),
        in_specs=[pl.BlockSpec((tm, d), lambda i: (i, 0)),
                  pl.BlockSpec((d, ct), lambda i: (0, 0)),
                  pl.BlockSpec((1, ct), lambda i: (0, 0))],
        out_specs=[pl.BlockSpec((tm, CB), lambda i: (i, 0)),
                   pl.BlockSpec((tm, CF), lambda i: (i, 0))],
        out_shape=[jax.ShapeDtypeStruct((n, CB), BF16), jax.ShapeDtypeStruct((n, CF), F32)],
        compiler_params=_params("arbitrary"),
        name="inproj",
    )(x2, w, b)


def _gates_kernel(g_ref, o_ref, *, chunk):
    s = g_ref.shape[1]
    lane = _lane_iota()
    r = lax.broadcasted_iota(I32, (chunk, chunk), 0)
    c = lax.broadcasted_iota(I32, (chunk, chunk), 1)
    tri = jnp.where(c <= r, 1.0, 0.0).astype(BF16)

    def body(j, carry):
        x = g_ref[0, pl.ds(pl.multiple_of(j * chunk, chunk), chunk), :]
        ls = jnp.minimum(x, 0.0) - jnp.log1p(jnp.exp(-jnp.abs(x)))
        cs = carry
        for part in _split3(ls):
            cs = cs + jnp.dot(tri, part.astype(BF16), preferred_element_type=F32)
        sg = jax.nn.sigmoid(x)
        o_ref[0, pl.ds(pl.multiple_of(j * chunk, chunk), chunk), :] = jnp.where(lane < FOX_HEADS, cs, sg)
        return cs[chunk - 1:chunk, :]

    lax.fori_loop(0, s // chunk, body, jnp.zeros((1, LANES), F32))


def _gates(hf3):
    b, s, _ = hf3.shape
    chunk = min(256, s)
    return pl.pallas_call(
        functools.partial(_gates_kernel, chunk=chunk),
        grid=(b,),
        in_specs=[pl.BlockSpec((1, s, LANES), lambda i: (i, 0, 2))],
        out_specs=pl.BlockSpec((1, s, LANES), lambda i: (i, 0, 0)),
        out_shape=jax.ShapeDtypeStruct((b, s, LANES), F32),
        compiler_params=_params("arbitrary"),
        name="gates",
    )(hf3)


def _softmax_steps(carries, scores, values):
    stats = []
    for (m, l, _), sc in zip(carries, scores):
        m_new = jnp.maximum(m, jnp.max(sc, axis=0, keepdims=True))
        alpha = jnp.exp(m - m_new)
        pr = jnp.exp(sc - m_new)
        stats.append((m_new, alpha * l + jnp.sum(pr, axis=0, keepdims=True), alpha, pr.astype(BF16)))
    return tuple((m, l, alpha * acc + jnp.dot(vt, pr, preferred_element_type=F32))
                 for (m, l, alpha, pr), (_, _, acc), vt in zip(stats, carries, values))


def _softmax_init(queries):
    return (jnp.full((1, queries), NEG, F32), jnp.zeros((1, queries), F32), jnp.zeros((LANES, queries), F32))


def _fox_kernel(q_ref, k_ref, v_ref, cq_ref, ck_ref, o_ref, ka_ref, vt_ref, *, tile):
    i = pl.program_id(1)
    s_len = k_ref.shape[1]
    lane = _lane_iota()
    npair = FOX_HEADS // 2

    @pl.when(i == 0)
    def _fill():
        for h in range(FOX_HEADS):
            p, half = divmod(h, 2)
            a0 = _spare(half)
            k = k_ref[0, :, p * LANES:(p + 1) * LANES].astype(F32)
            hi, mid, lo = _split3(-ck_ref[0, :, h:h + 1])
            ka = jnp.where((lane >= a0 + L_QT) & (lane < a0 + L_QT + 3), 1.0, k)
            ka = jnp.where(lane == a0 + L_KH, hi, ka)
            ka = jnp.where(lane == a0 + L_KH + 1, mid, ka)
            ka = jnp.where(lane == a0 + L_KH + 2, lo, ka)
            ka_ref[h] = ka.astype(BF16)
        for p in range(npair):
            for c in range(s_len // tile):
                vt_ref[p, c] = _transposed(v_ref[0, c * tile:(c + 1) * tile, p * LANES:(p + 1) * LANES])

    qas = []
    for h in range(FOX_HEADS):
        p, half = divmod(h, 2)
        a0 = _spare(half)
        q = q_ref[0, :, p * LANES:(p + 1) * LANES].astype(F32)
        in_half = (lane < HEAD_DIM) if half == 0 else (lane >= HEAD_DIM)
        hi, mid, lo = _split3(cq_ref[0, :, h:h + 1])
        qa = jnp.where(in_half, q, 0.0)
        qa = jnp.where(lane == a0 + L_QT, hi, qa)
        qa = jnp.where(lane == a0 + L_QT + 1, mid, qa)
        qa = jnp.where(lane == a0 + L_QT + 2, lo, qa)
        qas.append(jnp.where((lane >= a0 + L_KH) & (lane < a0 + L_KH + 3), 1.0, qa).astype(BF16))

    key = lax.broadcasted_iota(I32, (tile, tile), 0)
    qry = lax.broadcasted_iota(I32, (tile, tile), 1)
    causal = jnp.where(key <= qry, 0.0, NEG)

    def step(j, carry, masked):
        start = pl.multiple_of(j * tile, tile)
        scores = [_dot_nt(ka_ref[h, pl.ds(start, tile), :], qas[h]) for h in range(FOX_HEADS)]
        if masked:
            scores = [sc + causal for sc in scores]
        return _softmax_steps(carry, scores, [vt_ref[h // 2, j] for h in range(FOX_HEADS)])

    init = tuple(_softmax_init(tile) for _ in range(FOX_HEADS))
    carry = lax.fori_loop(0, i, functools.partial(step, masked=False), init)
    carry = step(i, carry, True)
    row = _row_iota()
    for p in range(npair):
        (_, l0, a0_), (_, l1, a1_) = carry[2 * p], carry[2 * p + 1]
        o_ref[0, :, p * LANES:(p + 1) * LANES] = jnp.where(row < HEAD_DIM, a0_ / l0, a1_ / l1).T.astype(BF16)


def _fox(hb3, cg):
    b, s, _ = hb3.shape
    tile = min(FOX_TILE, s)
    w = FOX_HEADS * HEAD_DIM
    return pl.pallas_call(
        functools.partial(_fox_kernel, tile=tile),
        grid=(b, s // tile),
        in_specs=[pl.BlockSpec((1, tile, w), lambda bi, i: (bi, i, C_FQ // w)),
                  pl.BlockSpec((1, s, w), lambda bi, i: (bi, 0, C_FK // w)),
                  pl.BlockSpec((1, s, w), lambda bi, i: (bi, 0, C_FV // w)),
                  pl.BlockSpec((1, tile, LANES), lambda bi, i: (bi, i, 0)),
                  pl.BlockSpec((1, s, LANES), lambda bi, i: (bi, 0, 0))],
        out_specs=pl.BlockSpec((1, tile, w), lambda bi, i: (bi, i, 0)),
        out_shape=jax.ShapeDtypeStruct((b, s, w), BF16),
        scratch_shapes=[pltpu.VMEM((FOX_HEADS, s, LANES), BF16),
                        pltpu.VMEM((FOX_HEADS // 2, s // tile, LANES, tile), BF16)],
        compiler_params=_params("arbitrary", "arbitrary"),
        name="fox",
    )(hb3, hb3, hb3, cg, cg)


def _fill_banded(k_ref, v_ref, kp_ref, vt_ref, window):
    s_len = k_ref.shape[1]
    lane = _lane_iota()
    k = k_ref[0].astype(F32)
    pos = lax.broadcasted_iota(I32, (s_len, 1), 0) + window
    pos_hi = ((pos >> 8) << 8).astype(F32)
    pos_lo = (pos & 255).astype(F32)
    for kv in range(KV_HEADS):
        a0 = _spare(kv)
        ka = jnp.where((lane >= a0 + L_QT) & (lane < a0 + L_QT + 3), 1.0, k)
        ka = jnp.where((lane >= a0 + L_KH) & (lane < a0 + L_KH + 3), pos_hi, ka)
        ka = jnp.where((lane >= a0 + L_KL) & (lane < a0 + L_KL + 3), pos_lo, ka)
        ka = jnp.where(lane == a0 + L_PAD, 0.0, ka)
        kp_ref[kv, pl.ds(window, s_len), :] = ka.astype(BF16)
        pad = jnp.where(lane == a0 + L_PAD, NEG, 0.0) + jnp.zeros((window, LANES), F32)
        kp_ref[kv, pl.ds(0, window), :] = pad.astype(BF16)
    npad = window // LANES
    for c in range(npad):
        vt_ref[c] = jnp.zeros((LANES, LANES), BF16)
    for c in range(s_len // LANES):
        vt_ref[npad + c] = _transposed(v_ref[0, c * LANES:(c + 1) * LANES, :])


def _alibi_query(q, kv, g, t_pos, slopes, lane, pad_lane):
    a0 = _spare(kv)
    in_half = (lane < HEAD_DIM) if kv == 0 else (lane >= HEAD_DIM)
    sl = float(slopes[kv][g])
    hi, mid, lo = _split3(t_pos * (-sl))
    qa = jnp.where(in_half, q.astype(F32), 0.0)
    qa = jnp.where(lane == a0 + L_QT, hi, qa)
    qa = jnp.where(lane == a0 + L_QT + 1, mid, qa)
    qa = jnp.where(lane == a0 + L_QT + 2, lo, qa)
    for c, part in enumerate(_bf16_parts(sl)):
        qa = jnp.where((lane == a0 + L_KH + c) | (lane == a0 + L_KL + c), part, qa)
    if pad_lane:
        qa = jnp.where(lane == a0 + L_PAD, 1.0, qa)
    return qa


def _banded_attend(q, kp_ref, vt_ref, i, window, slopes, sink_ref):
    tq = q.shape[0]
    span = window + tq
    lane = _lane_iota()
    start = pl.multiple_of(i * tq, tq)
    t_pad = (lax.broadcasted_iota(I32, (tq, 1), 0) + i * tq + window).astype(F32)
    key = lax.broadcasted_iota(I32, (span, tq), 0)
    qry = lax.broadcasted_iota(I32, (span, tq), 1)
    band = jnp.where((key > qry) & (key <= qry + window), 0.0, NEG)
    band = jnp.concatenate([band] * GQA_GROUP, axis=1)
    outs = []
    for kv in range(KV_HEADS):
        qst = jnp.concatenate(
            [_alibi_query(q[:, g * LANES:(g + 1) * LANES], kv, g, t_pad, slopes, lane, True).astype(BF16)
             for g in range(GQA_GROUP)], axis=0)
        sc = _dot_nt(kp_ref[kv, pl.ds(start, span), :], qst) + band
        m = jnp.max(sc, axis=0, keepdims=True)
        if sink_ref is not None:
            sink = jnp.concatenate([jnp.full((1, tq), sink_ref[kv * GQA_GROUP + g], F32)
                                    for g in range(GQA_GROUP)], axis=1)
            m = jnp.maximum(m, sink)
        pr = jnp.exp(sc - m)
        den = jnp.sum(pr, axis=0, keepdims=True)
        if sink_ref is not None:
            den = den + jnp.exp(sink - m)
        pr = pr.astype(BF16)
        o = jnp.zeros((LANES, GQA_GROUP * tq), F32)
        for c in range(span // LANES):
            o = o + jnp.dot(vt_ref[i + c], pr[c * LANES:(c + 1) * LANES, :], preferred_element_type=F32)
        outs.append(o / den)
    return outs


def _merge_kv(outs, g, tq, row):
    return jnp.where(row < HEAD_DIM, outs[0][:, g * tq:(g + 1) * tq], outs[1][:, g * tq:(g + 1) * tq])


def _swa_kernel(sink_ref, q_ref, k_ref, v_ref, o_ref, kp_ref, vt_ref, *, slopes):
    i = pl.program_id(1)
    tq = q_ref.shape[1]

    @pl.when(i == 0)
    def _fill():
        _fill_banded(k_ref, v_ref, kp_ref, vt_ref, SWA_WINDOW)

    outs = _banded_attend(q_ref[0], kp_ref, vt_ref, i, SWA_WINDOW, slopes, sink_ref)
    row = _row_iota()
    for g in range(GQA_GROUP):
        o_ref[0, :, g * LANES:(g + 1) * LANES] = _merge_kv(outs, g, tq, row).T.astype(BF16)


def _swa(hb3, sinks, slopes):
    b, s, _ = hb3.shape
    w = GQA_GROUP * LANES
    return pl.pallas_call(
        functools.partial(_swa_kernel, slopes=slopes),
        grid=(b, s // Q_TILE),
        in_specs=[pl.BlockSpec(memory_space=pltpu.SMEM),
                  pl.BlockSpec((1, Q_TILE, w), lambda bi, i: (bi, i, C_SQ // w)),
                  pl.BlockSpec((1, s, LANES), lambda bi, i: (bi, 0, C_SK // LANES)),
                  pl.BlockSpec((1, s, LANES), lambda bi, i: (bi, 0, C_SV // LANES))],
        out_specs=pl.BlockSpec((1, Q_TILE, w), lambda bi, i: (bi, i, 0)),
        out_shape=jax.ShapeDtypeStruct((b, s, w), BF16),
        scratch_shapes=[pltpu.VMEM((KV_HEADS, s + SWA_WINDOW, LANES), BF16),
                        pltpu.VMEM(((s + SWA_WINDOW) // LANES, LANES, LANES), BF16)],
        compiler_params=_params("arbitrary", "arbitrary"),
        name="swa",
    )(sinks, hb3, hb3, hb3)


def _compress_kernel(g_ref, pe_ref, w1_ref, w2_ref, kc_ref, vc_ref):
    nc = g_ref.shape[1]
    for which, out_ref in ((0, kc_ref), (1, vc_ref)):
        ha = jnp.zeros((nc, LANES), F32)
        hb = jnp.zeros((nc, LANES), F32)
        for l in range(CMP_STRIDE):
            x = g_ref[0, :, (2 * l + which) * LANES:(2 * l + which + 1) * LANES]
            xa = (x + pe_ref[which, l:l + 1, :]).astype(BF16)
            xb = (x + pe_ref[which, CMP_STRIDE + l:CMP_STRIDE + l + 1, :]).astype(BF16)
            ha = ha + jnp.dot(xa, w1_ref[which, l], preferred_element_type=F32)
            hb = hb + jnp.dot(xb, w1_ref[which, CMP_STRIDE + l], preferred_element_type=F32)
        hid = ha + pltpu.roll(hb, nc - 1, 0)
        act = 0.5 * hid * (1.0 + jnp.tanh(np.sqrt(2.0 / np.pi).astype(np.float32) * (hid + 0.044715 * (hid * hid * hid))))
        out_ref[0] = jnp.dot(act.astype(BF16), w2_ref[which], preferred_element_type=F32).astype(BF16)


def _compress(g, pe2, w1b, w2b):
    b, nc, gw = g.shape
    return pl.pallas_call(
        _compress_kernel,
        grid=(b,),
        in_specs=[pl.BlockSpec((1, nc, gw), lambda i: (i, 0, 0)),
                  pl.BlockSpec(pe2.shape, lambda i: (0, 0, 0)),
                  pl.BlockSpec(w1b.shape, lambda i: (0, 0, 0, 0)),
                  pl.BlockSpec(w2b.shape, lambda i: (0, 0, 0))],
        out_specs=[pl.BlockSpec((1, nc, LANES), lambda i: (i, 0, 0)),
                   pl.BlockSpec((1, nc, LANES), lambda i: (i, 0, 0))],
        out_shape=[jax.ShapeDtypeStruct((b, nc, LANES), BF16), jax.ShapeDtypeStruct((b, nc, LANES), BF16)],
        compiler_params=_params("arbitrary"),
        name="compress",
    )(g, pe2, w1b, w2b)


def _nsa_kernel(q_ref, ks_ref, vs_ref, kw_ref, vw_ref, kc_ref, vc_ref, cg_ref, ovt_ref, o_ref,
                ksa_ref, vst_ref, kwp_ref, vwt_ref, kca_ref, vct_ref, *, slopes, n_sel):
    i = pl.program_id(1)
    tq = q_ref.shape[1]
    s_len = ks_ref.shape[1]
    nc = kc_ref.shape[1]
    ns = s_len // SEL_BLOCK
    lane = _lane_iota()
    row = _row_iota()

    @pl.when(i == 0)
    def _fill():
        _fill_banded(kw_ref, vw_ref, kwp_ref, vwt_ref, NSA_WINDOW)
        ks = ks_ref[0].astype(F32)
        pos = lax.broadcasted_iota(I32, (s_len, 1), 0)
        pos_hi = ((pos >> 8) << 8).astype(F32)
        pos_lo = (pos & 255).astype(F32)
        blk = pos // SEL_BLOCK
        kc = kc_ref[0].astype(F32)
        cend = lax.broadcasted_iota(I32, (nc, 1), 0) * CMP_STRIDE + (CMP_BLOCK - 1)
        cend_hi = ((cend >> 8) << 8).astype(F32)
        cend_lo = (cend & 255).astype(F32)
        for kv in range(KV_HEADS):
            a0 = _spare(kv)
            ka = jnp.where((lane >= a0 + L_BLK) & (lane < a0 + L_BLK + MAX_SEL_BLOCKS),
                           jnp.where(lane - (a0 + L_BLK) == blk, 1.0, 0.0), ks)
            ka = jnp.where((lane >= a0 + L_QT) & (lane < a0 + L_QT + 3), 1.0, ka)
            ka = jnp.where((lane >= a0 + L_KH) & (lane < a0 + L_KH + 3), pos_hi, ka)
            ka = jnp.where((lane >= a0 + L_KL) & (lane < a0 + L_KL + 3), pos_lo, ka)
            ksa_ref[kv] = ka.astype(BF16)
            ca = jnp.where((lane >= a0 + L_QT) & (lane < a0 + L_QT + 3), 1.0, kc)
            ca = jnp.where((lane >= a0 + L_KH) & (lane < a0 + L_KH + 3), cend_hi, ca)
            ca = jnp.where((lane >= a0 + L_KL) & (lane < a0 + L_KL + 3), cend_lo, ca)
            kca_ref[kv] = ca.astype(BF16)
        for c in range(s_len // SEL_KEYS):
            vst_ref[c] = _transposed(vs_ref[0, c * SEL_KEYS:(c + 1) * SEL_KEYS, :])
        vct_ref[...] = _transposed(vc_ref[0])

    q = q_ref[0]
    t_col = (lax.broadcasted_iota(I32, (tq, 1), 0) + i * tq).astype(F32)
    t_row = lax.broadcasted_iota(I32, (1, tq), 1) + i * tq
    cur = t_row // SEL_BLOCK
    cend = lax.broadcasted_iota(I32, (nc, tq), 0) * CMP_STRIDE + (CMP_BLOCK - 1)
    cvalid = jnp.concatenate([cend <= t_row] * GQA_GROUP, axis=1)
    nfull = (i * tq) // SEL_KEYS
    dkey = lax.broadcasted_iota(I32, (SEL_KEYS, tq), 0) + nfull * SEL_KEYS
    dcausal = jnp.where(dkey <= t_row, 0.0, NEG)
    dcausal = jnp.concatenate([dcausal] * GQA_GROUP, axis=1)

    o_cmp, qsels = [], []
    for kv in range(KV_HEADS):
        a0 = _spare(kv)
        qbase = [_alibi_query(q[:, g * LANES:(g + 1) * LANES], kv, g, t_col, slopes, lane, False)
                 for g in range(GQA_GROUP)]

        qst = jnp.concatenate([x.astype(BF16) for x in qbase], axis=0)
        sc = jnp.where(cvalid, _dot_nt(kca_ref[kv], qst), NEG)
        m = jnp.max(sc, axis=0, keepdims=True)
        m = jnp.where(m > 0.5 * NEG, m, 0.0)
        e = jnp.where(cvalid, jnp.exp(sc - m), 0.0)
        den = jnp.sum(e, axis=0, keepdims=True)
        pc = (e / jnp.where(den > 0.0, den, 1.0)).astype(BF16)
        o_cmp.append(jnp.dot(vct_ref[...], pc, preferred_element_type=F32))

        p_slc = jnp.zeros((LANES, tq), F32)
        for g in range(GQA_GROUP):
            p_slc = p_slc + jnp.dot(ovt_ref[kv], pc[:, g * tq:(g + 1) * tq], preferred_element_type=F32)
        r0 = a0 + L_BLK
        p_slc = p_slc[r0:r0 + MAX_SEL_BLOCKS, :]
        jb = lax.broadcasted_iota(I32, (MAX_SEL_BLOCKS, 1), 0)
        forced = (jb == 0) | (jb == cur) | (jb == cur - 1)
        score = jnp.where(jb > cur, -1.0, jnp.where(forced, SEL_FORCE, p_slc))
        score = jnp.where(jb < ns, score, -2.0)
        rank = jnp.zeros((MAX_SEL_BLOCKS, tq), F32)
        for b2 in range(ns):
            other = score[b2:b2 + 1, :]
            beats = (other > score) | ((other == score) & (jb > b2))
            rank = rank + jnp.where(beats, 1.0, 0.0)
        selq = jnp.where(rank < float(n_sel), 0.0, NEG)
        pieces = [selq]
        if r0 > 0:
            pieces.insert(0, jnp.zeros((r0, tq), F32))
        if LANES - r0 - MAX_SEL_BLOCKS > 0:
            pieces.append(jnp.zeros((LANES - r0 - MAX_SEL_BLOCKS, tq), F32))
        selq = jnp.concatenate(pieces, axis=0).T
        in_blk = (lane >= r0) & (lane < r0 + MAX_SEL_BLOCKS)
        qsels.append(jnp.concatenate([jnp.where(in_blk, selq, x).astype(BF16) for x in qbase], axis=0))

    def step(j, carry, bias):
        start = pl.multiple_of(j * SEL_KEYS, SEL_KEYS)
        scores = [_dot_nt(ksa_ref[kv, pl.ds(start, SEL_KEYS), :], qsels[kv]) for kv in range(KV_HEADS)]
        if bias is not None:
            scores = [sc + bias for sc in scores]
        return _softmax_steps(carry, scores, [vst_ref[j]] * KV_HEADS)

    init = tuple(_softmax_init(GQA_GROUP * tq) for _ in range(KV_HEADS))
    carry = lax.fori_loop(0, nfull, lambda j, c: step(j, c, None), init)
    carry = step(nfull, carry, dcausal)
    o_sel = [acc / l for (_, l, acc) in carry]

    o_win = _banded_attend(q, kwp_ref, vwt_ref, i, NSA_WINDOW, slopes, None)

    gates = cg_ref[0].T
    for g in range(GQA_GROUP):
        total = jnp.zeros((LANES, tq), F32)
        for r, branch in enumerate((o_cmp, o_sel, o_win)):
            c0 = FOX_HEADS + (r * GQA_GROUP + g) * KV_HEADS
            gate = jnp.where(row < HEAD_DIM, gates[c0:c0 + 1, :], gates[c0 + 1:c0 + 2, :])
            total = total + gate * _merge_kv(branch, g, tq, row)
        o_ref[0, :, g * LANES:(g + 1) * LANES] = total.T.astype(BF16)


def _nsa(hb3, kc, vc, cg, ovt, slopes):
    b, s, _ = hb3.shape
    nc = kc.shape[1]
    w = GQA_GROUP * LANES
    n_sel = min(SEL_TOP_N, s // SEL_BLOCK)
    kvspec = lambda c: pl.BlockSpec((1, s, LANES), lambda bi, i: (bi, 0, c // LANES))
    return pl.pallas_call(
        functools.partial(_nsa_kernel, slopes=slopes, n_sel=n_sel),
        grid=(b, s // Q_TILE),
        in_specs=[pl.BlockSpec((1, Q_TILE, w), lambda bi, i: (bi, i, C_NQ // w)),
                  kvspec(C_NKS), kvspec(C_NVS), kvspec(C_NKW), kvspec(C_NVW),
                  pl.BlockSpec((1, nc, LANES), lambda bi, i: (bi, 0, 0)),
                  pl.BlockSpec((1, nc, LANES), lambda bi, i: (bi, 0, 0)),
                  pl.BlockSpec((1, Q_TILE, LANES), lambda bi, i: (bi, i, 0)),
                  pl.BlockSpec(ovt.shape, lambda bi, i: (0, 0, 0))],
        out_specs=pl.BlockSpec((1, Q_TILE, w), lambda bi, i: (bi, i, 0)),
        out_shape=jax.ShapeDtypeStruct((b, s, w), BF16),
        scratch_shapes=[pltpu.VMEM((KV_HEADS, s, LANES), BF16),
                        pltpu.VMEM((s // SEL_KEYS, LANES, SEL_KEYS), BF16),
                        pltpu.VMEM((KV_HEADS, s + NSA_WINDOW, LANES), BF16),
                        pltpu.VMEM(((s + NSA_WINDOW) // LANES, LANES, LANES), BF16),
                        pltpu.VMEM((KV_HEADS, nc, LANES), BF16),
                        pltpu.VMEM((LANES, nc), BF16)],
        compiler_params=_params("arbitrary", "arbitrary"),
        name="nsa",
    )(hb3, hb3, hb3, hb3, hb3, kc, vc, cg, ovt)


def _layer_norm(y, g, b):
    mu = jnp.mean(y, axis=-1, keepdims=True)
    var = jnp.mean(jnp.square(y - mu), axis=-1, keepdims=True)
    return (y - mu) * lax.rsqrt(var + LN_EPS) * g + b


def _mix_kernel(x_ref, of_ref, os_ref, on_ref, wf_ref, ws_ref, wn_ref, g_ref, b_ref, rw_ref, rb_ref,
                x1_ref, ri_ref, rwt_ref, cnt_ref, carry_ref, *, alpha):
    i = pl.program_id(0)
    tm = x_ref.shape[0]
    lane = _lane_iota()

    @pl.when(i == 0)
    def _init():
        carry_ref[...] = jnp.zeros_like(carry_ref)

    mix = jnp.dot(of_ref[...], wf_ref[...], preferred_element_type=F32)
    mix = mix + jnp.dot(os_ref[...], ws_ref[...], preferred_element_type=F32)
    mix = mix + jnp.dot(on_ref[...], wn_ref[...], preferred_element_type=F32)
    x1 = _layer_norm(alpha * x_ref[...] + mix, g_ref[...], b_ref[...])
    x1_ref[...] = x1

    logits = jnp.dot(x1.astype(BF16), rw_ref[...], preferred_element_type=F32) + rb_ref[...]
    v = jnp.where(lane < N_EXPERTS, logits, NEG)
    tops, hots = [], []
    for _ in range(TOP_K):
        mx = jnp.max(v, axis=-1, keepdims=True)
        idx = jnp.min(jnp.where(v == mx, lane, LANES), axis=-1, keepdims=True)
        hot = lane == idx
        tops.append((mx, idx))
        hots.append(hot)
        v = jnp.where(hot, 2.0 * NEG, v)
    es = [jnp.exp(mx - tops[0][0]) for mx, _ in tops]
    den = es[0] + es[1] + es[2] + es[3]

    member = jnp.zeros((tm, LANES), F32)
    for hot in hots:
        member = member + jnp.where(hot, 1.0, 0.0)
    r = lax.broadcasted_iota(I32, (tm, tm), 0)
    c = lax.broadcasted_iota(I32, (tm, tm), 1)
    strict = jnp.where(c < r, 1.0, 0.0).astype(BF16)
    before = jnp.dot(strict, member.astype(BF16), preferred_element_type=F32) + carry_ref[0:1, :]
    ri = jnp.zeros((tm, LANES), F32)
    rwt = jnp.zeros((tm, LANES), F32)
    for k in range(TOP_K):
        rank = jnp.sum(jnp.where(hots[k], before, 0.0), axis=-1, keepdims=True)
        ri = jnp.where(lane == k, tops[k][1].astype(F32), ri)
        ri = jnp.where(lane == TOP_K + k, rank, ri)
        rwt = jnp.where(lane == k, es[k] / den, rwt)
    ri_ref[...] = ri.astype(I32)
    rwt_ref[...] = rwt
    new = carry_ref[0:1, :] + jnp.sum(member, axis=0, keepdims=True)
    carry_ref[...] = jnp.zeros_like(carry_ref) + new
    cnt_ref[...] = jnp.zeros_like(cnt_ref) + new


def _mix(x2, of, osw, on, wf, ws, wn, g, b, rw, rb, alpha, tm):
    n, d = x2.shape
    full = lambda a: pl.BlockSpec(a.shape, lambda i: (0,) * a.ndim)
    rows = lambda a: pl.BlockSpec((tm, a.shape[1]), lambda i: (i, 0))
    return pl.pallas_call(
        functools.partial(_mix_kernel, alpha=alpha),
        grid=(n // tm,),
        in_specs=[rows(x2), rows(of), rows(osw), rows(on), full(wf), full(ws), full(wn),
                  full(g), full(b), full(rw), full(rb)],
        out_specs=[pl.BlockSpec((tm, d), lambda i: (i, 0)),
                   pl.BlockSpec((tm, LANES), lambda i: (i, 0)),
                   pl.BlockSpec((tm, LANES), lambda i: (i, 0)),
                   pl.BlockSpec((8, LANES), lambda i: (0, 0))],
        out_shape=[jax.ShapeDtypeStruct((n, d), F32), jax.ShapeDtypeStruct((n, LANES), I32),
                   jax.ShapeDtypeStruct((n, LANES), F32), jax.ShapeDtypeStruct((8, LANES), F32)],
        scratch_shapes=[pltpu.VMEM((8, LANES), F32)],
        compiler_params=_params("arbitrary"),
        name="mix_ln_router",
    )(x2, of, osw, on, wf, ws, wn, g, b, rw, rb)


def _dispatch_kernel(dest_ref, x_ref, zeros_hbm, xs_hbm, sem):
    del zeros_hbm
    tokens = x_ref.shape[0]

    def issue(t, _):
        for k in range(TOP_K):
            pltpu.make_async_copy(x_ref.at[pl.ds(t, 1)],
                                  xs_hbm.at[pl.ds(dest_ref[t * TOP_K + k], 1)], sem).start()
        return 0

    lax.fori_loop(0, tokens, issue, 0)
    for k in range(TOP_K):
        pltpu.make_async_copy(x_ref, xs_hbm.at[pl.ds(0, tokens)], sem).wait()


def _dispatch(dest_flat, x1, rows_total, tokens):
    n, d = x1.shape
    zeros = jnp.zeros((rows_total, d), F32)
    return pl.pallas_call(
        _dispatch_kernel,
        grid=(n // tokens,),
        in_specs=[pl.BlockSpec((tokens * TOP_K,), lambda i: (i,), memory_space=pltpu.SMEM),
                  pl.BlockSpec((tokens, d), lambda i: (i, 0)),
                  pl.BlockSpec(memory_space=pl.ANY)],
        out_specs=pl.BlockSpec(memory_space=pl.ANY),
        out_shape=jax.ShapeDtypeStruct((rows_total, d), F32),
        scratch_shapes=[pltpu.SemaphoreType.DMA(())],
        input_output_aliases={2: 0},
        compiler_params=_params("arbitrary"),
        name="moe_dispatch",
    )(dest_flat, x1, zeros)


def _ffn_kernel(be_ref, nu_ref, x_ref, wg_ref, wu_ref, bg_ref, bu_ref, wd_ref, bd_ref, y_ref):
    i = pl.program_id(0)

    @pl.when(i < nu_ref[0])
    def _run():
        x = x_ref[...].astype(BF16)
        gate = jnp.dot(x, wg_ref[0], preferred_element_type=F32) + bg_ref[0]
        up = jnp.dot(x, wu_ref[0], preferred_element_type=F32) + bu_ref[0]
        gate = jnp.minimum(gate, SWIGLU_LIMIT)
        up = jnp.clip(up, -SWIGLU_LIMIT, SWIGLU_LIMIT)
        glu = gate * jax.nn.sigmoid(gate * SWIGLU_ALPHA)
        act = ((up + 1.0) * glu).astype(BF16)
        y_ref[...] = jnp.dot(act, wd_ref[0], preferred_element_type=F32) + bd_ref[0]

    @pl.when(i >= nu_ref[0])
    def _skip():
        y_ref[...] = jnp.zeros_like(y_ref)


def _ffn(blk_e, n_used, xs, wg, wu, bg, bu, wd, bd, rows):
    p, d = xs.shape
    wspec = lambda a: pl.BlockSpec((1,) + a.shape[1:], lambda i, be, nu: (be[i], 0, 0))
    grid_spec = pltpu.PrefetchScalarGridSpec(
        num_scalar_prefetch=2,
        grid=(p // rows,),
        in_specs=[pl.BlockSpec((rows, d), lambda i, be, nu: (i, 0)),
                  wspec(wg), wspec(wu), wspec(bg), wspec(bu), wspec(wd), wspec(bd)],
        out_specs=pl.BlockSpec((rows, d), lambda i, be, nu: (i, 0)),
    )
    return pl.pallas_call(
        _ffn_kernel,
        grid_spec=grid_spec,
        out_shape=jax.ShapeDtypeStruct((p, d), F32),
        compiler_params=_params("arbitrary"),
        name="moe_ffn",
    )(blk_e, n_used, xs, wg, wu, bg, bu, wd, bd)


def _combine_kernel(dest_ref, x1_ref, w_ref, g_ref, b_ref, y_hbm, o_ref, buf_ref, sem, *, alpha):
    tokens = x1_ref.shape[0]

    def issue(t, _):
        for k in range(TOP_K):
            pltpu.make_async_copy(y_hbm.at[pl.ds(dest_ref[t * TOP_K + k], 1)],
                                  buf_ref.at[k, pl.ds(t, 1)], sem).start()
        return 0

    lax.fori_loop(0, tokens, issue, 0)
    for k in range(TOP_K):
        pltpu.make_async_copy(y_hbm.at[pl.ds(0, tokens)], buf_ref.at[k], sem).wait()
    w = w_ref[...]
    ffn = jnp.zeros(x1_ref.shape, F32)
    for k in range(TOP_K):
        ffn = ffn + buf_ref[k] * w[:, k:k + 1]
    o_ref[...] = _layer_norm(alpha * x1_ref[...] + ffn, g_ref[...], b_ref[...])


def _combine(dest_flat, x1, rwt, g, b, y, alpha, tokens):
    n, d = x1.shape
    full = lambda a: pl.BlockSpec(a.shape, lambda i: (0,) * a.ndim)
    return pl.pallas_call(
        functools.partial(_combine_kernel, alpha=alpha),
        grid=(n // tokens,),
        in_specs=[pl.BlockSpec((tokens * TOP_K,), lambda i: (i,), memory_space=pltpu.SMEM),
                  pl.BlockSpec((tokens, d), lambda i: (i, 0)),
                  pl.BlockSpec((tokens, LANES), lambda i: (i, 0)),
                  full(g), full(b),
                  pl.BlockSpec(memory_space=pl.ANY)],
        out_specs=pl.BlockSpec((tokens, d), lambda i: (i, 0)),
        out_shape=jax.ShapeDtypeStruct((n, d), F32),
        scratch_shapes=[pltpu.VMEM((TOP_K, tokens, d), F32), pltpu.SemaphoreType.DMA(())],
        compiler_params=_params("arbitrary"),
        name="moe_combine_ln",
    )(dest_flat, x1, rwt, g, b, y)


def _gqa_cols(base):
    idx = np.empty(GQA_GROUP * KV_HEADS * HEAD_DIM, np.int64)
    for g in range(GQA_GROUP):
        for kv in range(KV_HEADS):
            dst = (g * KV_HEADS + kv) * HEAD_DIM
            src = (kv * GQA_GROUP + g) * HEAD_DIM
            idx[dst:dst + HEAD_DIM] = base + src + np.arange(HEAD_DIM)
    return idx


def _inproj_layout():
    fw = FOX_HEADS * HEAD_DIM
    qw = SWA_Q_HEADS * HEAD_DIM
    kw = KV_HEADS * HEAD_DIM
    sizes = (fw, fw, fw, FOX_HEADS, qw, kw, kw, qw, kw, kw, kw, kw, kw, kw, 3 * NSA_Q_HEADS)
    off = np.concatenate([[0], np.cumsum(sizes)])
    (o_fq, o_fk, o_fv, o_ff, o_sq, o_sk, o_sv, o_nq, o_nkc, o_nvc, o_nks, o_nvs, o_nkw, o_nvw, o_ng, total) = off
    ar = np.arange
    cols = np.concatenate([
        _gqa_cols(o_sq), _gqa_cols(o_nq), o_fq + ar(fw), o_fk + ar(fw), o_fv + ar(fw),
        o_sk + ar(kw), o_sv + ar(kw), o_nks + ar(kw), o_nvs + ar(kw), o_nkw + ar(kw), o_nvw + ar(kw),
        o_nkc + ar(kw), o_nvc + ar(kw)])
    gate = np.full(LANES, total, np.int64)
    gate[:FOX_HEADS] = o_ff + ar(FOX_HEADS)
    for r in range(3):
        for g in range(GQA_GROUP):
            for kv in range(KV_HEADS):
                gate[FOX_HEADS + (r * GQA_GROUP + g) * KV_HEADS + kv] = o_ng + (kv * GQA_GROUP + g) * 3 + r
    cols = np.concatenate([cols, gate])
    scale = np.ones(cols.shape[0], np.float32)
    scale[C_SQ:C_SQ + qw] = HEAD_DIM ** -0.5
    scale[C_NQ:C_NQ + qw] = HEAD_DIM ** -0.5
    scale[C_FQ:C_FQ + fw] = HEAD_DIM ** -0.5
    return cols, scale


def _outproj_rows():
    fw = FOX_HEADS * HEAD_DIM
    qw = SWA_Q_HEADS * HEAD_DIM
    return np.arange(fw), _gqa_cols(fw), _gqa_cols(fw + qw)


def _overlap_matrix_t(s_len, nc):
    ns = s_len // SEL_BLOCK
    cmp_start = np.arange(nc) * CMP_STRIDE
    sel_start = np.arange(ns) * SEL_BLOCK
    ov = np.clip(np.minimum(cmp_start[:, None] + CMP_BLOCK, sel_start[None, :] + SEL_BLOCK)
                 - np.maximum(cmp_start[:, None], sel_start[None, :]), 0, None) / CMP_BLOCK
    ov[(s_len - CMP_BLOCK) // CMP_STRIDE + 1:] = 0.0
    out = np.zeros((KV_HEADS, LANES, nc), np.float32)
    for kv in range(KV_HEADS):
        r0 = _spare(kv) + L_BLK
        out[kv, r0:r0 + ns, :] = ov.T
    return out


def _compress_weights(pe, w1, w2):
    pe2 = jnp.concatenate([pe, pe], axis=-1)
    w1r = w1.reshape(2, CMP_BLOCK, HEAD_DIM, HEAD_DIM)
    z = jnp.zeros_like(w1r)
    w1b = jnp.concatenate([jnp.concatenate([w1r, z], axis=-1), jnp.concatenate([z, w1r], axis=-1)], axis=-2)
    z2 = jnp.zeros_like(w2)
    w2b = jnp.concatenate([jnp.concatenate([w2, z2], axis=-1), jnp.concatenate([z2, w2], axis=-1)], axis=-2)
    return pe2, w1b.astype(BF16), w2b.astype(BF16)


def kernel(x, w_in, b_in, sinks, cmp_pe, cmp_w1, cmp_w2, w_out, ln1_g, ln1_b, router_w, router_b,
           w_gate_up, b_gate_up, w_down, b_down, ln2_g, ln2_b):
    depth = w_in.shape[0]
    bsz, s_len, d = x.shape
    n = bsz * s_len
    nk = n * TOP_K
    alpha = float((2.0 * depth) ** 0.25)
    assert s_len % SEL_KEYS == 0 and s_len // SEL_BLOCK <= MAX_SEL_BLOCKS and n % 512 == 0
    slopes_swa, slopes_nsa = _alibi()
    cols, scale = _inproj_layout()
    rows_f, rows_s, rows_n = _outproj_rows()
    nc = s_len // CMP_STRIDE
    ovt = jnp.asarray(_overlap_matrix_t(s_len, nc), BF16)
    nblk = -(-nk // MOE_ROWS) + N_EXPERTS
    p_rows = nblk * MOE_ROWS
    f = w_down.shape[2]

    x2 = x.reshape(n, d)
    for l in range(depth):
        w_ext = jnp.concatenate([w_in[l], jnp.zeros((d, 1), F32)], axis=1)
        b_ext = jnp.concatenate([b_in[l], jnp.zeros((1,), F32)])
        w_l = (jnp.take(w_ext, cols, axis=1) * scale).astype(BF16)
        b_l = (jnp.take(b_ext, cols) * scale).reshape(1, -1)
        hb, hf = _inproj(x2, w_l, b_l, 512)
        hb3 = hb.reshape(bsz, s_len, CB)
        hf3 = hf.reshape(bsz, s_len, CF)
        cg = _gates(hf3)
        o_fox = _fox(hb3, cg)
        o_swa = _swa(hb3, sinks[l], slopes_swa)
        grp = hf3[:, :, :2 * LANES].reshape(bsz, nc, CMP_STRIDE * 2 * LANES)
        pe2, w1b, w2b = _compress_weights(cmp_pe[l], cmp_w1[l], cmp_w2[l])
        kc, vc = _compress(grp, pe2, w1b, w2b)
        o_nsa = _nsa(hb3, kc, vc, cg, ovt, slopes_nsa)

        wo = w_out[l].astype(BF16)
        rw = jnp.concatenate([router_w[l], jnp.zeros((d, LANES - N_EXPERTS), F32)], axis=1).astype(BF16)
        rb = jnp.concatenate([router_b[l], jnp.zeros((LANES - N_EXPERTS,), F32)]).reshape(1, LANES)
        x1, ri, rwt, cnt = _mix(
            x2, o_fox.reshape(n, -1), o_swa.reshape(n, -1), o_nsa.reshape(n, -1),
            jnp.take(wo, rows_f, axis=0), jnp.take(wo, rows_s, axis=0), jnp.take(wo, rows_n, axis=0),
            ln1_g[l].reshape(1, d), ln1_b[l].reshape(1, d), rw, rb, alpha, 256)

        counts = cnt[0, :N_EXPERTS].astype(I32)
        padded = (counts + MOE_ROWS - 1) // MOE_ROWS * MOE_ROWS
        pad_end = jnp.cumsum(padded)
        pad_start = pad_end - padded
        dest = (jnp.take(pad_start, ri[:, :TOP_K]) + ri[:, TOP_K:2 * TOP_K]).reshape(nk)
        blk_start = jnp.arange(nblk, dtype=I32) * MOE_ROWS
        blk_e = jnp.minimum(jnp.sum((pad_end[None, :] <= blk_start[:, None]).astype(I32), axis=1),
                            N_EXPERTS - 1)
        n_used = (pad_end[-1:] // MOE_ROWS).astype(I32)

        xs = _dispatch(dest, x1, p_rows, 256)
        wgu = w_gate_up[l].reshape(N_EXPERTS, d, f, 2)
        bgu = b_gate_up[l].reshape(N_EXPERTS, 1, f, 2)
        y = _ffn(blk_e, n_used, xs, wgu[..., 0].astype(BF16), wgu[..., 1].astype(BF16),
                 bgu[..., 0], bgu[..., 1], w_down[l].astype(BF16), b_down[l].reshape(N_EXPERTS, 1, d),
                 MOE_ROWS)
        x2 = _combine(dest, x1, rwt, ln2_g[l].reshape(1, d), ln2_b[l].reshape(1, d), y, alpha, 128)
    return x2.reshape(bsz, s_len, d)
```

```python
import functools

import numpy as np
import jax
import jax.numpy as jnp
from jax import lax
from jax.experimental import pallas as pl
from jax.experimental.pallas import tpu as pltpu

F32 = jnp.float32
BF16 = jnp.bfloat16
I32 = jnp.int32

HEAD_DIM = 64
LANES = 128
N_HEADS = 16
SWA_Q_HEADS = 6
NSA_Q_HEADS = 6
FOX_HEADS = 4
KV_HEADS = 2
GQA_GROUP = 3
SWA_WINDOW = 128
CMP_BLOCK = 32
CMP_STRIDE = 16
SEL_BLOCK = 64
SEL_TOP_N = 8
NSA_WINDOW = 512
SEL_FORCE = 1.0e4
N_EXPERTS = 32
TOP_K = 4
SWIGLU_LIMIT = 7.0
SWIGLU_ALPHA = 1.702
LN_EPS = 1e-5
NEG = -1.0e30

Q_TILE = 256
FOX_TILE = 256
SEL_KEYS = 512
MAX_SEL_BLOCKS = 32
MOE_ROWS = 512
VMEM_LIMIT = 56 * 1024 * 1024

L_BLK = 0
L_QT = 32
L_KH = 35
L_KL = 38
L_PAD = 41

C_SQ, C_NQ, C_FQ, C_FK, C_FV = 0, 384, 768, 1024, 1280
C_SK, C_SV, C_NKS, C_NVS, C_NKW, C_NVW = 1536, 1664, 1792, 1920, 2048, 2176
CB = 2304
CF = 384


def _alibi():
    n = SWA_Q_HEADS + NSA_Q_HEADS
    s = (2.0 ** (-8.0 * np.arange(1, n + 1) / n)).astype(np.float32)
    return s[0::2].reshape(KV_HEADS, GQA_GROUP), s[1::2].reshape(KV_HEADS, GQA_GROUP)


def _bf16_parts(v):
    v = np.float32(v)
    out = []
    for _ in range(3):
        p = np.float32(np.asarray(v, dtype=BF16))
        out.append(float(p))
        v = np.float32(v - p)
    return out


def _split3(x):
    hi = x.astype(BF16).astype(F32)
    r = x - hi
    mid = r.astype(BF16).astype(F32)
    lo = (r - mid).astype(BF16).astype(F32)
    return hi, mid, lo


def _lane_iota():
    return lax.broadcasted_iota(I32, (1, LANES), 1)


def _row_iota():
    return lax.broadcasted_iota(I32, (LANES, 1), 0)


def _spare(kv_half):
    return HEAD_DIM if kv_half == 0 else 0


def _dot_nt(a, b):
    return lax.dot_general(a, b, (((1,), (1,)), ((), ())), preferred_element_type=F32)


def _transposed(x):
    return x.astype(F32).T.astype(BF16)


def _params(*sem):
    return pltpu.CompilerParams(dimension_semantics=sem, vmem_limit_bytes=VMEM_LIMIT)


def _inproj_kernel(x_ref, w_ref, b_ref, hb_ref, hf_ref):
    acc = jnp.dot(x_ref[...].astype(BF16), w_ref[...], preferred_element_type=F32) + b_ref[...]
    hb_ref[...] = acc[:, :CB].astype(BF16)
    hf_ref[...] = acc[:, CB:]


def _inproj(x2, w, b, tm):
    n, d = x2.shape
    ct = w.shape[1]
    return pl.pallas_call(
        _inproj_kernel,
        grid=(n // tm,),
        in_specs=[pl.BlockSpec((tm, d), lambda i: (i, 0)),
                  pl.BlockSpec((d, ct), lambda i: (0, 0)),
                  pl.BlockSpec((1, ct), lambda i: (0, 0))],
        out_specs=[pl.BlockSpec((tm, CB), lambda i: (i, 0)),
                   pl.BlockSpec((tm, CF), lambda i: (i, 0))],
        out_shape=[jax.ShapeDtypeStruct((n, CB), BF16), jax.ShapeDtypeStruct((n, CF), F32)],
        compiler_params=_params("arbitrary"),
        name="inproj",
    )(x2, w, b)


def _gates_kernel(g_ref, o_ref, *, chunk):
    s = g_ref.shape[1]
    lane = _lane_iota()
    r = lax.broadcasted_iota(I32, (chunk, chunk), 0)
    c = lax.broadcasted_iota(I32, (chunk, chunk), 1)
    tri = jnp.where(c <= r, 1.0, 0.0).astype(BF16)

    def body(j, carry):
        x = g_ref[0, pl.ds(pl.multiple_of(j * chunk, chunk), chunk), :]
        ls = jnp.minimum(x, 0.0) - jnp.log1p(jnp.exp(-jnp.abs(x)))
        cs = carry
        for part in _split3(ls):
            cs = cs + jnp.dot(tri, part.astype(BF16), preferred_element_type=F32)
        sg = jax.nn.sigmoid(x)
        o_ref[0, pl.ds(pl.multiple_of(j * chunk, chunk), chunk), :] = jnp.where(lane < FOX_HEADS, cs, sg)
        return cs[chunk - 1:chunk, :]

    lax.fori_loop(0, s // chunk, body, jnp.zeros((1, LANES), F32))


def _gates(hf3):
    b, s, _ = hf3.shape
    chunk = min(256, s)
    return pl.pallas_call(
        functools.partial(_gates_kernel, chunk=chunk),
        grid=(b,),
        in_specs=[pl.BlockSpec((1, s, LANES), lambda i: (i, 0, 2))],
        out_specs=pl.BlockSpec((1, s, LANES), lambda i: (i, 0, 0)),
        out_shape=jax.ShapeDtypeStruct((b, s, LANES), F32),
        compiler_params=_params("arbitrary"),
        name="gates",
    )(hf3)


def _softmax_steps(carries, scores, values):
    stats = []
    for (m, l, _), sc in zip(carries, scores):
        m_new = jnp.maximum(m, jnp.max(sc, axis=0, keepdims=True))
        alpha = jnp.exp(m - m_new)
        pr = jnp.exp(sc - m_new)
        stats.append((m_new, alpha * l + jnp.sum(pr, axis=0, keepdims=True), alpha, pr.astype(BF16)))
    return tuple((m, l, alpha * acc + jnp.dot(vt, pr, preferred_element_type=F32))
                 for (m, l, alpha, pr), (_, _, acc), vt in zip(stats, carries, values))


def _softmax_init(queries):
    return (jnp.full((1, queries), NEG, F32), jnp.zeros((1, queries), F32), jnp.zeros((LANES, queries), F32))


def _fox_kernel(q_ref, k_ref, v_ref, cq_ref, ck_ref, o_ref, ka_ref, vt_ref, *, tile):
    i = pl.program_id(1)
    s_len = k_ref.shape[1]
    lane = _lane_iota()
    npair = FOX_HEADS // 2

    @pl.when(i == 0)
    def _fill():
        for h in range(FOX_HEADS):
            p, half = divmod(h, 2)
            a0 = _spare(half)
            k = k_ref[0, :, p * LANES:(p + 1) * LANES].astype(F32)
            hi, mid, lo = _split3(-ck_ref[0, :, h:h + 1])
            ka = jnp.where((lane >= a0 + L_QT) & (lane < a0 + L_QT + 3), 1.0, k)
            ka = jnp.where(lane == a0 + L_KH, hi, ka)
            ka = jnp.where(lane == a0 + L_KH + 1, mid, ka)
            ka = jnp.where(lane == a0 + L_KH + 2, lo, ka)
            ka_ref[h] = ka.astype(BF16)
        for p in range(npair):
            for c in range(s_len // tile):
                vt_ref[p, c] = _transposed(v_ref[0, c * tile:(c + 1) * tile, p * LANES:(p + 1) * LANES])

    qas = []
    for h in range(FOX_HEADS):
        p, half = divmod(h, 2)
        a0 = _spare(half)
        q = q_ref[0, :, p * LANES:(p + 1) * LANES].astype(F32)
        in_half = (lane < HEAD_DIM) if half == 0 else (lane >= HEAD_DIM)
        hi, mid, lo = _split3(cq_ref[0, :, h:h + 1])
        qa = jnp.where(in_half, q, 0.0)
        qa = jnp.where(lane == a0 + L_QT, hi, qa)
        qa = jnp.where(lane == a0 + L_QT + 1, mid, qa)
        qa = jnp.where(lane == a0 + L_QT + 2, lo, qa)
        qas.append(jnp.where((lane >= a0 + L_KH) & (lane < a0 + L_KH + 3), 1.0, qa).astype(BF16))

    key = lax.broadcasted_iota(I32, (tile, tile), 0)
    qry = lax.broadcasted_iota(I32, (tile, tile), 1)
    causal = jnp.where(key <= qry, 0.0, NEG)

    def step(j, carry, masked):
        start = pl.multiple_of(j * tile, tile)
        scores = [_dot_nt(ka_ref[h, pl.ds(start, tile), :], qas[h]) for h in range(FOX_HEADS)]
        if masked:
            scores = [sc + causal for sc in scores]
        return _softmax_steps(carry, scores, [vt_ref[h // 2, j] for h in range(FOX_HEADS)])

    init = tuple(_softmax_init(tile) for _ in range(FOX_HEADS))
    carry = lax.fori_loop(0, i, functools.partial(step, masked=False), init)
    carry = step(i, carry, True)
    row = _row_iota()
    for p in range(npair):
        (_, l0, a0_), (_, l1, a1_) = carry[2 * p], carry[2 * p + 1]
        o_ref[0, :, p * LANES:(p + 1) * LANES] = jnp.where(row < HEAD_DIM, a0_ / l0, a1_ / l1).T.astype(BF16)


def _fox(hb3, cg):
    b, s, _ = hb3.shape
    tile = min(FOX_TILE, s)
    w = FOX_HEADS * HEAD_DIM
    return pl.pallas_call(
        functools.partial(_fox_kernel, tile=tile),
        grid=(b, s // tile),
        in_specs=[pl.BlockSpec((1, tile, w), lambda bi, i: (bi, i, C_FQ // w)),
                  pl.BlockSpec((1, s, w), lambda bi, i: (bi, 0, C_FK // w)),
                  pl.BlockSpec((1, s, w), lambda bi, i: (bi, 0, C_FV // w)),
                  pl.BlockSpec((1, tile, LANES), lambda bi, i: (bi, i, 0)),
                  pl.BlockSpec((1, s, LANES), lambda bi, i: (bi, 0, 0))],
        out_specs=pl.BlockSpec((1, tile, w), lambda bi, i: (bi, i, 0)),
        out_shape=jax.ShapeDtypeStruct((b, s, w), BF16),
        scratch_shapes=[pltpu.VMEM((FOX_HEADS, s, LANES), BF16),
                        pltpu.VMEM((FOX_HEADS // 2, s // tile, LANES, tile), BF16)],
        compiler_params=_params("arbitrary", "arbitrary"),
        name="fox",
    )(hb3, hb3, hb3, cg, cg)


def _fill_banded(k_ref, v_ref, kp_ref, vt_ref, window):
    s_len = k_ref.shape[1]
    lane = _lane_iota()
    k = k_ref[0].astype(F32)
    pos = lax.broadcasted_iota(I32, (s_len, 1), 0) + window
    pos_hi = ((pos >> 8) << 8).astype(F32)
    pos_lo = (pos & 255).astype(F32)
    for kv in range(KV_HEADS):
        a0 = _spare(kv)
        ka = jnp.where((lane >= a0 + L_QT) & (lane < a0 + L_QT + 3), 1.0, k)
        ka = jnp.where((lane >= a0 + L_KH) & (lane < a0 + L_KH + 3), pos_hi, ka)
        ka = jnp.where((lane >= a0 + L_KL) & (lane < a0 + L_KL + 3), pos_lo, ka)
        ka = jnp.where(lane == a0 + L_PAD, 0.0, ka)
        kp_ref[kv, pl.ds(window, s_len), :] = ka.astype(BF16)
        pad = jnp.where(lane == a0 + L_PAD, NEG, 0.0) + jnp.zeros((window, LANES), F32)
        kp_ref[kv, pl.ds(0, window), :] = pad.astype(BF16)
    npad = window // LANES
    for c in range(npad):
        vt_ref[c] = jnp.zeros((LANES, LANES), BF16)
    for c in range(s_len // LANES):
        vt_ref[npad + c] = _transposed(v_ref[0, c * LANES:(c + 1) * LANES, :])


def _alibi_query(q, kv, g, t_pos, slopes, lane, pad_lane):
    a0 = _spare(kv)
    in_half = (lane < HEAD_DIM) if kv == 0 else (lane >= HEAD_DIM)
    sl = float(slopes[kv][g])
    hi, mid, lo = _split3(t_pos * (-sl))
    qa = jnp.where(in_half, q.astype(F32), 0.0)
    qa = jnp.where(lane == a0 + L_QT, hi, qa)
    qa = jnp.where(lane == a0 + L_QT + 1, mid, qa)
    qa = jnp.where(lane == a0 + L_QT + 2, lo, qa)
    for c, part in enumerate(_bf16_parts(sl)):
        qa = jnp.where((lane == a0 + L_KH + c) | (lane == a0 + L_KL + c), part, qa)
    if pad_lane:
        qa = jnp.where(lane == a0 + L_PAD, 1.0, qa)
    return qa


def _banded_attend(q, kp_ref, vt_ref, i, window, slopes, sink_ref):
    tq = q.shape[0]
    span = window + tq
    lane = _lane_iota()
    start = pl.multiple_of(i * tq, tq)
    t_pad = (lax.broadcasted_iota(I32, (tq, 1), 0) + i * tq + window).astype(F32)
    key = lax.broadcasted_iota(I32, (span, tq), 0)
    qry = lax.broadcasted_iota(I32, (span, tq), 1)
    band = jnp.where((key > qry) & (key <= qry + window), 0.0, NEG)
    band = jnp.concatenate([band] * GQA_GROUP, axis=1)
    outs = []
    for kv in range(KV_HEADS):
        qst = jnp.concatenate(
            [_alibi_query(q[:, g * LANES:(g + 1) * LANES], kv, g, t_pad, slopes, lane, True).astype(BF16)
             for g in range(GQA_GROUP)], axis=0)
        sc = _dot_nt(kp_ref[kv, pl.ds(start, span), :], qst) + band
        m = jnp.max(sc, axis=0, keepdims=True)
        if sink_ref is not None:
            sink = jnp.concatenate([jnp.full((1, tq), sink_ref[kv * GQA_GROUP + g], F32)
                                    for g in range(GQA_GROUP)], axis=1)
            m = jnp.maximum(m, sink)
        pr = jnp.exp(sc - m)
        den = jnp.sum(pr, axis=0, keepdims=True)
        if sink_ref is not None:
            den = den + jnp.exp(sink - m)
        pr = pr.astype(BF16)
        o = jnp.zeros((LANES, GQA_GROUP * tq), F32)
        base = i * (tq // LANES)
        for c in range(0, span // LANES, 2):
            wide = min(2, span // LANES - c)
            vt = jnp.concatenate([vt_ref[base + c + j] for j in range(wide)], axis=1)
            o = o + jnp.dot(vt, pr[c * LANES:(c + wide) * LANES, :], preferred_element_type=F32)
        outs.append(o / den)
    return outs


def _merge_kv(outs, g, tq, row):
    return jnp.where(row < HEAD_DIM, outs[0][:, g * tq:(g + 1) * tq], outs[1][:, g * tq:(g + 1) * tq])


def _swa_kernel(sink_ref, q_ref, k_ref, v_ref, o_ref, kp_ref, vt_ref, *, slopes):
    i = pl.program_id(1)
    tq = q_ref.shape[1]

    @pl.when(i == 0)
    def _fill():
        _fill_banded(k_ref, v_ref, kp_ref, vt_ref, SWA_WINDOW)

    outs = _banded_attend(q_ref[0], kp_ref, vt_ref, i, SWA_WINDOW, slopes, sink_ref)
    row = _row_iota()
    for g in range(GQA_GROUP):
        o_ref[0, :, g * LANES:(g + 1) * LANES] = _merge_kv(outs, g, tq, row).T.astype(BF16)


def _swa(hb3, sinks, slopes):
    b, s, _ = hb3.shape
    w = GQA_GROUP * LANES
    return pl.pallas_call(
        functools.partial(_swa_kernel, slopes=slopes),
        grid=(b, s // Q_TILE),
        in_specs=[pl.BlockSpec(memory_space=pltpu.SMEM),
                  pl.BlockSpec((1, Q_TILE, w), lambda bi, i: (bi, i, C_SQ // w)),
                  pl.BlockSpec((1, s, LANES), lambda bi, i: (bi, 0, C_SK // LANES)),
                  pl.BlockSpec((1, s, LANES), lambda bi, i: (bi, 0, C_SV // LANES))],
        out_specs=pl.BlockSpec((1, Q_TILE, w), lambda bi, i: (bi, i, 0)),
        out_shape=jax.ShapeDtypeStruct((b, s, w), BF16),
        scratch_shapes=[pltpu.VMEM((KV_HEADS, s + SWA_WINDOW, LANES), BF16),
                        pltpu.VMEM(((s + SWA_WINDOW) // LANES, LANES, LANES), BF16)],
        compiler_params=_params("arbitrary", "arbitrary"),
        name="swa",
    )(sinks, hb3, hb3, hb3)


def _compress_kernel(g_ref, pe_ref, w1_ref, w2_ref, kc_ref, vc_ref):
    nc = g_ref.shape[1]
    for which, out_ref in ((0, kc_ref), (1, vc_ref)):
        ha = jnp.zeros((nc, LANES), F32)
        hb = jnp.zeros((nc, LANES), F32)
        for l in range(CMP_STRIDE):
            x = g_ref[0, :, (2 * l + which) * LANES:(2 * l + which + 1) * LANES]
            xa = (x + pe_ref[which, l:l + 1, :]).astype(BF16)
            xb = (x + pe_ref[which, CMP_STRIDE + l:CMP_STRIDE + l + 1, :]).astype(BF16)
            ha = ha + jnp.dot(xa, w1_ref[which, l], preferred_element_type=F32)
            hb = hb + jnp.dot(xb, w1_ref[which, CMP_STRIDE + l], preferred_element_type=F32)
        hid = ha + pltpu.roll(hb, nc - 1, 0)
        act = 0.5 * hid * (1.0 + jnp.tanh(np.sqrt(2.0 / np.pi).astype(np.float32) * (hid + 0.044715 * (hid * hid * hid))))
        out_ref[0] = jnp.dot(act.astype(BF16), w2_ref[which], preferred_element_type=F32).astype(BF16)


def _compress(g, pe2, w1b, w2b):
    b, nc, gw = g.shape
    return pl.pallas_call(
        _compress_kernel,
        grid=(b,),
        in_specs=[pl.BlockSpec((1, nc, gw), lambda i: (i, 0, 0)),
                  pl.BlockSpec(pe2.shape, lambda i: (0, 0, 0)),
                  pl.BlockSpec(w1b.shape, lambda i: (0, 0, 0, 0)),
                  pl.BlockSpec(w2b.shape, lambda i: (0, 0, 0))],
        out_specs=[pl.BlockSpec((1, nc, LANES), lambda i: (i, 0, 0)),
                   pl.BlockSpec((1, nc, LANES), lambda i: (i, 0, 0))],
        out_shape=[jax.ShapeDtypeStruct((b, nc, LANES), BF16), jax.ShapeDtypeStruct((b, nc, LANES), BF16)],
        compiler_params=_params("arbitrary"),
        name="compress",
    )(g, pe2, w1b, w2b)


def _nsa_kernel(q_ref, ks_ref, vs_ref, kw_ref, vw_ref, kc_ref, vc_ref, cg_ref, ovt_ref, o_ref,
                ksa_ref, vst_ref, kwp_ref, vwt_ref, kca_ref, vct_ref, *, slopes, n_sel):
    i = pl.program_id(1)
    tq = q_ref.shape[1]
    s_len = ks_ref.shape[1]
    nc = kc_ref.shape[1]
    ns = s_len // SEL_BLOCK
    lane = _lane_iota()
    row = _row_iota()

    @pl.when(i == 0)
    def _fill():
        _fill_banded(kw_ref, vw_ref, kwp_ref, vwt_ref, NSA_WINDOW)
        ks = ks_ref[0].astype(F32)
        pos = lax.broadcasted_iota(I32, (s_len, 1), 0)
        pos_hi = ((pos >> 8) << 8).astype(F32)
        pos_lo = (pos & 255).astype(F32)
        blk = pos // SEL_BLOCK
        kc = kc_ref[0].astype(F32)
        cend = lax.broadcasted_iota(I32, (nc, 1), 0) * CMP_STRIDE + (CMP_BLOCK - 1)
        cend_hi = ((cend >> 8) << 8).astype(F32)
        cend_lo = (cend & 255).astype(F32)
        for kv in range(KV_HEADS):
            a0 = _spare(kv)
            ka = jnp.where((lane >= a0 + L_BLK) & (lane < a0 + L_BLK + MAX_SEL_BLOCKS),
                           jnp.where(lane - (a0 + L_BLK) == blk, 1.0, 0.0), ks)
            ka = jnp.where((lane >= a0 + L_QT) & (lane < a0 + L_QT + 3), 1.0, ka)
            ka = jnp.where((lane >= a0 + L_KH) & (lane < a0 + L_KH + 3), pos_hi, ka)
            ka = jnp.where((lane >= a0 + L_KL) & (lane < a0 + L_KL + 3), pos_lo, ka)
            ksa_ref[kv] = ka.astype(BF16)
            ca = jnp.where((lane >= a0 + L_QT) & (lane < a0 + L_QT + 3), 1.0, kc)
            ca = jnp.where((lane >= a0 + L_KH) & (lane < a0 + L_KH + 3), cend_hi, ca)
            ca = jnp.where((lane >= a0 + L_KL) & (lane < a0 + L_KL + 3), cend_lo, ca)
            kca_ref[kv] = ca.astype(BF16)
        for c in range(s_len // SEL_KEYS):
            vst_ref[c] = _transposed(vs_ref[0, c * SEL_KEYS:(c + 1) * SEL_KEYS, :])
        vct_ref[...] = _transposed(vc_ref[0])

    q = q_ref[0]
    t_col = (lax.broadcasted_iota(I32, (tq, 1), 0) + i * tq).astype(F32)
    t_row = lax.broadcasted_iota(I32, (1, tq), 1) + i * tq
    cur = t_row // SEL_BLOCK
    cend = lax.broadcasted_iota(I32, (nc, tq), 0) * CMP_STRIDE + (CMP_BLOCK - 1)
    cvalid = jnp.concatenate([cend <= t_row] * GQA_GROUP, axis=1)
    nfull = (i * tq) // SEL_KEYS
    dkey = lax.broadcasted_iota(I32, (SEL_KEYS, tq), 0) + nfull * SEL_KEYS
    dcausal = jnp.where(dkey <= t_row, 0.0, NEG)
    dcausal = jnp.concatenate([dcausal] * GQA_GROUP, axis=1)

    o_cmp, qsels = [], []
    for kv in range(KV_HEADS):
        a0 = _spare(kv)
        qbase = [_alibi_query(q[:, g * LANES:(g + 1) * LANES], kv, g, t_col, slopes, lane, False)
                 for g in range(GQA_GROUP)]

        qst = jnp.concatenate([x.astype(BF16) for x in qbase], axis=0)
        sc = jnp.where(cvalid, _dot_nt(kca_ref[kv], qst), NEG)
        m = jnp.max(sc, axis=0, keepdims=True)
        m = jnp.where(m > 0.5 * NEG, m, 0.0)
        e = jnp.where(cvalid, jnp.exp(sc - m), 0.0)
        den = jnp.sum(e, axis=0, keepdims=True)
        pc = (e / jnp.where(den > 0.0, den, 1.0)).astype(BF16)
        o_cmp.append(jnp.dot(vct_ref[...], pc, preferred_element_type=F32))

        p_slc = jnp.zeros((LANES, tq), F32)
        for g in range(GQA_GROUP):
            p_slc = p_slc + jnp.dot(ovt_ref[kv], pc[:, g * tq:(g + 1) * tq], preferred_element_type=F32)
        r0 = a0 + L_BLK
        p_slc = p_slc[r0:r0 + MAX_SEL_BLOCKS, :]
        jb = lax.broadcasted_iota(I32, (MAX_SEL_BLOCKS, 1), 0)
        forced = (jb == 0) | (jb == cur) | (jb == cur - 1)
        score = jnp.where(jb > cur, -1.0, jnp.where(forced, SEL_FORCE, p_slc))
        score = jnp.where(jb < ns, score, -2.0)
        rank = jnp.zeros((MAX_SEL_BLOCKS, tq), F32)
        for b2 in range(ns):
            other = score[b2:b2 + 1, :]
            beats = (other > score) | ((other == score) & (jb > b2))
            rank = rank + jnp.where(beats, 1.0, 0.0)
        selq = jnp.where(rank < float(n_sel), 0.0, NEG)
        pieces = [selq]
        if r0 > 0:
            pieces.insert(0, jnp.zeros((r0, tq), F32))
        if LANES - r0 - MAX_SEL_BLOCKS > 0:
            pieces.append(jnp.zeros((LANES - r0 - MAX_SEL_BLOCKS, tq), F32))
        selq = jnp.concatenate(pieces, axis=0).T
        in_blk = (lane >= r0) & (lane < r0 + MAX_SEL_BLOCKS)
        qsels.append(jnp.concatenate([jnp.where(in_blk, selq, x).astype(BF16) for x in qbase], axis=0))

    def step(j, carry, bias):
        start = pl.multiple_of(j * SEL_KEYS, SEL_KEYS)
        scores = [_dot_nt(ksa_ref[kv, pl.ds(start, SEL_KEYS), :], qsels[kv]) for kv in range(KV_HEADS)]
        if bias is not None:
            scores = [sc + bias for sc in scores]
        return _softmax_steps(carry, scores, [vst_ref[j]] * KV_HEADS)

    init = tuple(_softmax_init(GQA_GROUP * tq) for _ in range(KV_HEADS))
    carry = lax.fori_loop(0, nfull, lambda j, c: step(j, c, None), init)
    carry = step(nfull, carry, dcausal)
    o_sel = [acc / l for (_, l, acc) in carry]

    o_win = _banded_attend(q, kwp_ref, vwt_ref, i, NSA_WINDOW, slopes, None)

    gates = cg_ref[0].T
    for g in range(GQA_GROUP):
        total = jnp.zeros((LANES, tq), F32)
        for r, branch in enumerate((o_cmp, o_sel, o_win)):
            c0 = FOX_HEADS + (r * GQA_GROUP + g) * KV_HEADS
            gate = jnp.where(row < HEAD_DIM, gates[c0:c0 + 1, :], gates[c0 + 1:c0 + 2, :])
            total = total + gate * _merge_kv(branch, g, tq, row)
        o_ref[0, :, g * LANES:(g + 1) * LANES] = total.T.astype(BF16)


def _nsa(hb3, kc, vc, cg, ovt, slopes):
    b, s, _ = hb3.shape
    nc = kc.shape[1]
    w = GQA_GROUP * LANES
    n_sel = min(SEL_TOP_N, s // SEL_BLOCK)
    kvspec = lambda c: pl.BlockSpec((1, s, LANES), lambda bi, i: (bi, 0, c // LANES))
    return pl.pallas_call(
        functools.partial(_nsa_kernel, slopes=slopes, n_sel=n_sel),
        grid=(b, s // Q_TILE),
        in_specs=[pl.BlockSpec((1, Q_TILE, w), lambda bi, i: (bi, i, C_NQ // w)),
                  kvspec(C_NKS), kvspec(C_NVS), kvspec(C_NKW), kvspec(C_NVW),
                  pl.BlockSpec((1, nc, LANES), lambda bi, i: (bi, 0, 0)),
                  pl.BlockSpec((1, nc, LANES), lambda bi, i: (bi, 0, 0)),
                  pl.BlockSpec((1, Q_TILE, LANES), lambda bi, i: (bi, i, 0)),
                  pl.BlockSpec(ovt.shape, lambda bi, i: (0, 0, 0))],
        out_specs=pl.BlockSpec((1, Q_TILE, w), lambda bi, i: (bi, i, 0)),
        out_shape=jax.ShapeDtypeStruct((b, s, w), BF16),
        scratch_shapes=[pltpu.VMEM((KV_HEADS, s, LANES), BF16),
                        pltpu.VMEM((s // SEL_KEYS, LANES, SEL_KEYS), BF16),
                        pltpu.VMEM((KV_HEADS, s + NSA_WINDOW, LANES), BF16),
                        pltpu.VMEM(((s + NSA_WINDOW) // LANES, LANES, LANES), BF16),
                        pltpu.VMEM((KV_HEADS, nc, LANES), BF16),
                        pltpu.VMEM((LANES, nc), BF16)],
        compiler_params=_params("arbitrary", "arbitrary"),
        name="nsa",
    )(hb3, hb3, hb3, hb3, hb3, kc, vc, cg, ovt)


def _layer_norm(y, g, b):
    mu = jnp.mean(y, axis=-1, keepdims=True)
    var = jnp.mean(jnp.square(y - mu), axis=-1, keepdims=True)
    return (y - mu) * lax.rsqrt(var + LN_EPS) * g + b


def _mix_kernel(x_ref, of_ref, os_ref, on_ref, wf_ref, ws_ref, wn_ref, g_ref, b_ref, rw_ref, rb_ref,
                x1_ref, ri_ref, rwt_ref, cnt_ref, carry_ref, *, alpha):
    i = pl.program_id(0)
    tm = x_ref.shape[0]
    lane = _lane_iota()

    @pl.when(i == 0)
    def _init():
        carry_ref[...] = jnp.zeros_like(carry_ref)

    mix = jnp.dot(of_ref[...], wf_ref[...], preferred_element_type=F32)
    mix = mix + jnp.dot(os_ref[...], ws_ref[...], preferred_element_type=F32)
    mix = mix + jnp.dot(on_ref[...], wn_ref[...], preferred_element_type=F32)
    x1 = _layer_norm(alpha * x_ref[...] + mix, g_ref[...], b_ref[...])
    x1_ref[...] = x1

    logits = jnp.dot(x1.astype(BF16), rw_ref[...], preferred_element_type=F32) + rb_ref[...]
    v = jnp.where(lane < N_EXPERTS, logits, NEG)
    tops, hots = [], []
    for _ in range(TOP_K):
        mx = jnp.max(v, axis=-1, keepdims=True)
        idx = jnp.min(jnp.where(v == mx, lane, LANES), axis=-1, keepdims=True)
        hot = lane == idx
        tops.append((mx, idx))
        hots.append(hot)
        v = jnp.where(hot, 2.0 * NEG, v)
    es = [jnp.exp(mx - tops[0][0]) for mx, _ in tops]
    den = es[0] + es[1] + es[2] + es[3]

    member = jnp.zeros((tm, LANES), F32)
    for hot in hots:
        member = member + jnp.where(hot, 1.0, 0.0)
    r = lax.broadcasted_iota(I32, (tm, tm), 0)
    c = lax.broadcasted_iota(I32, (tm, tm), 1)
    strict = jnp.where(c < r, 1.0, 0.0).astype(BF16)
    before = jnp.dot(strict, member.astype(BF16), preferred_element_type=F32) + carry_ref[0:1, :]
    ri = jnp.zeros((tm, LANES), F32)
    rwt = jnp.zeros((tm, LANES), F32)
    for k in range(TOP_K):
        rank = jnp.sum(jnp.where(hots[k], before, 0.0), axis=-1, keepdims=True)
        ri = jnp.where(lane == k, tops[k][1].astype(F32), ri)
        ri = jnp.where(lane == TOP_K + k, rank, ri)
        rwt = jnp.where(lane == k, es[k] / den, rwt)
    ri_ref[...] = ri.astype(I32)
    rwt_ref[...] = rwt
    new = carry_ref[0:1, :] + jnp.sum(member, axis=0, keepdims=True)
    carry_ref[...] = jnp.zeros_like(carry_ref) + new
    cnt_ref[...] = jnp.zeros_like(cnt_ref) + new


def _mix(x2, of, osw, on, wf, ws, wn, g, b, rw, rb, alpha, tm):
    n, d = x2.shape
    full = lambda a: pl.BlockSpec(a.shape, lambda i: (0,) * a.ndim)
    rows = lambda a: pl.BlockSpec((tm, a.shape[1]), lambda i: (i, 0))
    return pl.pallas_call(
        functools.partial(_mix_kernel, alpha=alpha),
        grid=(n // tm,),
        in_specs=[rows(x2), rows(of), rows(osw), rows(on), full(wf), full(ws), full(wn),
                  full(g), full(b), full(rw), full(rb)],
        out_specs=[pl.BlockSpec((tm, d), lambda i: (i, 0)),
                   pl.BlockSpec((tm, LANES), lambda i: (i, 0)),
                   pl.BlockSpec((tm, LANES), lambda i: (i, 0)),
                   pl.BlockSpec((8, LANES), lambda i: (0, 0))],
        out_shape=[jax.ShapeDtypeStruct((n, d), F32), jax.ShapeDtypeStruct((n, LANES), I32),
                   jax.ShapeDtypeStruct((n, LANES), F32), jax.ShapeDtypeStruct((8, LANES), F32)],
        scratch_shapes=[pltpu.VMEM((8, LANES), F32)],
        compiler_params=_params("arbitrary"),
        name="mix_ln_router",
    )(x2, of, osw, on, wf, ws, wn, g, b, rw, rb)


def _ffn_kernel(be_ref, nu_ref, tokc_ref, tokn_ref, posp_ref, x_hbm, wg_ref, wu_ref, bg_ref, bu_ref,
                wd_ref, bd_ref, yk_hbm, xb0, xb1, yb0, yb1, gsem, ssem, *, rows, chunks):
    del be_ref
    i = pl.program_id(0)
    nu = nu_ref[0]
    xbufs, ybufs = (xb0, xb1), (yb0, yb1)
    d = xb0.shape[1]
    fc = wg_ref.shape[2] // chunks
    per = rows // chunks

    def gather(tok_ref, r, dst):
        return pltpu.make_async_copy(x_hbm.at[pl.ds(tok_ref[r], 1)], xbufs[dst].at[pl.ds(r, 1)], gsem.at[dst])

    def gather_all(dst):
        return pltpu.make_async_copy(x_hbm.at[pl.ds(0, rows)], xbufs[dst], gsem.at[dst])

    def scatter_all(src):
        return pltpu.make_async_copy(ybufs[src], yk_hbm.at[pl.ds(0, rows)], ssem.at[src])

    @pl.when(i == 0)
    def _prologue():
        yb1[...] = jnp.zeros((rows, d), F32)

        def issue(r, _):
            gather(tokc_ref, r, 0).start()
            return 0

        lax.fori_loop(0, rows, issue, 0)

    def run(slot):
        other = 1 - slot
        gather_all(slot).wait()

        @pl.when(i > 0)
        def _():
            scatter_all(slot).wait()

        x = xbufs[slot][...].astype(BF16)
        y = jnp.zeros((rows, d), F32) + bd_ref[0]
        for c in range(chunks):
            cs = slice(c * fc, (c + 1) * fc)
            gate = jnp.dot(x, wg_ref[0, :, cs], preferred_element_type=F32) + bg_ref[0, :, cs]
            up = jnp.dot(x, wu_ref[0, :, cs], preferred_element_type=F32) + bu_ref[0, :, cs]
            gate = jnp.minimum(gate, SWIGLU_LIMIT)
            up = jnp.clip(up, -SWIGLU_LIMIT, SWIGLU_LIMIT)
            glu = gate * jax.nn.sigmoid(gate * SWIGLU_ALPHA)
            act = ((up + 1.0) * glu).astype(BF16)
            y = y + jnp.dot(act, wd_ref[0, cs, :], preferred_element_type=F32)
            for r in range(c * per, (c + 1) * per):
                gather(tokn_ref, r, other).start()
                pltpu.make_async_copy(ybufs[other].at[pl.ds(r, 1)], yk_hbm.at[pl.ds(posp_ref[r], 1)],
                                      ssem.at[other]).start()
        ybufs[slot][...] = y

        @pl.when(i == nu)
        def _epilogue():
            gather_all(other).wait()
            scatter_all(other).wait()

    for slot in range(2):
        pl.when((i <= nu) & (i % 2 == slot))(functools.partial(run, slot))


def _ffn(blk_e, n_used, tok, pos_ext, x1, wg, wu, bg, bu, wd, bd, rows):
    n, d = x1.shape
    nblk = tok.shape[0] // rows
    wspec = lambda a: pl.BlockSpec((1,) + a.shape[1:], lambda i, be, nu: (be[i], 0, 0))
    smem = lambda fn: pl.BlockSpec((rows,), fn, memory_space=pltpu.SMEM)
    grid_spec = pltpu.PrefetchScalarGridSpec(
        num_scalar_prefetch=2,
        grid=(nblk,),
        in_specs=[smem(lambda i, be, nu: (i,)),
                  smem(lambda i, be, nu: (jnp.minimum(i + 1, nblk - 1),)),
                  smem(lambda i, be, nu: (i,)),
                  pl.BlockSpec(memory_space=pl.ANY),
                  wspec(wg), wspec(wu), wspec(bg), wspec(bu), wspec(wd), wspec(bd)],
        out_specs=pl.BlockSpec(memory_space=pl.ANY),
        scratch_shapes=[pltpu.VMEM((rows, d), F32)] * 4
                       + [pltpu.SemaphoreType.DMA((2,)), pltpu.SemaphoreType.DMA((2,))],
    )
    return pl.pallas_call(
        functools.partial(_ffn_kernel, rows=rows, chunks=4),
        grid_spec=grid_spec,
        out_shape=jax.ShapeDtypeStruct((n * TOP_K + rows, d), F32),
        compiler_params=_params("arbitrary"),
        name="moe_ffn",
    )(blk_e, n_used, tok, tok, pos_ext, x1, wg, wu, bg, bu, wd, bd)


def _combine_kernel(x1_ref, w_ref, g_ref, b_ref, *rest, alpha):
    yk_refs, o_ref = rest[:TOP_K], rest[TOP_K]
    w = w_ref[...]
    ffn = jnp.zeros(x1_ref.shape, F32)
    for k in range(TOP_K):
        ffn = ffn + yk_refs[k][...] * w[:, k:k + 1]
    o_ref[...] = _layer_norm(alpha * x1_ref[...] + ffn, g_ref[...], b_ref[...])


def _combine(x1, rwt, g, b, yk, alpha, tokens):
    n, d = x1.shape
    full = lambda a: pl.BlockSpec(a.shape, lambda i: (0,) * a.ndim)
    plane = lambda k: pl.BlockSpec((tokens, d), lambda i: (k * (n // tokens) + i, 0))
    return pl.pallas_call(
        functools.partial(_combine_kernel, alpha=alpha),
        grid=(n // tokens,),
        in_specs=[pl.BlockSpec((tokens, d), lambda i: (i, 0)),
                  pl.BlockSpec((tokens, LANES), lambda i: (i, 0)),
                  full(g), full(b)] + [plane(k) for k in range(TOP_K)],
        out_specs=pl.BlockSpec((tokens, d), lambda i: (i, 0)),
        out_shape=jax.ShapeDtypeStruct((n, d), F32),
        compiler_params=_params("arbitrary"),
        name="moe_combine_ln",
    )(x1, rwt, g, b, *([yk] * TOP_K))


def _gqa_cols(base):
    idx = np.empty(GQA_GROUP * KV_HEADS * HEAD_DIM, np.int64)
    for g in range(GQA_GROUP):
        for kv in range(KV_HEADS):
            dst = (g * KV_HEADS + kv) * HEAD_DIM
            src = (kv * GQA_GROUP + g) * HEAD_DIM
            idx[dst:dst + HEAD_DIM] = base + src + np.arange(HEAD_DIM)
    return idx


def _inproj_layout():
    fw = FOX_HEADS * HEAD_DIM
    qw = SWA_Q_HEADS * HEAD_DIM
    kw = KV_HEADS * HEAD_DIM
    sizes = (fw, fw, fw, FOX_HEADS, qw, kw, kw, qw, kw, kw, kw, kw, kw, kw, 3 * NSA_Q_HEADS)
    off = np.concatenate([[0], np.cumsum(sizes)])
    (o_fq, o_fk, o_fv, o_ff, o_sq, o_sk, o_sv, o_nq, o_nkc, o_nvc, o_nks, o_nvs, o_nkw, o_nvw, o_ng, total) = off
    ar = np.arange
    cols = np.concatenate([
        _gqa_cols(o_sq), _gqa_cols(o_nq), o_fq + ar(fw), o_fk + ar(fw), o_fv + ar(fw),
        o_sk + ar(kw), o_sv + ar(kw), o_nks + ar(kw), o_nvs + ar(kw), o_nkw + ar(kw), o_nvw + ar(kw),
        o_nkc + ar(kw), o_nvc + ar(kw)])
    gate = np.full(LANES, total, np.int64)
    gate[:FOX_HEADS] = o_ff + ar(FOX_HEADS)
    for r in range(3):
        for g in range(GQA_GROUP):
            for kv in range(KV_HEADS):
                gate[FOX_HEADS + (r * GQA_GROUP + g) * KV_HEADS + kv] = o_ng + (kv * GQA_GROUP + g) * 3 + r
    cols = np.concatenate([cols, gate])
    scale = np.ones(cols.shape[0], np.float32)
    scale[C_SQ:C_SQ + qw] = HEAD_DIM ** -0.5
    scale[C_NQ:C_NQ + qw] = HEAD_DIM ** -0.5
    scale[C_FQ:C_FQ + fw] = HEAD_DIM ** -0.5
    return cols, scale


def _outproj_rows():
    fw = FOX_HEADS * HEAD_DIM
    qw = SWA_Q_HEADS * HEAD_DIM
    return np.arange(fw), _gqa_cols(fw), _gqa_cols(fw + qw)


def _overlap_matrix_t(s_len, nc):
    ns = s_len // SEL_BLOCK
    cmp_start = np.arange(nc) * CMP_STRIDE
    sel_start = np.arange(ns) * SEL_BLOCK
    ov = np.clip(np.minimum(cmp_start[:, None] + CMP_BLOCK, sel_start[None, :] + SEL_BLOCK)
                 - np.maximum(cmp_start[:, None], sel_start[None, :]), 0, None) / CMP_BLOCK
    ov[(s_len - CMP_BLOCK) // CMP_STRIDE + 1:] = 0.0
    out = np.zeros((KV_HEADS, LANES, nc), np.float32)
    for kv in range(KV_HEADS):
        r0 = _spare(kv) + L_BLK
        out[kv, r0:r0 + ns, :] = ov.T
    return out


def _compress_weights(pe, w1, w2):
    pe2 = jnp.concatenate([pe, pe], axis=-1)
    w1r = w1.reshape(2, CMP_BLOCK, HEAD_DIM, HEAD_DIM)
    z = jnp.zeros_like(w1r)
    w1b = jnp.concatenate([jnp.concatenate([w1r, z], axis=-1), jnp.concatenate([z, w1r], axis=-1)], axis=-2)
    z2 = jnp.zeros_like(w2)
    w2b = jnp.concatenate([jnp.concatenate([w2, z2], axis=-1), jnp.concatenate([z2, w2], axis=-1)], axis=-2)
    return pe2, w1b.astype(BF16), w2b.astype(BF16)


def kernel(x, w_in, b_in, sinks, cmp_pe, cmp_w1, cmp_w2, w_out, ln1_g, ln1_b, router_w, router_b,
           w_gate_up, b_gate_up, w_down, b_down, ln2_g, ln2_b):
    depth = w_in.shape[0]
    bsz, s_len, d = x.shape
    n = bsz * s_len
    nk = n * TOP_K
    alpha = float((2.0 * depth) ** 0.25)
    assert s_len % SEL_KEYS == 0 and s_len // SEL_BLOCK <= MAX_SEL_BLOCKS and n % 512 == 0
    slopes_swa, slopes_nsa = _alibi()
    cols, scale = _inproj_layout()
    rows_f, rows_s, rows_n = _outproj_rows()
    nc = s_len // CMP_STRIDE
    ovt = jnp.asarray(_overlap_matrix_t(s_len, nc), BF16)
    nblk = -(-nk // MOE_ROWS) + N_EXPERTS + 1
    p_rows = nblk * MOE_ROWS
    f = w_down.shape[2]

    x2 = x.reshape(n, d)
    for l in range(depth):
        w_ext = jnp.concatenate([w_in[l], jnp.zeros((d, 1), F32)], axis=1)
        b_ext = jnp.concatenate([b_in[l], jnp.zeros((1,), F32)])
        w_l = (jnp.take(w_ext, cols, axis=1) * scale).astype(BF16)
        b_l = (jnp.take(b_ext, cols) * scale).reshape(1, -1)
        hb, hf = _inproj(x2, w_l, b_l, 512)
        hb3 = hb.reshape(bsz, s_len, CB)
        hf3 = hf.reshape(bsz, s_len, CF)
        cg = _gates(hf3)
        o_fox = _fox(hb3, cg)
        o_swa = _swa(hb3, sinks[l], slopes_swa)
        grp = hf3[:, :, :2 * LANES].reshape(bsz, nc, CMP_STRIDE * 2 * LANES)
        pe2, w1b, w2b = _compress_weights(cmp_pe[l], cmp_w1[l], cmp_w2[l])
        kc, vc = _compress(grp, pe2, w1b, w2b)
        o_nsa = _nsa(hb3, kc, vc, cg, ovt, slopes_nsa)

        wo = w_out[l].astype(BF16)
        rw = jnp.concatenate([router_w[l], jnp.zeros((d, LANES - N_EXPERTS), F32)], axis=1).astype(BF16)
        rb = jnp.concatenate([router_b[l], jnp.zeros((LANES - N_EXPERTS,), F32)]).reshape(1, LANES)
        x1, ri, rwt, cnt = _mix(
            x2, o_fox.reshape(n, -1), o_swa.reshape(n, -1), o_nsa.reshape(n, -1),
            jnp.take(wo, rows_f, axis=0), jnp.take(wo, rows_s, axis=0), jnp.take(wo, rows_n, axis=0),
            ln1_g[l].reshape(1, d), ln1_b[l].reshape(1, d), rw, rb, alpha, 256)

        counts = cnt[0, :N_EXPERTS].astype(I32)
        padded = (counts + MOE_ROWS - 1) // MOE_ROWS * MOE_ROWS
        pad_end = jnp.cumsum(padded)
        pad_start = pad_end - padded
        dest = (jnp.take(pad_start, ri[:, :TOP_K]) + ri[:, TOP_K:2 * TOP_K]).reshape(nk)
        blk_start = jnp.arange(nblk, dtype=I32) * MOE_ROWS
        blk_e = jnp.minimum(jnp.sum((pad_end[None, :] <= blk_start[:, None]).astype(I32), axis=1),
                            N_EXPERTS - 1)
        n_used = (pad_end[-1:] // MOE_ROWS).astype(I32)
        src = jnp.full((p_rows,), -1, I32).at[dest].set(jnp.arange(nk, dtype=I32))
        is_pad = src < 0
        dump = nk + jnp.arange(p_rows, dtype=I32) % MOE_ROWS
        pos = jnp.where(is_pad, dump, (src % TOP_K) * n + src // TOP_K)
        tok = jnp.where(is_pad, 0, src // TOP_K)
        pos_ext = jnp.concatenate([dump[:MOE_ROWS], pos])

        wgu = w_gate_up[l].reshape(N_EXPERTS, d, f, 2)
        bgu = b_gate_up[l].reshape(N_EXPERTS, 1, f, 2)
        yk = _ffn(blk_e, n_used, tok, pos_ext, x1, wgu[..., 0].astype(BF16), wgu[..., 1].astype(BF16),
                  bgu[..., 0], bgu[..., 1], w_down[l].astype(BF16), b_down[l].reshape(N_EXPERTS, 1, d),
                  MOE_ROWS)
        x2 = _combine(x1, rwt, ln2_g[l].reshape(1, d), ln2_b[l].reshape(1, d), yk, alpha, 256)
    return x2.reshape(bsz, s_len, d)
```

```python
import functools

import numpy as np
import jax
import jax.numpy as jnp
from jax import lax
from jax.experimental import pallas as pl
from jax.experimental.pallas import tpu as pltpu

F32 = jnp.float32
BF16 = jnp.bfloat16
I32 = jnp.int32

HEAD_DIM = 64
LANES = 128
N_HEADS = 16
SWA_Q_HEADS = 6
NSA_Q_HEADS = 6
FOX_HEADS = 4
KV_HEADS = 2
GQA_GROUP = 3
SWA_WINDOW = 128
CMP_BLOCK = 32
CMP_STRIDE = 16
SEL_BLOCK = 64
SEL_TOP_N = 8
NSA_WINDOW = 512
SEL_FORCE = 1.0e4
N_EXPERTS = 32
TOP_K = 4
SWIGLU_LIMIT = 7.0
SWIGLU_ALPHA = 1.702
LN_EPS = 1e-5
NEG = -1.0e30

Q_TILE = 256
FOX_TILE = 256
SEL_KEYS = 512
MAX_SEL_BLOCKS = 32
MOE_ROWS = 512
VMEM_LIMIT = 56 * 1024 * 1024

L_BLK = 0
L_QT = 32
L_KH = 35
L_KL = 38
L_PAD = 41

C_SQ, C_NQ, C_FQ, C_FK, C_FV = 0, 384, 768, 1024, 1280
C_SK, C_SV, C_NKS, C_NVS, C_NKW, C_NVW = 1536, 1664, 1792, 1920, 2048, 2176
CB = 2304
CF = 384


def _alibi():
    n = SWA_Q_HEADS + NSA_Q_HEADS
    s = (2.0 ** (-8.0 * np.arange(1, n + 1) / n)).astype(np.float32)
    return s[0::2].reshape(KV_HEADS, GQA_GROUP), s[1::2].reshape(KV_HEADS, GQA_GROUP)


def _bf16_parts(v):
    v = np.float32(v)
    out = []
    for _ in range(3):
        p = np.float32(np.asarray(v, dtype=BF16))
        out.append(float(p))
        v = np.float32(v - p)
    return out


def _split3(x):
    hi = x.astype(BF16).astype(F32)
    r = x - hi
    mid = r.astype(BF16).astype(F32)
    lo = (r - mid).astype(BF16).astype(F32)
    return hi, mid, lo


def _lane_iota():
    return lax.broadcasted_iota(I32, (1, LANES), 1)


def _row_iota():
    return lax.broadcasted_iota(I32, (LANES, 1), 0)


def _spare(kv_half):
    return HEAD_DIM if kv_half == 0 else 0


def _dot_nt(a, b):
    return lax.dot_general(a, b, (((1,), (1,)), ((), ())), preferred_element_type=F32)


def _transposed(x):
    return x.astype(F32).T.astype(BF16)


def _params(*sem):
    return pltpu.CompilerParams(dimension_semantics=sem, vmem_limit_bytes=VMEM_LIMIT)


def _inproj_kernel(x_ref, w_ref, b_ref, hb_ref, hf_ref):
    acc = jnp.dot(x_ref[...].astype(BF16), w_ref[...], preferred_element_type=F32) + b_ref[...]
    hb_ref[...] = acc[:, :CB].astype(BF16)
    hf_ref[...] = acc[:, CB:]


def _inproj(x2, w, b, tm):
    n, d = x2.shape
    ct = w.shape[1]
    return pl.pallas_call(
        _inproj_kernel,
        grid=(n // tm,),
        in_specs=[pl.BlockSpec((tm, d), lambda i: (i, 0)),
                  pl.BlockSpec((d, ct), lambda i: (0, 0)),
                  pl.BlockSpec((1, ct), lambda i: (0, 0))],
        out_specs=[pl.BlockSpec((tm, CB), lambda i: (i, 0)),
                   pl.BlockSpec((tm, CF), lambda i: (i, 0))],
        out_shape=[jax.ShapeDtypeStruct((n, CB), BF16), jax.ShapeDtypeStruct((n, CF), F32)],
        compiler_params=_params("arbitrary"),
        name="inproj",
    )(x2, w, b)


def _gates_kernel(g_ref, o_ref, *, chunk):
    s = g_ref.shape[1]
    lane = _lane_iota()
    r = lax.broadcasted_iota(I32, (chunk, chunk), 0)
    c = lax.broadcasted_iota(I32, (chunk, chunk), 1)
    tri = jnp.where(c <= r, 1.0, 0.0).astype(BF16)

    def body(j, carry):
        x = g_ref[0, pl.ds(pl.multiple_of(j * chunk, chunk), chunk), :]
        ls = jnp.minimum(x, 0.0) - jnp.log1p(jnp.exp(-jnp.abs(x)))
        cs = carry
        for part in _split3(ls):
            cs = cs + jnp.dot(tri, part.astype(BF16), preferred_element_type=F32)
        sg = jax.nn.sigmoid(x)
        o_ref[0, pl.ds(pl.multiple_of(j * chunk, chunk), chunk), :] = jnp.where(lane < FOX_HEADS, cs, sg)
        return cs[chunk - 1:chunk, :]

    lax.fori_loop(0, s // chunk, body, jnp.zeros((1, LANES), F32))


def _gates(hf3):
    b, s, _ = hf3.shape
    chunk = min(256, s)
    return pl.pallas_call(
        functools.partial(_gates_kernel, chunk=chunk),
        grid=(b,),
        in_specs=[pl.BlockSpec((1, s, LANES), lambda i: (i, 0, 2))],
        out_specs=pl.BlockSpec((1, s, LANES), lambda i: (i, 0, 0)),
        out_shape=jax.ShapeDtypeStruct((b, s, LANES), F32),
        compiler_params=_params("arbitrary"),
        name="gates",
    )(hf3)


def _softmax_steps(carries, scores, values):
    stats = []
    for (m, l, _), sc in zip(carries, scores):
        m_new = jnp.maximum(m, jnp.max(sc, axis=0, keepdims=True))
        alpha = jnp.exp(m - m_new)
        pr = jnp.exp(sc - m_new)
        stats.append((m_new, alpha * l + jnp.sum(pr, axis=0, keepdims=True), alpha, pr.astype(BF16)))
    return tuple((m, l, alpha * acc + jnp.dot(vt, pr, preferred_element_type=F32))
                 for (m, l, alpha, pr), (_, _, acc), vt in zip(stats, carries, values))


def _softmax_init(queries):
    return (jnp.full((1, queries), NEG, F32), jnp.zeros((1, queries), F32), jnp.zeros((LANES, queries), F32))


def _fox_kernel(q_ref, k_ref, v_ref, cq_ref, ck_ref, o_ref, ka_ref, vt_ref, *, tile):
    i = pl.program_id(1)
    s_len = k_ref.shape[1]
    lane = _lane_iota()
    npair = FOX_HEADS // 2

    @pl.when(i == 0)
    def _fill():
        for h in range(FOX_HEADS):
            p, half = divmod(h, 2)
            a0 = _spare(half)
            k = k_ref[0, :, p * LANES:(p + 1) * LANES].astype(F32)
            hi, mid, lo = _split3(-ck_ref[0, :, h:h + 1])
            ka = jnp.where((lane >= a0 + L_QT) & (lane < a0 + L_QT + 3), 1.0, k)
            ka = jnp.where(lane == a0 + L_KH, hi, ka)
            ka = jnp.where(lane == a0 + L_KH + 1, mid, ka)
            ka = jnp.where(lane == a0 + L_KH + 2, lo, ka)
            ka_ref[h] = ka.astype(BF16)
        for p in range(npair):
            for c in range(s_len // tile):
                vt_ref[p, c] = _transposed(v_ref[0, c * tile:(c + 1) * tile, p * LANES:(p + 1) * LANES])

    qas = []
    for h in range(FOX_HEADS):
        p, half = divmod(h, 2)
        a0 = _spare(half)
        q = q_ref[0, :, p * LANES:(p + 1) * LANES].astype(F32)
        in_half = (lane < HEAD_DIM) if half == 0 else (lane >= HEAD_DIM)
        hi, mid, lo = _split3(cq_ref[0, :, h:h + 1])
        qa = jnp.where(in_half, q, 0.0)
        qa = jnp.where(lane == a0 + L_QT, hi, qa)
        qa = jnp.where(lane == a0 + L_QT + 1, mid, qa)
        qa = jnp.where(lane == a0 + L_QT + 2, lo, qa)
        qas.append(jnp.where((lane >= a0 + L_KH) & (lane < a0 + L_KH + 3), 1.0, qa).astype(BF16))

    key = lax.broadcasted_iota(I32, (tile, tile), 0)
    qry = lax.broadcasted_iota(I32, (tile, tile), 1)
    causal = jnp.where(key <= qry, 0.0, NEG)

    def step(j, carry, masked):
        start = pl.multiple_of(j * tile, tile)
        scores = [_dot_nt(ka_ref[h, pl.ds(start, tile), :], qas[h]) for h in range(FOX_HEADS)]
        if masked:
            scores = [sc + causal for sc in scores]
        return _softmax_steps(carry, scores, [vt_ref[h // 2, j] for h in range(FOX_HEADS)])

    init = tuple(_softmax_init(tile) for _ in range(FOX_HEADS))
    carry = lax.fori_loop(0, i, functools.partial(step, masked=False), init)
    carry = step(i, carry, True)
    row = _row_iota()
    for p in range(npair):
        (_, l0, a0_), (_, l1, a1_) = carry[2 * p], carry[2 * p + 1]
        o_ref[0, :, p * LANES:(p + 1) * LANES] = jnp.where(row < HEAD_DIM, a0_ / l0, a1_ / l1).T.astype(BF16)


def _fox(hb3, cg):
    b, s, _ = hb3.shape
    tile = min(FOX_TILE, s)
    w = FOX_HEADS * HEAD_DIM
    return pl.pallas_call(
        functools.partial(_fox_kernel, tile=tile),
        grid=(b, s // tile),
        in_specs=[pl.BlockSpec((1, tile, w), lambda bi, i: (bi, i, C_FQ // w)),
                  pl.BlockSpec((1, s, w), lambda bi, i: (bi, 0, C_FK // w)),
                  pl.BlockSpec((1, s, w), lambda bi, i: (bi, 0, C_FV // w)),
                  pl.BlockSpec((1, tile, LANES), lambda bi, i: (bi, i, 0)),
                  pl.BlockSpec((1, s, LANES), lambda bi, i: (bi, 0, 0))],
        out_specs=pl.BlockSpec((1, tile, w), lambda bi, i: (bi, i, 0)),
        out_shape=jax.ShapeDtypeStruct((b, s, w), BF16),
        scratch_shapes=[pltpu.VMEM((FOX_HEADS, s, LANES), BF16),
                        pltpu.VMEM((FOX_HEADS // 2, s // tile, LANES, tile), BF16)],
        compiler_params=_params("arbitrary", "arbitrary"),
        name="fox",
    )(hb3, hb3, hb3, cg, cg)


def _fill_banded(k_ref, v_ref, kp_ref, vt_ref, window):
    s_len = k_ref.shape[1]
    lane = _lane_iota()
    k = k_ref[0].astype(F32)
    pos = lax.broadcasted_iota(I32, (s_len, 1), 0) + window
    pos_hi = ((pos >> 8) << 8).astype(F32)
    pos_lo = (pos & 255).astype(F32)
    for kv in range(KV_HEADS):
        a0 = _spare(kv)
        ka = jnp.where((lane >= a0 + L_QT) & (lane < a0 + L_QT + 3), 1.0, k)
        ka = jnp.where((lane >= a0 + L_KH) & (lane < a0 + L_KH + 3), pos_hi, ka)
        ka = jnp.where((lane >= a0 + L_KL) & (lane < a0 + L_KL + 3), pos_lo, ka)
        ka = jnp.where(lane == a0 + L_PAD, 0.0, ka)
        kp_ref[kv, pl.ds(window, s_len), :] = ka.astype(BF16)
        pad = jnp.where(lane == a0 + L_PAD, NEG, 0.0) + jnp.zeros((window, LANES), F32)
        kp_ref[kv, pl.ds(0, window), :] = pad.astype(BF16)
    npad = window // LANES
    for c in range(npad):
        vt_ref[c] = jnp.zeros((LANES, LANES), BF16)
    for c in range(s_len // LANES):
        vt_ref[npad + c] = _transposed(v_ref[0, c * LANES:(c + 1) * LANES, :])


def _alibi_query(q, kv, g, t_pos, slopes, lane, pad_lane):
    a0 = _spare(kv)
    in_half = (lane < HEAD_DIM) if kv == 0 else (lane >= HEAD_DIM)
    sl = float(slopes[kv][g])
    hi, mid, lo = _split3(t_pos * (-sl))
    qa = jnp.where(in_half, q.astype(F32), 0.0)
    qa = jnp.where(lane == a0 + L_QT, hi, qa)
    qa = jnp.where(lane == a0 + L_QT + 1, mid, qa)
    qa = jnp.where(lane == a0 + L_QT + 2, lo, qa)
    for c, part in enumerate(_bf16_parts(sl)):
        qa = jnp.where((lane == a0 + L_KH + c) | (lane == a0 + L_KL + c), part, qa)
    if pad_lane:
        qa = jnp.where(lane == a0 + L_PAD, 1.0, qa)
    return qa


def _banded_attend(q, kp_ref, vt_ref, i, window, slopes, sink_ref):
    tq = q.shape[0]
    span = window + tq
    lane = _lane_iota()
    start = pl.multiple_of(i * tq, tq)
    t_pad = (lax.broadcasted_iota(I32, (tq, 1), 0) + i * tq + window).astype(F32)
    key = lax.broadcasted_iota(I32, (span, tq), 0)
    qry = lax.broadcasted_iota(I32, (span, tq), 1)
    band = jnp.where((key > qry) & (key <= qry + window), 0.0, NEG)
    band = jnp.concatenate([band] * GQA_GROUP, axis=1)
    outs = []
    for kv in range(KV_HEADS):
        qst = jnp.concatenate(
            [_alibi_query(q[:, g * LANES:(g + 1) * LANES], kv, g, t_pad, slopes, lane, True).astype(BF16)
             for g in range(GQA_GROUP)], axis=0)
        sc = _dot_nt(kp_ref[kv, pl.ds(start, span), :], qst) + band
        m = jnp.max(sc, axis=0, keepdims=True)
        if sink_ref is not None:
            sink = jnp.concatenate([jnp.full((1, tq), sink_ref[kv * GQA_GROUP + g], F32)
                                    for g in range(GQA_GROUP)], axis=1)
            m = jnp.maximum(m, sink)
        pr = jnp.exp(sc - m)
        den = jnp.sum(pr, axis=0, keepdims=True)
        if sink_ref is not None:
            den = den + jnp.exp(sink - m)
        pr = pr.astype(BF16)
        o = jnp.zeros((LANES, GQA_GROUP * tq), F32)
        base = i * (tq // LANES)
        for c in range(0, span // LANES, 2):
            wide = min(2, span // LANES - c)
            vt = jnp.concatenate([vt_ref[base + c + j] for j in range(wide)], axis=1)
            o = o + jnp.dot(vt, pr[c * LANES:(c + wide) * LANES, :], preferred_element_type=F32)
        outs.append(o / den)
    return outs


def _merge_kv(outs, g, tq, row):
    return jnp.where(row < HEAD_DIM, outs[0][:, g * tq:(g + 1) * tq], outs[1][:, g * tq:(g + 1) * tq])


def _swa_kernel(sink_ref, q_ref, k_ref, v_ref, o_ref, kp_ref, vt_ref, *, slopes):
    i = pl.program_id(1)
    tq = q_ref.shape[1]

    @pl.when(i == 0)
    def _fill():
        _fill_banded(k_ref, v_ref, kp_ref, vt_ref, SWA_WINDOW)

    outs = _banded_attend(q_ref[0], kp_ref, vt_ref, i, SWA_WINDOW, slopes, sink_ref)
    row = _row_iota()
    for g in range(GQA_GROUP):
        o_ref[0, :, g * LANES:(g + 1) * LANES] = _merge_kv(outs, g, tq, row).T.astype(BF16)


def _swa(hb3, sinks, slopes):
    b, s, _ = hb3.shape
    w = GQA_GROUP * LANES
    return pl.pallas_call(
        functools.partial(_swa_kernel, slopes=slopes),
        grid=(b, s // Q_TILE),
        in_specs=[pl.BlockSpec(memory_space=pltpu.SMEM),
                  pl.BlockSpec((1, Q_TILE, w), lambda bi, i: (bi, i, C_SQ // w)),
                  pl.BlockSpec((1, s, LANES), lambda bi, i: (bi, 0, C_SK // LANES)),
                  pl.BlockSpec((1, s, LANES), lambda bi, i: (bi, 0, C_SV // LANES))],
        out_specs=pl.BlockSpec((1, Q_TILE, w), lambda bi, i: (bi, i, 0)),
        out_shape=jax.ShapeDtypeStruct((b, s, w), BF16),
        scratch_shapes=[pltpu.VMEM((KV_HEADS, s + SWA_WINDOW, LANES), BF16),
                        pltpu.VMEM(((s + SWA_WINDOW) // LANES, LANES, LANES), BF16)],
        compiler_params=_params("arbitrary", "arbitrary"),
        name="swa",
    )(sinks, hb3, hb3, hb3)


def _compress_kernel(g_ref, pe_ref, w1_ref, w2_ref, kc_ref, vc_ref):
    nc = g_ref.shape[1]
    for which, out_ref in ((0, kc_ref), (1, vc_ref)):
        ha = jnp.zeros((nc, LANES), F32)
        hb = jnp.zeros((nc, LANES), F32)
        for l in range(CMP_STRIDE):
            x = g_ref[0, :, (2 * l + which) * LANES:(2 * l + which + 1) * LANES]
            xa = (x + pe_ref[which, l:l + 1, :]).astype(BF16)
            xb = (x + pe_ref[which, CMP_STRIDE + l:CMP_STRIDE + l + 1, :]).astype(BF16)
            ha = ha + jnp.dot(xa, w1_ref[which, l], preferred_element_type=F32)
            hb = hb + jnp.dot(xb, w1_ref[which, CMP_STRIDE + l], preferred_element_type=F32)
        hid = ha + pltpu.roll(hb, nc - 1, 0)
        act = 0.5 * hid * (1.0 + jnp.tanh(np.sqrt(2.0 / np.pi).astype(np.float32) * (hid + 0.044715 * (hid * hid * hid))))
        out_ref[0] = jnp.dot(act.astype(BF16), w2_ref[which], preferred_element_type=F32).astype(BF16)


def _compress(g, pe2, w1b, w2b):
    b, nc, gw = g.shape
    return pl.pallas_call(
        _compress_kernel,
        grid=(b,),
        in_specs=[pl.BlockSpec((1, nc, gw), lambda i: (i, 0, 0)),
                  pl.BlockSpec(pe2.shape, lambda i: (0, 0, 0)),
                  pl.BlockSpec(w1b.shape, lambda i: (0, 0, 0, 0)),
                  pl.BlockSpec(w2b.shape, lambda i: (0, 0, 0))],
        out_specs=[pl.BlockSpec((1, nc, LANES), lambda i: (i, 0, 0)),
                   pl.BlockSpec((1, nc, LANES), lambda i: (i, 0, 0))],
        out_shape=[jax.ShapeDtypeStruct((b, nc, LANES), BF16), jax.ShapeDtypeStruct((b, nc, LANES), BF16)],
        compiler_params=_params("arbitrary"),
        name="compress",
    )(g, pe2, w1b, w2b)


def _nsa_kernel(q_ref, ks_ref, vs_ref, kw_ref, vw_ref, kc_ref, vc_ref, cg_ref, ovt_ref, o_ref,
                ksa_ref, vst_ref, kwp_ref, vwt_ref, kca_ref, vct_ref, *, slopes, n_sel):
    i = pl.program_id(1)
    tq = q_ref.shape[1]
    s_len = ks_ref.shape[1]
    nc = kc_ref.shape[1]
    ns = s_len // SEL_BLOCK
    lane = _lane_iota()
    row = _row_iota()

    @pl.when(i == 0)
    def _fill():
        _fill_banded(kw_ref, vw_ref, kwp_ref, vwt_ref, NSA_WINDOW)
        ks = ks_ref[0].astype(F32)
        pos = lax.broadcasted_iota(I32, (s_len, 1), 0)
        pos_hi = ((pos >> 8) << 8).astype(F32)
        pos_lo = (pos & 255).astype(F32)
        blk = pos // SEL_BLOCK
        kc = kc_ref[0].astype(F32)
        cend = lax.broadcasted_iota(I32, (nc, 1), 0) * CMP_STRIDE + (CMP_BLOCK - 1)
        cend_hi = ((cend >> 8) << 8).astype(F32)
        cend_lo = (cend & 255).astype(F32)
        for kv in range(KV_HEADS):
            a0 = _spare(kv)
            ka = jnp.where((lane >= a0 + L_BLK) & (lane < a0 + L_BLK + MAX_SEL_BLOCKS),
                           jnp.where(lane - (a0 + L_BLK) == blk, 1.0, 0.0), ks)
            ka = jnp.where((lane >= a0 + L_QT) & (lane < a0 + L_QT + 3), 1.0, ka)
            ka = jnp.where((lane >= a0 + L_KH) & (lane < a0 + L_KH + 3), pos_hi, ka)
            ka = jnp.where((lane >= a0 + L_KL) & (lane < a0 + L_KL + 3), pos_lo, ka)
            ksa_ref[kv] = ka.astype(BF16)
            ca = jnp.where((lane >= a0 + L_QT) & (lane < a0 + L_QT + 3), 1.0, kc)
            ca = jnp.where((lane >= a0 + L_KH) & (lane < a0 + L_KH + 3), cend_hi, ca)
            ca = jnp.where((lane >= a0 + L_KL) & (lane < a0 + L_KL + 3), cend_lo, ca)
            kca_ref[kv] = ca.astype(BF16)
        for c in range(s_len // SEL_KEYS):
            vst_ref[c] = _transposed(vs_ref[0, c * SEL_KEYS:(c + 1) * SEL_KEYS, :])
        vct_ref[...] = _transposed(vc_ref[0])

    q = q_ref[0]
    t_col = (lax.broadcasted_iota(I32, (tq, 1), 0) + i * tq).astype(F32)
    t_row = lax.broadcasted_iota(I32, (1, tq), 1) + i * tq
    cur = t_row // SEL_BLOCK
    cend = lax.broadcasted_iota(I32, (nc, tq), 0) * CMP_STRIDE + (CMP_BLOCK - 1)
    cvalid = jnp.concatenate([cend <= t_row] * GQA_GROUP, axis=1)
    nfull = (i * tq) // SEL_KEYS
    dkey = lax.broadcasted_iota(I32, (SEL_KEYS, tq), 0) + nfull * SEL_KEYS
    dcausal = jnp.where(dkey <= t_row, 0.0, NEG)
    dcausal = jnp.concatenate([dcausal] * GQA_GROUP, axis=1)

    o_cmp, qsels = [], []
    for kv in range(KV_HEADS):
        a0 = _spare(kv)
        qbase = [_alibi_query(q[:, g * LANES:(g + 1) * LANES], kv, g, t_col, slopes, lane, False)
                 for g in range(GQA_GROUP)]

        qst = jnp.concatenate([x.astype(BF16) for x in qbase], axis=0)
        sc = jnp.where(cvalid, _dot_nt(kca_ref[kv], qst), NEG)
        m = jnp.max(sc, axis=0, keepdims=True)
        m = jnp.where(m > 0.5 * NEG, m, 0.0)
        e = jnp.where(cvalid, jnp.exp(sc - m), 0.0)
        den = jnp.sum(e, axis=0, keepdims=True)
        pc = (e / jnp.where(den > 0.0, den, 1.0)).astype(BF16)
        o_cmp.append(jnp.dot(vct_ref[...], pc, preferred_element_type=F32))

        p_slc = jnp.zeros((LANES, tq), F32)
        for g in range(GQA_GROUP):
            p_slc = p_slc + jnp.dot(ovt_ref[kv], pc[:, g * tq:(g + 1) * tq], preferred_element_type=F32)
        r0 = a0 + L_BLK
        p_slc = p_slc[r0:r0 + MAX_SEL_BLOCKS, :]
        jb = lax.broadcasted_iota(I32, (MAX_SEL_BLOCKS, 1), 0)
        forced = (jb == 0) | (jb == cur) | (jb == cur - 1)
        score = jnp.where(jb > cur, -1.0, jnp.where(forced, SEL_FORCE, p_slc))
        score = jnp.where(jb < ns, score, -2.0)
        rank = jnp.zeros((MAX_SEL_BLOCKS, tq), F32)
        for b2 in range(ns):
            other = score[b2:b2 + 1, :]
            beats = (other > score) | ((other == score) & (jb > b2))
            rank = rank + jnp.where(beats, 1.0, 0.0)
        selq = jnp.where(rank < float(n_sel), 0.0, NEG)
        pieces = [selq]
        if r0 > 0:
            pieces.insert(0, jnp.zeros((r0, tq), F32))
        if LANES - r0 - MAX_SEL_BLOCKS > 0:
            pieces.append(jnp.zeros((LANES - r0 - MAX_SEL_BLOCKS, tq), F32))
        selq = jnp.concatenate(pieces, axis=0).T
        in_blk = (lane >= r0) & (lane < r0 + MAX_SEL_BLOCKS)
        qsels.append(jnp.concatenate([jnp.where(in_blk, selq, x).astype(BF16) for x in qbase], axis=0))

    def step(j, carry, bias):
        start = pl.multiple_of(j * SEL_KEYS, SEL_KEYS)
        scores = [_dot_nt(ksa_ref[kv, pl.ds(start, SEL_KEYS), :], qsels[kv]) for kv in range(KV_HEADS)]
        if bias is not None:
            scores = [sc + bias for sc in scores]
        return _softmax_steps(carry, scores, [vst_ref[j]] * KV_HEADS)

    init = tuple(_softmax_init(GQA_GROUP * tq) for _ in range(KV_HEADS))
    carry = lax.fori_loop(0, nfull, lambda j, c: step(j, c, None), init)
    carry = step(nfull, carry, dcausal)
    o_sel = [acc / l for (_, l, acc) in carry]

    o_win = _banded_attend(q, kwp_ref, vwt_ref, i, NSA_WINDOW, slopes, None)

    gates = cg_ref[0].T
    for g in range(GQA_GROUP):
        total = jnp.zeros((LANES, tq), F32)
        for r, branch in enumerate((o_cmp, o_sel, o_win)):
            c0 = FOX_HEADS + (r * GQA_GROUP + g) * KV_HEADS
            gate = jnp.where(row < HEAD_DIM, gates[c0:c0 + 1, :], gates[c0 + 1:c0 + 2, :])
            total = total + gate * _merge_kv(branch, g, tq, row)
        o_ref[0, :, g * LANES:(g + 1) * LANES] = total.T.astype(BF16)


def _nsa(hb3, kc, vc, cg, ovt, slopes):
    b, s, _ = hb3.shape
    nc = kc.shape[1]
    w = GQA_GROUP * LANES
    n_sel = min(SEL_TOP_N, s // SEL_BLOCK)
    kvspec = lambda c: pl.BlockSpec((1, s, LANES), lambda bi, i: (bi, 0, c // LANES))
    return pl.pallas_call(
        functools.partial(_nsa_kernel, slopes=slopes, n_sel=n_sel),
        grid=(b, s // Q_TILE),
        in_specs=[pl.BlockSpec((1, Q_TILE, w), lambda bi, i: (bi, i, C_NQ // w)),
                  kvspec(C_NKS), kvspec(C_NVS), kvspec(C_NKW), kvspec(C_NVW),
                  pl.BlockSpec((1, nc, LANES), lambda bi, i: (bi, 0, 0)),
                  pl.BlockSpec((1, nc, LANES), lambda bi, i: (bi, 0, 0)),
                  pl.BlockSpec((1, Q_TILE, LANES), lambda bi, i: (bi, i, 0)),
                  pl.BlockSpec(ovt.shape, lambda bi, i: (0, 0, 0))],
        out_specs=pl.BlockSpec((1, Q_TILE, w), lambda bi, i: (bi, i, 0)),
        out_shape=jax.ShapeDtypeStruct((b, s, w), BF16),
        scratch_shapes=[pltpu.VMEM((KV_HEADS, s, LANES), BF16),
                        pltpu.VMEM((s // SEL_KEYS, LANES, SEL_KEYS), BF16),
                        pltpu.VMEM((KV_HEADS, s + NSA_WINDOW, LANES), BF16),
                        pltpu.VMEM(((s + NSA_WINDOW) // LANES, LANES, LANES), BF16),
                        pltpu.VMEM((KV_HEADS, nc, LANES), BF16),
                        pltpu.VMEM((LANES, nc), BF16)],
        compiler_params=_params("arbitrary", "arbitrary"),
        name="nsa",
    )(hb3, hb3, hb3, hb3, hb3, kc, vc, cg, ovt)


def _layer_norm(y, g, b):
    mu = jnp.mean(y, axis=-1, keepdims=True)
    var = jnp.mean(jnp.square(y - mu), axis=-1, keepdims=True)
    return (y - mu) * lax.rsqrt(var + LN_EPS) * g + b


def _mix_kernel(x_ref, of_ref, os_ref, on_ref, wf_ref, ws_ref, wn_ref, g_ref, b_ref, rw_ref, rb_ref,
                x1_ref, ri_ref, rwt_ref, cnt_ref, carry_ref, *, alpha):
    i = pl.program_id(0)
    tm = x_ref.shape[0]
    lane = _lane_iota()

    @pl.when(i == 0)
    def _init():
        carry_ref[...] = jnp.zeros_like(carry_ref)

    mix = jnp.dot(of_ref[...], wf_ref[...], preferred_element_type=F32)
    mix = mix + jnp.dot(os_ref[...], ws_ref[...], preferred_element_type=F32)
    mix = mix + jnp.dot(on_ref[...], wn_ref[...], preferred_element_type=F32)
    x1 = _layer_norm(alpha * x_ref[...] + mix, g_ref[...], b_ref[...])
    x1_ref[...] = x1

    logits = jnp.dot(x1.astype(BF16), rw_ref[...], preferred_element_type=F32) + rb_ref[...]
    v = jnp.where(lane < N_EXPERTS, logits, NEG)
    lane_f = lane.astype(F32)
    tops, hots = [], []
    for _ in range(TOP_K):
        mx = jnp.max(v, axis=-1, keepdims=True)
        idx = jnp.min(jnp.where(v == mx, lane_f, float(LANES)), axis=-1, keepdims=True)
        hot = lane_f == idx
        tops.append((mx, idx))
        hots.append(hot)
        v = jnp.where(hot, 2.0 * NEG, v)
    es = [jnp.exp(mx - tops[0][0]) for mx, _ in tops]
    den = es[0] + es[1] + es[2] + es[3]

    member = jnp.zeros((tm, LANES), F32)
    for hot in hots:
        member = member + jnp.where(hot, 1.0, 0.0)
    r = lax.broadcasted_iota(I32, (tm, tm), 0)
    c = lax.broadcasted_iota(I32, (tm, tm), 1)
    strict = jnp.where(c < r, 1.0, 0.0).astype(BF16)
    before = jnp.dot(strict, member.astype(BF16), preferred_element_type=F32) + carry_ref[0:1, :]
    ri = jnp.zeros((tm, LANES), F32)
    rwt = jnp.zeros((tm, LANES), F32)
    for k in range(TOP_K):
        rank = jnp.sum(jnp.where(hots[k], before, 0.0), axis=-1, keepdims=True)
        ri = jnp.where(lane == k, tops[k][1], ri)
        ri = jnp.where(lane == TOP_K + k, rank, ri)
        rwt = jnp.where(lane == k, es[k] / den, rwt)
    ri_ref[...] = ri.astype(I32)
    rwt_ref[...] = rwt
    new = carry_ref[0:1, :] + jnp.sum(member, axis=0, keepdims=True)
    carry_ref[...] = jnp.zeros_like(carry_ref) + new
    cnt_ref[...] = jnp.zeros_like(cnt_ref) + new


def _mix(x2, of, osw, on, wf, ws, wn, g, b, rw, rb, alpha, tm):
    n, d = x2.shape
    full = lambda a: pl.BlockSpec(a.shape, lambda i: (0,) * a.ndim)
    rows = lambda a: pl.BlockSpec((tm, a.shape[1]), lambda i: (i, 0))
    return pl.pallas_call(
        functools.partial(_mix_kernel, alpha=alpha),
        grid=(n // tm,),
        in_specs=[rows(x2), rows(of), rows(osw), rows(on), full(wf), full(ws), full(wn),
                  full(g), full(b), full(rw), full(rb)],
        out_specs=[pl.BlockSpec((tm, d), lambda i: (i, 0)),
                   pl.BlockSpec((tm, LANES), lambda i: (i, 0)),
                   pl.BlockSpec((tm, LANES), lambda i: (i, 0)),
                   pl.BlockSpec((8, LANES), lambda i: (0, 0))],
        out_shape=[jax.ShapeDtypeStruct((n, d), F32), jax.ShapeDtypeStruct((n, LANES), I32),
                   jax.ShapeDtypeStruct((n, LANES), F32), jax.ShapeDtypeStruct((8, LANES), F32)],
        scratch_shapes=[pltpu.VMEM((8, LANES), F32)],
        compiler_params=_params("arbitrary"),
        name="mix_ln_router",
    )(x2, of, osw, on, wf, ws, wn, g, b, rw, rb)


def _dispatch_kernel(dest_ref, x_ref, zeros_hbm, xs_hbm, sem):
    del zeros_hbm
    tokens = x_ref.shape[0]

    for t in range(tokens):
        for k in range(TOP_K):
            pltpu.make_async_copy(x_ref.at[pl.ds(t, 1)],
                                  xs_hbm.at[pl.ds(dest_ref[t * TOP_K + k], 1)], sem).start()
    for k in range(TOP_K):
        pltpu.make_async_copy(x_ref, xs_hbm.at[pl.ds(0, tokens)], sem).wait()


def _dispatch(dest_flat, x1, rows_total, tokens):
    n, d = x1.shape
    zeros = jnp.zeros((rows_total, d), F32)
    return pl.pallas_call(
        _dispatch_kernel,
        grid=(n // tokens,),
        in_specs=[pl.BlockSpec((tokens * TOP_K,), lambda i: (i,), memory_space=pltpu.SMEM),
                  pl.BlockSpec((tokens, d), lambda i: (i, 0)),
                  pl.BlockSpec(memory_space=pl.ANY)],
        out_specs=pl.BlockSpec(memory_space=pl.ANY),
        out_shape=jax.ShapeDtypeStruct((rows_total, d), F32),
        scratch_shapes=[pltpu.SemaphoreType.DMA(())],
        input_output_aliases={2: 0},
        compiler_params=_params("arbitrary"),
        name="moe_dispatch",
    )(dest_flat, x1, zeros)


def _ffn_kernel(be_ref, nu_ref, x_ref, wg_ref, wu_ref, bg_ref, bu_ref, wd_ref, bd_ref, y_ref):
    i = pl.program_id(0)

    @pl.when(i < nu_ref[0])
    def _run():
        x = x_ref[...].astype(BF16)
        gate = jnp.dot(x, wg_ref[0], preferred_element_type=F32) + bg_ref[0]
        up = jnp.dot(x, wu_ref[0], preferred_element_type=F32) + bu_ref[0]
        gate = jnp.minimum(gate, SWIGLU_LIMIT)
        up = jnp.clip(up, -SWIGLU_LIMIT, SWIGLU_LIMIT)
        glu = gate * jax.nn.sigmoid(gate * SWIGLU_ALPHA)
        act = ((up + 1.0) * glu).astype(BF16)
        y_ref[...] = jnp.dot(act, wd_ref[0], preferred_element_type=F32) + bd_ref[0]

    @pl.when(i >= nu_ref[0])
    def _skip():
        y_ref[...] = jnp.zeros_like(y_ref)


def _ffn(blk_e, n_used, xs, wg, wu, bg, bu, wd, bd, rows):
    p, d = xs.shape
    wspec = lambda a: pl.BlockSpec((1,) + a.shape[1:], lambda i, be, nu: (be[i], 0, 0))
    grid_spec = pltpu.PrefetchScalarGridSpec(
        num_scalar_prefetch=2,
        grid=(p // rows,),
        in_specs=[pl.BlockSpec((rows, d), lambda i, be, nu: (i, 0)),
                  wspec(wg), wspec(wu), wspec(bg), wspec(bu), wspec(wd), wspec(bd)],
        out_specs=pl.BlockSpec((rows, d), lambda i, be, nu: (i, 0)),
    )
    return pl.pallas_call(
        _ffn_kernel,
        grid_spec=grid_spec,
        out_shape=jax.ShapeDtypeStruct((p, d), F32),
        compiler_params=_params("arbitrary"),
        name="moe_ffn",
    )(blk_e, n_used, xs, wg, wu, bg, bu, wd, bd)


def _combine_kernel(destc_ref, destn_ref, x1_ref, w_ref, g_ref, b_ref, y_hbm, o_ref, buf0, buf1, sem, *, alpha):
    i = pl.program_id(0)
    last = pl.num_programs(0) - 1
    tokens = x1_ref.shape[0]
    bufs = (buf0, buf1)

    def issue(dest_ref, dst):
        for t in range(tokens):
            for k in range(TOP_K):
                pltpu.make_async_copy(y_hbm.at[pl.ds(dest_ref[t * TOP_K + k], 1)],
                                      bufs[dst].at[k, pl.ds(t, 1)], sem.at[dst]).start()

    @pl.when(i == 0)
    def _prologue():
        issue(destc_ref, 0)

    def run(slot):
        @pl.when(i < last)
        def _():
            issue(destn_ref, 1 - slot)

        for k in range(TOP_K):
            pltpu.make_async_copy(y_hbm.at[pl.ds(0, tokens)], bufs[slot].at[k], sem.at[slot]).wait()
        w = w_ref[...]
        ffn = jnp.zeros(x1_ref.shape, F32)
        for k in range(TOP_K):
            ffn = ffn + bufs[slot][k] * w[:, k:k + 1]
        o_ref[...] = _layer_norm(alpha * x1_ref[...] + ffn, g_ref[...], b_ref[...])

    for slot in range(2):
        pl.when(i % 2 == slot)(functools.partial(run, slot))


def _combine(dest_flat, x1, rwt, g, b, y, alpha, tokens):
    n, d = x1.shape
    steps = n // tokens
    full = lambda a: pl.BlockSpec(a.shape, lambda i: (0,) * a.ndim)
    return pl.pallas_call(
        functools.partial(_combine_kernel, alpha=alpha),
        grid=(steps,),
        in_specs=[pl.BlockSpec((tokens * TOP_K,), lambda i: (i,), memory_space=pltpu.SMEM),
                  pl.BlockSpec((tokens * TOP_K,), lambda i: (jnp.minimum(i + 1, steps - 1),),
                               memory_space=pltpu.SMEM),
                  pl.BlockSpec((tokens, d), lambda i: (i, 0)),
                  pl.BlockSpec((tokens, LANES), lambda i: (i, 0)),
                  full(g), full(b),
                  pl.BlockSpec(memory_space=pl.ANY)],
        out_specs=pl.BlockSpec((tokens, d), lambda i: (i, 0)),
        out_shape=jax.ShapeDtypeStruct((n, d), F32),
        scratch_shapes=[pltpu.VMEM((TOP_K, tokens, d), F32), pltpu.VMEM((TOP_K, tokens, d), F32),
                        pltpu.SemaphoreType.DMA((2,))],
        compiler_params=_params("arbitrary"),
        name="moe_combine_ln",
    )(dest_flat, dest_flat, x1, rwt, g, b, y)


def _prep_gate_up_kernel(w_ref, p_ref, g_ref, u_ref):
    groups = w_ref.shape[2] // (2 * LANES)
    for c in range(groups):
        blk = w_ref[0, :, c * 2 * LANES:(c + 1) * 2 * LANES].astype(BF16)
        o = jnp.dot(blk, p_ref[...], preferred_element_type=F32)
        g_ref[0, :, c * LANES:(c + 1) * LANES] = o[:, :LANES].astype(BF16)
        u_ref[0, :, c * LANES:(c + 1) * LANES] = o[:, LANES:].astype(BF16)


def _prep_gate_up(w_gu):
    e, d, f2 = w_gu.shape
    perm = np.zeros((2 * LANES, 2 * LANES), np.float32)
    perm[2 * np.arange(LANES), np.arange(LANES)] = 1.0
    perm[2 * np.arange(LANES) + 1, LANES + np.arange(LANES)] = 1.0
    return pl.pallas_call(
        _prep_gate_up_kernel,
        grid=(e,),
        in_specs=[pl.BlockSpec((1, d, f2), lambda i: (i, 0, 0)),
                  pl.BlockSpec((2 * LANES, 2 * LANES), lambda i: (0, 0))],
        out_specs=[pl.BlockSpec((1, d, f2 // 2), lambda i: (i, 0, 0)),
                   pl.BlockSpec((1, d, f2 // 2), lambda i: (i, 0, 0))],
        out_shape=[jax.ShapeDtypeStruct((e, d, f2 // 2), BF16), jax.ShapeDtypeStruct((e, d, f2 // 2), BF16)],
        compiler_params=_params("arbitrary"),
        name="prep_gate_up",
    )(w_gu, jnp.asarray(perm, BF16))


def _gqa_cols(base):
    idx = np.empty(GQA_GROUP * KV_HEADS * HEAD_DIM, np.int64)
    for g in range(GQA_GROUP):
        for kv in range(KV_HEADS):
            dst = (g * KV_HEADS + kv) * HEAD_DIM
            src = (kv * GQA_GROUP + g) * HEAD_DIM
            idx[dst:dst + HEAD_DIM] = base + src + np.arange(HEAD_DIM)
    return idx


def _inproj_layout():
    fw = FOX_HEADS * HEAD_DIM
    qw = SWA_Q_HEADS * HEAD_DIM
    kw = KV_HEADS * HEAD_DIM
    sizes = (fw, fw, fw, FOX_HEADS, qw, kw, kw, qw, kw, kw, kw, kw, kw, kw, 3 * NSA_Q_HEADS)
    off = np.concatenate([[0], np.cumsum(sizes)])
    (o_fq, o_fk, o_fv, o_ff, o_sq, o_sk, o_sv, o_nq, o_nkc, o_nvc, o_nks, o_nvs, o_nkw, o_nvw, o_ng, total) = off
    ar = np.arange
    cols = np.concatenate([
        _gqa_cols(o_sq), _gqa_cols(o_nq), o_fq + ar(fw), o_fk + ar(fw), o_fv + ar(fw),
        o_sk + ar(kw), o_sv + ar(kw), o_nks + ar(kw), o_nvs + ar(kw), o_nkw + ar(kw), o_nvw + ar(kw),
        o_nkc + ar(kw), o_nvc + ar(kw)])
    gate = np.full(LANES, total, np.int64)
    gate[:FOX_HEADS] = o_ff + ar(FOX_HEADS)
    for r in range(3):
        for g in range(GQA_GROUP):
            for kv in range(KV_HEADS):
                gate[FOX_HEADS + (r * GQA_GROUP + g) * KV_HEADS + kv] = o_ng + (kv * GQA_GROUP + g) * 3 + r
    cols = np.concatenate([cols, gate])
    scale = np.ones(cols.shape[0], np.float32)
    scale[C_SQ:C_SQ + qw] = HEAD_DIM ** -0.5
    scale[C_NQ:C_NQ + qw] = HEAD_DIM ** -0.5
    scale[C_FQ:C_FQ + fw] = HEAD_DIM ** -0.5
    return cols, scale


def _outproj_rows():
    fw = FOX_HEADS * HEAD_DIM
    qw = SWA_Q_HEADS * HEAD_DIM
    return np.arange(fw), _gqa_cols(fw), _gqa_cols(fw + qw)


def _overlap_matrix_t(s_len, nc):
    ns = s_len // SEL_BLOCK
    cmp_start = np.arange(nc) * CMP_STRIDE
    sel_start = np.arange(ns) * SEL_BLOCK
    ov = np.clip(np.minimum(cmp_start[:, None] + CMP_BLOCK, sel_start[None, :] + SEL_BLOCK)
                 - np.maximum(cmp_start[:, None], sel_start[None, :]), 0, None) / CMP_BLOCK
    ov[(s_len - CMP_BLOCK) // CMP_STRIDE + 1:] = 0.0
    out = np.zeros((KV_HEADS, LANES, nc), np.float32)
    for kv in range(KV_HEADS):
        r0 = _spare(kv) + L_BLK
        out[kv, r0:r0 + ns, :] = ov.T
    return out


def _compress_weights(pe, w1, w2):
    pe2 = jnp.concatenate([pe, pe], axis=-1)
    w1r = w1.reshape(2, CMP_BLOCK, HEAD_DIM, HEAD_DIM)
    z = jnp.zeros_like(w1r)
    w1b = jnp.concatenate([jnp.concatenate([w1r, z], axis=-1), jnp.concatenate([z, w1r], axis=-1)], axis=-2)
    z2 = jnp.zeros_like(w2)
    w2b = jnp.concatenate([jnp.concatenate([w2, z2], axis=-1), jnp.concatenate([z2, w2], axis=-1)], axis=-2)
    return pe2, w1b.astype(BF16), w2b.astype(BF16)


def kernel(x, w_in, b_in, sinks, cmp_pe, cmp_w1, cmp_w2, w_out, ln1_g, ln1_b, router_w, router_b,
           w_gate_up, b_gate_up, w_down, b_down, ln2_g, ln2_b):
    depth = w_in.shape[0]
    bsz, s_len, d = x.shape
    n = bsz * s_len
    nk = n * TOP_K
    alpha = float((2.0 * depth) ** 0.25)
    assert s_len % SEL_KEYS == 0 and s_len // SEL_BLOCK <= MAX_SEL_BLOCKS and n % 512 == 0
    slopes_swa, slopes_nsa = _alibi()
    cols, scale = _inproj_layout()
    rows_f, rows_s, rows_n = _outproj_rows()
    nc = s_len // CMP_STRIDE
    ovt = jnp.asarray(_overlap_matrix_t(s_len, nc), BF16)
    nblk = -(-nk // MOE_ROWS) + N_EXPERTS
    p_rows = nblk * MOE_ROWS
    f = w_down.shape[2]

    x2 = x.reshape(n, d)
    for l in range(depth):
        w_ext = jnp.concatenate([w_in[l], jnp.zeros((d, 1), F32)], axis=1)
        b_ext = jnp.concatenate([b_in[l], jnp.zeros((1,), F32)])
        w_l = (jnp.take(w_ext, cols, axis=1) * scale).astype(BF16)
        b_l = (jnp.take(b_ext, cols) * scale).reshape(1, -1)
        hb, hf = _inproj(x2, w_l, b_l, 512)
        hb3 = hb.reshape(bsz, s_len, CB)
        hf3 = hf.reshape(bsz, s_len, CF)
        cg = _gates(hf3)
        o_fox = _fox(hb3, cg)
        o_swa = _swa(hb3, sinks[l], slopes_swa)
        grp = hf3[:, :, :2 * LANES].reshape(bsz, nc, CMP_STRIDE * 2 * LANES)
        pe2, w1b, w2b = _compress_weights(cmp_pe[l], cmp_w1[l], cmp_w2[l])
        kc, vc = _compress(grp, pe2, w1b, w2b)
        o_nsa = _nsa(hb3, kc, vc, cg, ovt, slopes_nsa)

        wo = w_out[l].astype(BF16)
        rw = jnp.concatenate([router_w[l], jnp.zeros((d, LANES - N_EXPERTS), F32)], axis=1).astype(BF16)
        rb = jnp.concatenate([router_b[l], jnp.zeros((LANES - N_EXPERTS,), F32)]).reshape(1, LANES)
        x1, ri, rwt, cnt = _mix(
            x2, o_fox.reshape(n, -1), o_swa.reshape(n, -1), o_nsa.reshape(n, -1),
            jnp.take(wo, rows_f, axis=0), jnp.take(wo, rows_s, axis=0), jnp.take(wo, rows_n, axis=0),
            ln1_g[l].reshape(1, d), ln1_b[l].reshape(1, d), rw, rb, alpha, 512)

        counts = cnt[0, :N_EXPERTS].astype(I32)
        padded = (counts + MOE_ROWS - 1) // MOE_ROWS * MOE_ROWS
        pad_end = jnp.cumsum(padded)
        pad_start = pad_end - padded
        dest = (jnp.take(pad_start, ri[:, :TOP_K]) + ri[:, TOP_K:2 * TOP_K]).reshape(nk)
        blk_start = jnp.arange(nblk, dtype=I32) * MOE_ROWS
        blk_e = jnp.minimum(jnp.sum((pad_end[None, :] <= blk_start[:, None]).astype(I32), axis=1),
                            N_EXPERTS - 1)
        n_used = (pad_end[-1:] // MOE_ROWS).astype(I32)

        xs = _dispatch(dest, x1, p_rows, 256)
        w_gate, w_up = _prep_gate_up(w_gate_up[l])
        bgu = b_gate_up[l].reshape(N_EXPERTS, 1, f, 2)
        y = _ffn(blk_e, n_used, xs, w_gate, w_up,
                 bgu[..., 0], bgu[..., 1], w_down[l].astype(BF16), b_down[l].reshape(N_EXPERTS, 1, d),
                 MOE_ROWS)
        x2 = _combine(dest, x1, rwt, ln2_g[l].reshape(1, d), ln2_b[l].reshape(1, d), y, alpha, 128)
    return x2.reshape(bsz, s_len, d)
```

```python
import functools

import numpy as np
import jax
import jax.numpy as jnp
from jax import lax
from jax.experimental import pallas as pl
from jax.experimental.pallas import tpu as pltpu

F32 = jnp.float32
BF16 = jnp.bfloat16
I32 = jnp.int32

HEAD_DIM = 64
LANES = 128
N_HEADS = 16
SWA_Q_HEADS = 6
NSA_Q_HEADS = 6
FOX_HEADS = 4
KV_HEADS = 2
GQA_GROUP = 3
SWA_WINDOW = 128
CMP_BLOCK = 32
CMP_STRIDE = 16
SEL_BLOCK = 64
SEL_TOP_N = 8
NSA_WINDOW = 512
SEL_FORCE = 1.0e4
N_EXPERTS = 32
TOP_K = 4
SWIGLU_LIMIT = 7.0
SWIGLU_ALPHA = 1.702
LN_EPS = 1e-5
NEG = -1.0e30

Q_TILE = 256
FOX_TILE = 256
FOX_KEYS = 512
SEL_KEYS = 512
MAX_SEL_BLOCKS = 32
MOE_ROWS = 512
VMEM_LIMIT = 56 * 1024 * 1024

L_BLK = 0
L_QT = 32
L_KH = 35
L_KL = 38
L_PAD = 41

C_SQ, C_NQ, C_FQ, C_FK, C_FV = 0, 384, 768, 1024, 1280
C_SK, C_SV, C_NKS, C_NVS, C_NKW, C_NVW = 1536, 1664, 1792, 1920, 2048, 2176
CB = 2304
CF = 384


def _alibi():
    n = SWA_Q_HEADS + NSA_Q_HEADS
    s = (2.0 ** (-8.0 * np.arange(1, n + 1) / n)).astype(np.float32)
    return s[0::2].reshape(KV_HEADS, GQA_GROUP), s[1::2].reshape(KV_HEADS, GQA_GROUP)


def _bf16_parts(v):
    v = np.float32(v)
    out = []
    for _ in range(3):
        p = np.float32(np.asarray(v, dtype=BF16))
        out.append(float(p))
        v = np.float32(v - p)
    return out


def _split3(x):
    hi = x.astype(BF16).astype(F32)
    r = x - hi
    mid = r.astype(BF16).astype(F32)
    lo = (r - mid).astype(BF16).astype(F32)
    return hi, mid, lo


def _lane_iota():
    return lax.broadcasted_iota(I32, (1, LANES), 1)


def _row_iota():
    return lax.broadcasted_iota(I32, (LANES, 1), 0)


def _spare(kv_half):
    return HEAD_DIM if kv_half == 0 else 0


def _dot_nt(a, b):
    return lax.dot_general(a, b, (((1,), (1,)), ((), ())), preferred_element_type=F32)


def _transposed(x):
    return x.astype(F32).T.astype(BF16)


def _params(*sem):
    return pltpu.CompilerParams(dimension_semantics=sem, vmem_limit_bytes=VMEM_LIMIT)


def _inproj_kernel(x_ref, w_ref, b_ref, hb_ref, hf_ref):
    acc = jnp.dot(x_ref[...].astype(BF16), w_ref[...], preferred_element_type=F32) + b_ref[...]
    hb_ref[...] = acc[:, :CB].astype(BF16)
    hf_ref[...] = acc[:, CB:]


def _inproj(x2, w, b, tm):
    n, d = x2.shape
    ct = w.shape[1]
    return pl.pallas_call(
        _inproj_kernel,
        grid=(n // tm,),
        in_specs=[pl.BlockSpec((tm, d), lambda i: (i, 0)),
                  pl.BlockSpec((d, ct), lambda i: (0, 0)),
                  pl.BlockSpec((1, ct), lambda i: (0, 0))],
        out_specs=[pl.BlockSpec((tm, CB), lambda i: (i, 0)),
                   pl.BlockSpec((tm, CF), lambda i: (i, 0))],
        out_shape=[jax.ShapeDtypeStruct((n, CB), BF16), jax.ShapeDtypeStruct((n, CF), F32)],
        compiler_params=_params("arbitrary"),
        name="inproj",
    )(x2, w, b)


def _gates_kernel(g_ref, o_ref, *, chunk):
    s = g_ref.shape[1]
    lane = _lane_iota()
    r = lax.broadcasted_iota(I32, (chunk, chunk), 0)
    c = lax.broadcasted_iota(I32, (chunk, chunk), 1)
    tri = jnp.where(c <= r, 1.0, 0.0).astype(BF16)

    def body(j, carry):
        x = g_ref[0, pl.ds(pl.multiple_of(j * chunk, chunk), chunk), :]
        ls = jnp.minimum(x, 0.0) - jnp.log1p(jnp.exp(-jnp.abs(x)))
        cs = carry
        for part in _split3(ls):
            cs = cs + jnp.dot(tri, part.astype(BF16), preferred_element_type=F32)
        sg = jax.nn.sigmoid(x)
        o_ref[0, pl.ds(pl.multiple_of(j * chunk, chunk), chunk), :] = jnp.where(lane < FOX_HEADS, cs, sg)
        return cs[chunk - 1:chunk, :]

    lax.fori_loop(0, s // chunk, body, jnp.zeros((1, LANES), F32))


def _gates(hf3):
    b, s, _ = hf3.shape
    chunk = min(256, s)
    return pl.pallas_call(
        functools.partial(_gates_kernel, chunk=chunk),
        grid=(b,),
        in_specs=[pl.BlockSpec((1, s, LANES), lambda i: (i, 0, 2))],
        out_specs=pl.BlockSpec((1, s, LANES), lambda i: (i, 0, 0)),
        out_shape=jax.ShapeDtypeStruct((b, s, LANES), F32),
        compiler_params=_params("arbitrary"),
        name="gates",
    )(hf3)


def _softmax_steps(carries, scores, values):
    stats = []
    for (m, l, _), sc in zip(carries, scores):
        m_new = jnp.maximum(m, jnp.max(sc, axis=0, keepdims=True))
        alpha = jnp.exp(m - m_new)
        pr = jnp.exp(sc - m_new)
        stats.append((m_new, alpha * l + jnp.sum(pr, axis=0, keepdims=True), alpha, pr.astype(BF16)))
    return tuple((m, l, alpha * acc + jnp.dot(vt, pr, preferred_element_type=F32))
                 for (m, l, alpha, pr), (_, _, acc), vt in zip(stats, carries, values))


def _softmax_init(queries):
    return (jnp.full((1, queries), NEG, F32), jnp.zeros((1, queries), F32), jnp.zeros((LANES, queries), F32))


def _fox_kernel(q_ref, k_ref, v_ref, cq_ref, ck_ref, o_ref, ka_ref, vt_ref, *, tile, keys):
    i = pl.program_id(1)
    s_len = k_ref.shape[1]
    lane = _lane_iota()
    npair = FOX_HEADS // 2

    @pl.when(i == 0)
    def _fill():
        for h in range(FOX_HEADS):
            p, half = divmod(h, 2)
            a0 = _spare(half)
            k = k_ref[0, :, p * LANES:(p + 1) * LANES].astype(F32)
            hi, mid, lo = _split3(-ck_ref[0, :, h:h + 1])
            ka = jnp.where((lane >= a0 + L_QT) & (lane < a0 + L_QT + 3), 1.0, k)
            ka = jnp.where(lane == a0 + L_KH, hi, ka)
            ka = jnp.where(lane == a0 + L_KH + 1, mid, ka)
            ka = jnp.where(lane == a0 + L_KH + 2, lo, ka)
            ka_ref[h] = ka.astype(BF16)
        for p in range(npair):
            for c in range(s_len // keys):
                vt_ref[p, c] = _transposed(v_ref[0, c * keys:(c + 1) * keys, p * LANES:(p + 1) * LANES])

    qas = []
    for h in range(FOX_HEADS):
        p, half = divmod(h, 2)
        a0 = _spare(half)
        q = q_ref[0, :, p * LANES:(p + 1) * LANES].astype(F32)
        in_half = (lane < HEAD_DIM) if half == 0 else (lane >= HEAD_DIM)
        hi, mid, lo = _split3(cq_ref[0, :, h:h + 1])
        qa = jnp.where(in_half, q, 0.0)
        qa = jnp.where(lane == a0 + L_QT, hi, qa)
        qa = jnp.where(lane == a0 + L_QT + 1, mid, qa)
        qa = jnp.where(lane == a0 + L_QT + 2, lo, qa)
        qas.append(jnp.where((lane >= a0 + L_KH) & (lane < a0 + L_KH + 3), 1.0, qa).astype(BF16))

    nfull = (i * tile) // keys
    key = lax.broadcasted_iota(I32, (keys, tile), 0) + nfull * keys
    qry = lax.broadcasted_iota(I32, (keys, tile), 1) + i * tile
    causal = jnp.where(key <= qry, 0.0, NEG)

    def step(j, carry, masked):
        start = pl.multiple_of(j * keys, keys)
        scores = [_dot_nt(ka_ref[h, pl.ds(start, keys), :], qas[h]) for h in range(FOX_HEADS)]
        if masked:
            scores = [sc + causal for sc in scores]
        return _softmax_steps(carry, scores, [vt_ref[h // 2, j] for h in range(FOX_HEADS)])

    init = tuple(_softmax_init(tile) for _ in range(FOX_HEADS))
    carry = lax.fori_loop(0, nfull, functools.partial(step, masked=False), init)
    carry = step(nfull, carry, True)
    row = _row_iota()
    for p in range(npair):
        (_, l0, a0_), (_, l1, a1_) = carry[2 * p], carry[2 * p + 1]
        o_ref[0, :, p * LANES:(p + 1) * LANES] = jnp.where(row < HEAD_DIM, a0_ / l0, a1_ / l1).T.astype(BF16)


def _fox(hb3, cg):
    b, s, _ = hb3.shape
    tile = min(FOX_TILE, s)
    keys = min(FOX_KEYS, s)
    assert keys % tile == 0 and s % keys == 0
    w = FOX_HEADS * HEAD_DIM
    return pl.pallas_call(
        functools.partial(_fox_kernel, tile=tile, keys=keys),
        grid=(b, s // tile),
        in_specs=[pl.BlockSpec((1, tile, w), lambda bi, i: (bi, i, C_FQ // w)),
                  pl.BlockSpec((1, s, w), lambda bi, i: (bi, 0, C_FK // w)),
                  pl.BlockSpec((1, s, w), lambda bi, i: (bi, 0, C_FV // w)),
                  pl.BlockSpec((1, tile, LANES), lambda bi, i: (bi, i, 0)),
                  pl.BlockSpec((1, s, LANES), lambda bi, i: (bi, 0, 0))],
        out_specs=pl.BlockSpec((1, tile, w), lambda bi, i: (bi, i, 0)),
        out_shape=jax.ShapeDtypeStruct((b, s, w), BF16),
        scratch_shapes=[pltpu.VMEM((FOX_HEADS, s, LANES), BF16),
                        pltpu.VMEM((FOX_HEADS // 2, s // keys, LANES, keys), BF16)],
        compiler_params=_params("arbitrary", "arbitrary"),
        name="fox",
    )(hb3, hb3, hb3, cg, cg)


def _fill_banded(k_ref, v_ref, kp_ref, vt_ref, window):
    s_len = k_ref.shape[1]
    lane = _lane_iota()
    k = k_ref[0].astype(F32)
    pos = lax.broadcasted_iota(I32, (s_len, 1), 0) + window
    pos_hi = ((pos >> 8) << 8).astype(F32)
    pos_lo = (pos & 255).astype(F32)
    for kv in range(KV_HEADS):
        a0 = _spare(kv)
        ka = jnp.where((lane >= a0 + L_QT) & (lane < a0 + L_QT + 3), 1.0, k)
        ka = jnp.where((lane >= a0 + L_KH) & (lane < a0 + L_KH + 3), pos_hi, ka)
        ka = jnp.where((lane >= a0 + L_KL) & (lane < a0 + L_KL + 3), pos_lo, ka)
        ka = jnp.where(lane == a0 + L_PAD, 0.0, ka)
        kp_ref[kv, pl.ds(window, s_len), :] = ka.astype(BF16)
        pad = jnp.where(lane == a0 + L_PAD, NEG, 0.0) + jnp.zeros((window, LANES), F32)
        kp_ref[kv, pl.ds(0, window), :] = pad.astype(BF16)
    npad = window // LANES
    for c in range(npad):
        vt_ref[c] = jnp.zeros((LANES, LANES), BF16)
    for c in range(s_len // LANES):
        vt_ref[npad + c] = _transposed(v_ref[0, c * LANES:(c + 1) * LANES, :])


def _alibi_query(q, kv, g, t_pos, slopes, lane, pad_lane):
    a0 = _spare(kv)
    in_half = (lane < HEAD_DIM) if kv == 0 else (lane >= HEAD_DIM)
    sl = float(slopes[kv][g])
    hi, mid, lo = _split3(t_pos * (-sl))
    qa = jnp.where(in_half, q.astype(F32), 0.0)
    qa = jnp.where(lane == a0 + L_QT, hi, qa)
    qa = jnp.where(lane == a0 + L_QT + 1, mid, qa)
    qa = jnp.where(lane == a0 + L_QT + 2, lo, qa)
    for c, part in enumerate(_bf16_parts(sl)):
        qa = jnp.where((lane == a0 + L_KH + c) | (lane == a0 + L_KL + c), part, qa)
    if pad_lane:
        qa = jnp.where(lane == a0 + L_PAD, 1.0, qa)
    return qa


def _banded_attend(q, kp_ref, vt_ref, i, window, slopes, sink_ref):
    tq = q.shape[0]
    span = window + tq
    lane = _lane_iota()
    start = pl.multiple_of(i * tq, tq)
    t_pad = (lax.broadcasted_iota(I32, (tq, 1), 0) + i * tq + window).astype(F32)
    key = lax.broadcasted_iota(I32, (span, tq), 0)
    qry = lax.broadcasted_iota(I32, (span, tq), 1)
    band = jnp.where((key > qry) & (key <= qry + window), 0.0, NEG)
    band = jnp.concatenate([band] * GQA_GROUP, axis=1)
    outs = []
    for kv in range(KV_HEADS):
        qst = jnp.concatenate(
            [_alibi_query(q[:, g * LANES:(g + 1) * LANES], kv, g, t_pad, slopes, lane, True).astype(BF16)
             for g in range(GQA_GROUP)], axis=0)
        sc = _dot_nt(kp_ref[kv, pl.ds(start, span), :], qst) + band
        m = jnp.max(sc, axis=0, keepdims=True)
        if sink_ref is not None:
            sink = jnp.concatenate([jnp.full((1, tq), sink_ref[kv * GQA_GROUP + g], F32)
                                    for g in range(GQA_GROUP)], axis=1)
            m = jnp.maximum(m, sink)
        pr = jnp.exp(sc - m)
        den = jnp.sum(pr, axis=0, keepdims=True)
        if sink_ref is not None:
            den = den + jnp.exp(sink - m)
        pr = pr.astype(BF16)
        o = jnp.zeros((LANES, GQA_GROUP * tq), F32)
        base = i * (tq // LANES)
        for c in range(0, span // LANES, 2):
            wide = min(2, span // LANES - c)
            vt = jnp.concatenate([vt_ref[base + c + j] for j in range(wide)], axis=1)
            o = o + jnp.dot(vt, pr[c * LANES:(c + wide) * LANES, :], preferred_element_type=F32)
        outs.append(o / den)
    return outs


def _merge_kv(outs, g, tq, row):
    return jnp.where(row < HEAD_DIM, outs[0][:, g * tq:(g + 1) * tq], outs[1][:, g * tq:(g + 1) * tq])


def _swa_kernel(sink_ref, q_ref, k_ref, v_ref, o_ref, kp_ref, vt_ref, *, slopes):
    i = pl.program_id(1)
    tq = q_ref.shape[1]

    @pl.when(i == 0)
    def _fill():
        _fill_banded(k_ref, v_ref, kp_ref, vt_ref, SWA_WINDOW)

    outs = _banded_attend(q_ref[0], kp_ref, vt_ref, i, SWA_WINDOW, slopes, sink_ref)
    row = _row_iota()
    for g in range(GQA_GROUP):
        o_ref[0, :, g * LANES:(g + 1) * LANES] = _merge_kv(outs, g, tq, row).T.astype(BF16)


def _swa(hb3, sinks, slopes):
    b, s, _ = hb3.shape
    w = GQA_GROUP * LANES
    return pl.pallas_call(
        functools.partial(_swa_kernel, slopes=slopes),
        grid=(b, s // Q_TILE),
        in_specs=[pl.BlockSpec(memory_space=pltpu.SMEM),
                  pl.BlockSpec((1, Q_TILE, w), lambda bi, i: (bi, i, C_SQ // w)),
                  pl.BlockSpec((1, s, LANES), lambda bi, i: (bi, 0, C_SK // LANES)),
                  pl.BlockSpec((1, s, LANES), lambda bi, i: (bi, 0, C_SV // LANES))],
        out_specs=pl.BlockSpec((1, Q_TILE, w), lambda bi, i: (bi, i, 0)),
        out_shape=jax.ShapeDtypeStruct((b, s, w), BF16),
        scratch_shapes=[pltpu.VMEM((KV_HEADS, s + SWA_WINDOW, LANES), BF16),
                        pltpu.VMEM(((s + SWA_WINDOW) // LANES, LANES, LANES), BF16)],
        compiler_params=_params("arbitrary", "arbitrary"),
        name="swa",
    )(sinks, hb3, hb3, hb3)


def _compress_kernel(g_ref, pe_ref, w1_ref, w2_ref, kc_ref, vc_ref):
    nc = g_ref.shape[1]
    for which, out_ref in ((0, kc_ref), (1, vc_ref)):
        ha = jnp.zeros((nc, LANES), F32)
        hb = jnp.zeros((nc, LANES), F32)
        for l in range(CMP_STRIDE):
            x = g_ref[0, :, (2 * l + which) * LANES:(2 * l + which + 1) * LANES]
            xa = (x + pe_ref[which, l:l + 1, :]).astype(BF16)
            xb = (x + pe_ref[which, CMP_STRIDE + l:CMP_STRIDE + l + 1, :]).astype(BF16)
            ha = ha + jnp.dot(xa, w1_ref[which, l], preferred_element_type=F32)
            hb = hb + jnp.dot(xb, w1_ref[which, CMP_STRIDE + l], preferred_element_type=F32)
        hid = ha + pltpu.roll(hb, nc - 1, 0)
        act = 0.5 * hid * (1.0 + jnp.tanh(np.sqrt(2.0 / np.pi).astype(np.float32) * (hid + 0.044715 * (hid * hid * hid))))
        out_ref[0] = jnp.dot(act.astype(BF16), w2_ref[which], preferred_element_type=F32).astype(BF16)


def _compress(g, pe2, w1b, w2b):
    b, nc, gw = g.shape
    return pl.pallas_call(
        _compress_kernel,
        grid=(b,),
        in_specs=[pl.BlockSpec((1, nc, gw), lambda i: (i, 0, 0)),
                  pl.BlockSpec(pe2.shape, lambda i: (0, 0, 0)),
                  pl.BlockSpec(w1b.shape, lambda i: (0, 0, 0, 0)),
                  pl.BlockSpec(w2b.shape, lambda i: (0, 0, 0))],
        out_specs=[pl.BlockSpec((1, nc, LANES), lambda i: (i, 0, 0)),
                   pl.BlockSpec((1, nc, LANES), lambda i: (i, 0, 0))],
        out_shape=[jax.ShapeDtypeStruct((b, nc, LANES), BF16), jax.ShapeDtypeStruct((b, nc, LANES), BF16)],
        compiler_params=_params("arbitrary"),
        name="compress",
    )(g, pe2, w1b, w2b)


def _nsa_kernel(q_ref, ks_ref, vs_ref, kw_ref, vw_ref, kc_ref, vc_ref, cg_ref, ovt_ref, o_ref,
                ksa_ref, vst_ref, kwp_ref, vwt_ref, kca_ref, vct_ref, *, slopes, n_sel):
    i = pl.program_id(1)
    tq = q_ref.shape[1]
    s_len = ks_ref.shape[1]
    nc = kc_ref.shape[1]
    ns = s_len // SEL_BLOCK
    lane = _lane_iota()
    row = _row_iota()

    @pl.when(i == 0)
    def _fill():
        _fill_banded(kw_ref, vw_ref, kwp_ref, vwt_ref, NSA_WINDOW)
        ks = ks_ref[0].astype(F32)
        pos = lax.broadcasted_iota(I32, (s_len, 1), 0)
        pos_hi = ((pos >> 8) << 8).astype(F32)
        pos_lo = (pos & 255).astype(F32)
        blk = pos // SEL_BLOCK
        kc = kc_ref[0].astype(F32)
        cend = lax.broadcasted_iota(I32, (nc, 1), 0) * CMP_STRIDE + (CMP_BLOCK - 1)
        cend_hi = ((cend >> 8) << 8).astype(F32)
        cend_lo = (cend & 255).astype(F32)
        for kv in range(KV_HEADS):
            a0 = _spare(kv)
            ka = jnp.where((lane >= a0 + L_BLK) & (lane < a0 + L_BLK + MAX_SEL_BLOCKS),
                           jnp.where(lane - (a0 + L_BLK) == blk, 1.0, 0.0), ks)
            ka = jnp.where((lane >= a0 + L_QT) & (lane < a0 + L_QT + 3), 1.0, ka)
            ka = jnp.where((lane >= a0 + L_KH) & (lane < a0 + L_KH + 3), pos_hi, ka)
            ka = jnp.where((lane >= a0 + L_KL) & (lane < a0 + L_KL + 3), pos_lo, ka)
            ksa_ref[kv] = ka.astype(BF16)
            ca = jnp.where((lane >= a0 + L_QT) & (lane < a0 + L_QT + 3), 1.0, kc)
            ca = jnp.where((lane >= a0 + L_KH) & (lane < a0 + L_KH + 3), cend_hi, ca)
            ca = jnp.where((lane >= a0 + L_KL) & (lane < a0 + L_KL + 3), cend_lo, ca)
            kca_ref[kv] = ca.astype(BF16)
        for c in range(s_len // SEL_KEYS):
            vst_ref[c] = _transposed(vs_ref[0, c * SEL_KEYS:(c + 1) * SEL_KEYS, :])
        vct_ref[...] = _transposed(vc_ref[0])

    q = q_ref[0]
    t_col = (lax.broadcasted_iota(I32, (tq, 1), 0) + i * tq).astype(F32)
    t_row = lax.broadcasted_iota(I32, (1, tq), 1) + i * tq
    cur = t_row // SEL_BLOCK
    cend = lax.broadcasted_iota(I32, (nc, tq), 0) * CMP_STRIDE + (CMP_BLOCK - 1)
    cvalid = jnp.concatenate([cend <= t_row] * GQA_GROUP, axis=1)
    nfull = (i * tq) // SEL_KEYS
    dkey = lax.broadcasted_iota(I32, (SEL_KEYS, tq), 0) + nfull * SEL_KEYS
    dcausal = jnp.where(dkey <= t_row, 0.0, NEG)
    dcausal = jnp.concatenate([dcausal] * GQA_GROUP, axis=1)

    o_cmp, qsels = [], []
    for kv in range(KV_HEADS):
        a0 = _spare(kv)
        qbase = [_alibi_query(q[:, g * LANES:(g + 1) * LANES], kv, g, t_col, slopes, lane, False)
                 for g in range(GQA_GROUP)]

        qst = jnp.concatenate([x.astype(BF16) for x in qbase], axis=0)
        sc = jnp.where(cvalid, _dot_nt(kca_ref[kv], qst), NEG)
        m = jnp.max(sc, axis=0, keepdims=True)
        m = jnp.where(m > 0.5 * NEG, m, 0.0)
        e = jnp.where(cvalid, jnp.exp(sc - m), 0.0)
        den = jnp.sum(e, axis=0, keepdims=True)
        pc = (e / jnp.where(den > 0.0, den, 1.0)).astype(BF16)
        o_cmp.append(jnp.dot(vct_ref[...], pc, preferred_element_type=F32))

        p_slc = jnp.zeros((LANES, tq), F32)
        for g in range(GQA_GROUP):
            p_slc = p_slc + jnp.dot(ovt_ref[kv], pc[:, g * tq:(g + 1) * tq], preferred_element_type=F32)
        r0 = a0 + L_BLK
        p_slc = p_slc[r0:r0 + MAX_SEL_BLOCKS, :]
        jb = lax.broadcasted_iota(I32, (MAX_SEL_BLOCKS, 1), 0)
        forced = (jb == 0) | (jb == cur) | (jb == cur - 1)
        score = jnp.where(jb > cur, -1.0, jnp.where(forced, SEL_FORCE, p_slc))
        score = jnp.where(jb < ns, score, -2.0)
        rank = jnp.zeros((MAX_SEL_BLOCKS, tq), F32)
        for b2 in range(ns):
            other = score[b2:b2 + 1, :]
            beats = (other > score) | ((other == score) & (jb > b2))
            rank = rank + jnp.where(beats, 1.0, 0.0)
        selq = jnp.where(rank < float(n_sel), 0.0, NEG)
        pieces = [selq]
        if r0 > 0:
            pieces.insert(0, jnp.zeros((r0, tq), F32))
        if LANES - r0 - MAX_SEL_BLOCKS > 0:
            pieces.append(jnp.zeros((LANES - r0 - MAX_SEL_BLOCKS, tq), F32))
        selq = jnp.concatenate(pieces, axis=0).T
        in_blk = (lane >= r0) & (lane < r0 + MAX_SEL_BLOCKS)
        qsels.append(jnp.concatenate([jnp.where(in_blk, selq, x).astype(BF16) for x in qbase], axis=0))

    def step(j, carry, bias):
        start = pl.multiple_of(j * SEL_KEYS, SEL_KEYS)
        scores = [_dot_nt(ksa_ref[kv, pl.ds(start, SEL_KEYS), :], qsels[kv]) for kv in range(KV_HEADS)]
        if bias is not None:
            scores = [sc + bias for sc in scores]
        return _softmax_steps(carry, scores, [vst_ref[j]] * KV_HEADS)

    init = tuple(_softmax_init(GQA_GROUP * tq) for _ in range(KV_HEADS))
    carry = lax.fori_loop(0, nfull, lambda j, c: step(j, c, None), init)
    carry = step(nfull, carry, dcausal)
    o_sel = [acc / l for (_, l, acc) in carry]

    o_win = _banded_attend(q, kwp_ref, vwt_ref, i, NSA_WINDOW, slopes, None)

    gates = cg_ref[0].T
    for g in range(GQA_GROUP):
        total = jnp.zeros((LANES, tq), F32)
        for r, branch in enumerate((o_cmp, o_sel, o_win)):
            c0 = FOX_HEADS + (r * GQA_GROUP + g) * KV_HEADS
            gate = jnp.where(row < HEAD_DIM, gates[c0:c0 + 1, :], gates[c0 + 1:c0 + 2, :])
            total = total + gate * _merge_kv(branch, g, tq, row)
        o_ref[0, :, g * LANES:(g + 1) * LANES] = total.T.astype(BF16)


def _nsa(hb3, kc, vc, cg, ovt, slopes):
    b, s, _ = hb3.shape
    nc = kc.shape[1]
    w = GQA_GROUP * LANES
    n_sel = min(SEL_TOP_N, s // SEL_BLOCK)
    kvspec = lambda c: pl.BlockSpec((1, s, LANES), lambda bi, i: (bi, 0, c // LANES))
    return pl.pallas_call(
        functools.partial(_nsa_kernel, slopes=slopes, n_sel=n_sel),
        grid=(b, s // Q_TILE),
        in_specs=[pl.BlockSpec((1, Q_TILE, w), lambda bi, i: (bi, i, C_NQ // w)),
                  kvspec(C_NKS), kvspec(C_NVS), kvspec(C_NKW), kvspec(C_NVW),
                  pl.BlockSpec((1, nc, LANES), lambda bi, i: (bi, 0, 0)),
                  pl.BlockSpec((1, nc, LANES), lambda bi, i: (bi, 0, 0)),
                  pl.BlockSpec((1, Q_TILE, LANES), lambda bi, i: (bi, i, 0)),
                  pl.BlockSpec(ovt.shape, lambda bi, i: (0, 0, 0))],
        out_specs=pl.BlockSpec((1, Q_TILE, w), lambda bi, i: (bi, i, 0)),
        out_shape=jax.ShapeDtypeStruct((b, s, w), BF16),
        scratch_shapes=[pltpu.VMEM((KV_HEADS, s, LANES), BF16),
                        pltpu.VMEM((s // SEL_KEYS, LANES, SEL_KEYS), BF16),
                        pltpu.VMEM((KV_HEADS, s + NSA_WINDOW, LANES), BF16),
                        pltpu.VMEM(((s + NSA_WINDOW) // LANES, LANES, LANES), BF16),
                        pltpu.VMEM((KV_HEADS, nc, LANES), BF16),
                        pltpu.VMEM((LANES, nc), BF16)],
        compiler_params=_params("arbitrary", "arbitrary"),
        name="nsa",
    )(hb3, hb3, hb3, hb3, hb3, kc, vc, cg, ovt)


def _layer_norm(y, g, b):
    mu = jnp.mean(y, axis=-1, keepdims=True)
    var = jnp.mean(jnp.square(y - mu), axis=-1, keepdims=True)
    return (y - mu) * lax.rsqrt(var + LN_EPS) * g + b


def _mix_kernel(x_ref, of_ref, os_ref, on_ref, wf_ref, ws_ref, wn_ref, g_ref, b_ref, rw_ref, rb_ref,
                x1_ref, ri_ref, rwt_ref, cnt_ref, carry_ref, *, alpha):
    i = pl.program_id(0)
    tm = x_ref.shape[0]
    lane = _lane_iota()

    @pl.when(i == 0)
    def _init():
        carry_ref[...] = jnp.zeros_like(carry_ref)

    mix = jnp.dot(of_ref[...], wf_ref[...], preferred_element_type=F32)
    mix = mix + jnp.dot(os_ref[...], ws_ref[...], preferred_element_type=F32)
    mix = mix + jnp.dot(on_ref[...], wn_ref[...], preferred_element_type=F32)
    x1 = _layer_norm(alpha * x_ref[...] + mix, g_ref[...], b_ref[...])
    x1_ref[...] = x1

    logits = jnp.dot(x1.astype(BF16), rw_ref[...], preferred_element_type=F32) + rb_ref[...]
    v = jnp.where(lane < N_EXPERTS, logits, NEG)
    lane_f = lane.astype(F32)
    tops, hots = [], []
    for _ in range(TOP_K):
        mx = jnp.max(v, axis=-1, keepdims=True)
        idx = jnp.min(jnp.where(v == mx, lane_f, float(LANES)), axis=-1, keepdims=True)
        hot = lane_f == idx
        tops.append((mx, idx))
        hots.append(hot)
        v = jnp.where(hot, 2.0 * NEG, v)
    es = [jnp.exp(mx - tops[0][0]) for mx, _ in tops]
    den = es[0] + es[1] + es[2] + es[3]

    member = jnp.zeros((tm, LANES), F32)
    for hot in hots:
        member = member + jnp.where(hot, 1.0, 0.0)
    r = lax.broadcasted_iota(I32, (tm, tm), 0)
    c = lax.broadcasted_iota(I32, (tm, tm), 1)
    strict = jnp.where(c < r, 1.0, 0.0).astype(BF16)
    before = jnp.dot(strict, member.astype(BF16), preferred_element_type=F32) + carry_ref[0:1, :]
    ri = jnp.zeros((tm, LANES), F32)
    rwt = jnp.zeros((tm, LANES), F32)
    for k in range(TOP_K):
        rank = jnp.sum(jnp.where(hots[k], before, 0.0), axis=-1, keepdims=True)
        ri = jnp.where(lane == k, tops[k][1], ri)
        ri = jnp.where(lane == TOP_K + k, rank, ri)
        rwt = jnp.where(lane == k, es[k] / den, rwt)
    ri_ref[...] = ri.astype(I32)
    rwt_ref[...] = rwt
    new = carry_ref[0:1, :] + jnp.sum(member, axis=0, keepdims=True)
    carry_ref[...] = jnp.zeros_like(carry_ref) + new
    cnt_ref[...] = jnp.zeros_like(cnt_ref) + new


def _mix(x2, of, osw, on, wf, ws, wn, g, b, rw, rb, alpha, tm):
    n, d = x2.shape
    full = lambda a: pl.BlockSpec(a.shape, lambda i: (0,) * a.ndim)
    rows = lambda a: pl.BlockSpec((tm, a.shape[1]), lambda i: (i, 0))
    return pl.pallas_call(
        functools.partial(_mix_kernel, alpha=alpha),
        grid=(n // tm,),
        in_specs=[rows(x2), rows(of), rows(osw), rows(on), full(wf), full(ws), full(wn),
                  full(g), full(b), full(rw), full(rb)],
        out_specs=[pl.BlockSpec((tm, d), lambda i: (i, 0)),
                   pl.BlockSpec((tm, LANES), lambda i: (i, 0)),
                   pl.BlockSpec((tm, LANES), lambda i: (i, 0)),
                   pl.BlockSpec((8, LANES), lambda i: (0, 0))],
        out_shape=[jax.ShapeDtypeStruct((n, d), F32), jax.ShapeDtypeStruct((n, LANES), I32),
                   jax.ShapeDtypeStruct((n, LANES), F32), jax.ShapeDtypeStruct((8, LANES), F32)],
        scratch_shapes=[pltpu.VMEM((8, LANES), F32)],
        compiler_params=_params("arbitrary"),
        name="mix_ln_router",
    )(x2, of, osw, on, wf, ws, wn, g, b, rw, rb)


def _dispatch_kernel(pe_ref, pd_ref, nu_ref, dest_ref, x_ref, xs_hbm, zbuf, zsem, sem, *, rows):
    i = pl.program_id(0)
    tokens = x_ref.shape[0]
    nblk = xs_hbm.shape[0] // rows

    @pl.when(i == 0)
    def _zero_fill():
        zbuf[...] = jnp.zeros_like(zbuf)

        def block(start):
            return pltpu.make_async_copy(zbuf, xs_hbm.at[pl.ds(pl.multiple_of(start, rows), rows)], zsem)

        for phase in ("start", "wait"):
            for e in range(N_EXPERTS):
                @pl.when(pd_ref[e] > 0)
                def _(e=e, phase=phase):
                    getattr(block(pe_ref[e] - rows), phase)()

            def tail(j, _, phase=phase):
                getattr(block(j * rows), phase)()
                return 0

            lax.fori_loop(nu_ref[0], nblk, tail, 0)

    for t in range(tokens):
        for k in range(TOP_K):
            c = t * TOP_K + k
            pltpu.make_async_copy(x_ref.at[pl.ds(t, 1)],
                                  xs_hbm.at[pl.ds(dest_ref[c], 1)], sem).start(priority=c % 2)
    for k in range(TOP_K):
        pltpu.make_async_copy(x_ref, xs_hbm.at[pl.ds(0, tokens)], sem).wait()


def _dispatch(pad_end, padded, n_used, dest_flat, x1, rows_total, tokens, rows):
    n, d = x1.shape
    grid_spec = pltpu.PrefetchScalarGridSpec(
        num_scalar_prefetch=3,
        grid=(n // tokens,),
        in_specs=[pl.BlockSpec((tokens * TOP_K,), lambda i, *_: (i,), memory_space=pltpu.SMEM),
                  pl.BlockSpec((tokens, d), lambda i, *_: (i, 0))],
        out_specs=pl.BlockSpec(memory_space=pl.ANY),
        scratch_shapes=[pltpu.VMEM((rows, d), F32), pltpu.SemaphoreType.DMA(()), pltpu.SemaphoreType.DMA(())],
    )
    return pl.pallas_call(
        functools.partial(_dispatch_kernel, rows=rows),
        grid_spec=grid_spec,
        out_shape=jax.ShapeDtypeStruct((rows_total, d), F32),
        compiler_params=_params("arbitrary"),
        name="moe_dispatch",
    )(pad_end, padded, n_used, dest_flat, x1)


def _ffn_kernel(be_ref, nu_ref, x_ref, wg_ref, wu_ref, bg_ref, bu_ref, wd_ref, bd_ref, y_ref):
    i = pl.program_id(0)

    @pl.when(i < nu_ref[0])
    def _run():
        x = x_ref[...].astype(BF16)
        gate = jnp.dot(x, wg_ref[0], preferred_element_type=F32) + bg_ref[0]
        up = jnp.dot(x, wu_ref[0], preferred_element_type=F32) + bu_ref[0]
        gate = jnp.minimum(gate, SWIGLU_LIMIT)
        up = jnp.clip(up, -SWIGLU_LIMIT, SWIGLU_LIMIT)
        glu = gate * jax.nn.sigmoid(gate * SWIGLU_ALPHA)
        act = ((up + 1.0) * glu).astype(BF16)
        y_ref[...] = jnp.dot(act, wd_ref[0], preferred_element_type=F32) + bd_ref[0]

    @pl.when(i >= nu_ref[0])
    def _skip():
        y_ref[...] = jnp.zeros_like(y_ref)


def _ffn(blk_e, n_used, xs, wg, wu, bg, bu, wd, bd, rows):
    p, d = xs.shape
    wspec = lambda a: pl.BlockSpec((1,) + a.shape[1:], lambda i, be, nu: (be[i], 0, 0))
    grid_spec = pltpu.PrefetchScalarGridSpec(
        num_scalar_prefetch=2,
        grid=(p // rows,),
        in_specs=[pl.BlockSpec((rows, d), lambda i, be, nu: (i, 0)),
                  wspec(wg), wspec(wu), wspec(bg), wspec(bu), wspec(wd), wspec(bd)],
        out_specs=pl.BlockSpec((rows, d), lambda i, be, nu: (i, 0)),
    )
    return pl.pallas_call(
        _ffn_kernel,
        grid_spec=grid_spec,
        out_shape=jax.ShapeDtypeStruct((p, d), F32),
        compiler_params=_params("arbitrary"),
        name="moe_ffn",
    )(blk_e, n_used, xs, wg, wu, bg, bu, wd, bd)


def _combine_kernel(destc_ref, destn_ref, x1_ref, w_ref, g_ref, b_ref, y_hbm, o_ref, buf0, buf1, sem, *, alpha):
    i = pl.program_id(0)
    last = pl.num_programs(0) - 1
    tokens = x1_ref.shape[0]
    bufs = (buf0, buf1)

    def issue(dest_ref, dst):
        for t in range(tokens):
            for k in range(TOP_K):
                c = t * TOP_K + k
                pltpu.make_async_copy(y_hbm.at[pl.ds(dest_ref[c], 1)],
                                      bufs[dst].at[k, pl.ds(t, 1)], sem.at[dst]).start(priority=c % 2)

    @pl.when(i == 0)
    def _prologue():
        issue(destc_ref, 0)

    def run(slot):
        @pl.when(i < last)
        def _():
            issue(destn_ref, 1 - slot)

        for k in range(TOP_K):
            pltpu.make_async_copy(y_hbm.at[pl.ds(0, tokens)], bufs[slot].at[k], sem.at[slot]).wait()
        w = w_ref[...]
        ffn = jnp.zeros(x1_ref.shape, F32)
        for k in range(TOP_K):
            ffn = ffn + bufs[slot][k] * w[:, k:k + 1]
        o_ref[...] = _layer_norm(alpha * x1_ref[...] + ffn, g_ref[...], b_ref[...])

    for slot in range(2):
        pl.when(i % 2 == slot)(functools.partial(run, slot))


def _combine(dest_flat, x1, rwt, g, b, y, alpha, tokens):
    n, d = x1.shape
    steps = n // tokens
    full = lambda a: pl.BlockSpec(a.shape, lambda i: (0,) * a.ndim)
    return pl.pallas_call(
        functools.partial(_combine_kernel, alpha=alpha),
        grid=(steps,),
        in_specs=[pl.BlockSpec((tokens * TOP_K,), lambda i: (i,), memory_space=pltpu.SMEM),
                  pl.BlockSpec((tokens * TOP_K,), lambda i: (jnp.minimum(i + 1, steps - 1),),
                               memory_space=pltpu.SMEM),
                  pl.BlockSpec((tokens, d), lambda i: (i, 0)),
                  pl.BlockSpec((tokens, LANES), lambda i: (i, 0)),
                  full(g), full(b),
                  pl.BlockSpec(memory_space=pl.ANY)],
        out_specs=pl.BlockSpec((tokens, d), lambda i: (i, 0)),
        out_shape=jax.ShapeDtypeStruct((n, d), F32),
        scratch_shapes=[pltpu.VMEM((TOP_K, tokens, d), F32), pltpu.VMEM((TOP_K, tokens, d), F32),
                        pltpu.SemaphoreType.DMA((2,))],
        compiler_params=_params("arbitrary"),
        name="moe_combine_ln",
    )(dest_flat, dest_flat, x1, rwt, g, b, y)


def _prep_gate_up_kernel(w_ref, p_ref, g_ref, u_ref):
    groups = w_ref.shape[3] // (2 * LANES)
    for c in range(groups):
        blk = w_ref[0, 0, :, c * 2 * LANES:(c + 1) * 2 * LANES].astype(BF16)
        o = jnp.dot(blk, p_ref[...], preferred_element_type=F32)
        g_ref[0, :, c * LANES:(c + 1) * LANES] = o[:, :LANES].astype(BF16)
        u_ref[0, :, c * LANES:(c + 1) * LANES] = o[:, LANES:].astype(BF16)


def _prep_gate_up(w_gu_all, layer):
    _, e, d, f2 = w_gu_all.shape
    perm = np.zeros((2 * LANES, 2 * LANES), np.float32)
    perm[2 * np.arange(LANES), np.arange(LANES)] = 1.0
    perm[2 * np.arange(LANES) + 1, LANES + np.arange(LANES)] = 1.0
    return pl.pallas_call(
        _prep_gate_up_kernel,
        grid=(e,),
        in_specs=[pl.BlockSpec((1, 1, d, f2), lambda i: (layer, i, 0, 0)),
                  pl.BlockSpec((2 * LANES, 2 * LANES), lambda i: (0, 0))],
        out_specs=[pl.BlockSpec((1, d, f2 // 2), lambda i: (i, 0, 0)),
                   pl.BlockSpec((1, d, f2 // 2), lambda i: (i, 0, 0))],
        out_shape=[jax.ShapeDtypeStruct((e, d, f2 // 2), BF16), jax.ShapeDtypeStruct((e, d, f2 // 2), BF16)],
        compiler_params=_params("arbitrary"),
        name="prep_gate_up",
    )(w_gu_all, jnp.asarray(perm, BF16))


def _gqa_cols(base):
    idx = np.empty(GQA_GROUP * KV_HEADS * HEAD_DIM, np.int64)
    for g in range(GQA_GROUP):
        for kv in range(KV_HEADS):
            dst = (g * KV_HEADS + kv) * HEAD_DIM
            src = (kv * GQA_GROUP + g) * HEAD_DIM
            idx[dst:dst + HEAD_DIM] = base + src + np.arange(HEAD_DIM)
    return idx


def _inproj_layout():
    fw = FOX_HEADS * HEAD_DIM
    qw = SWA_Q_HEADS * HEAD_DIM
    kw = KV_HEADS * HEAD_DIM
    sizes = (fw, fw, fw, FOX_HEADS, qw, kw, kw, qw, kw, kw, kw, kw, kw, kw, 3 * NSA_Q_HEADS)
    off = np.concatenate([[0], np.cumsum(sizes)])
    (o_fq, o_fk, o_fv, o_ff, o_sq, o_sk, o_sv, o_nq, o_nkc, o_nvc, o_nks, o_nvs, o_nkw, o_nvw, o_ng, total) = off
    ar = np.arange
    cols = np.concatenate([
        _gqa_cols(o_sq), _gqa_cols(o_nq), o_fq + ar(fw), o_fk + ar(fw), o_fv + ar(fw),
        o_sk + ar(kw), o_sv + ar(kw), o_nks + ar(kw), o_nvs + ar(kw), o_nkw + ar(kw), o_nvw + ar(kw),
        o_nkc + ar(kw), o_nvc + ar(kw)])
    gate = np.full(LANES, total, np.int64)
    gate[:FOX_HEADS] = o_ff + ar(FOX_HEADS)
    for r in range(3):
        for g in range(GQA_GROUP):
            for kv in range(KV_HEADS):
                gate[FOX_HEADS + (r * GQA_GROUP + g) * KV_HEADS + kv] = o_ng + (kv * GQA_GROUP + g) * 3 + r
    cols = np.concatenate([cols, gate])
    scale = np.ones(cols.shape[0], np.float32)
    scale[C_SQ:C_SQ + qw] = HEAD_DIM ** -0.5
    scale[C_NQ:C_NQ + qw] = HEAD_DIM ** -0.5
    scale[C_FQ:C_FQ + fw] = HEAD_DIM ** -0.5
    return cols, scale


def _outproj_rows():
    fw = FOX_HEADS * HEAD_DIM
    qw = SWA_Q_HEADS * HEAD_DIM
    return np.arange(fw), _gqa_cols(fw), _gqa_cols(fw + qw)


def _overlap_matrix_t(s_len, nc):
    ns = s_len // SEL_BLOCK
    cmp_start = np.arange(nc) * CMP_STRIDE
    sel_start = np.arange(ns) * SEL_BLOCK
    ov = np.clip(np.minimum(cmp_start[:, None] + CMP_BLOCK, sel_start[None, :] + SEL_BLOCK)
                 - np.maximum(cmp_start[:, None], sel_start[None, :]), 0, None) / CMP_BLOCK
    ov[(s_len - CMP_BLOCK) // CMP_STRIDE + 1:] = 0.0
    out = np.zeros((KV_HEADS, LANES, nc), np.float32)
    for kv in range(KV_HEADS):
        r0 = _spare(kv) + L_BLK
        out[kv, r0:r0 + ns, :] = ov.T
    return out


def _compress_weights(pe, w1, w2):
    pe2 = jnp.concatenate([pe, pe], axis=-1)
    w1r = w1.reshape(2, CMP_BLOCK, HEAD_DIM, HEAD_DIM)
    z = jnp.zeros_like(w1r)
    w1b = jnp.concatenate([jnp.concatenate([w1r, z], axis=-1), jnp.concatenate([z, w1r], axis=-1)], axis=-2)
    z2 = jnp.zeros_like(w2)
    w2b = jnp.concatenate([jnp.concatenate([w2, z2], axis=-1), jnp.concatenate([z2, w2], axis=-1)], axis=-2)
    return pe2, w1b.astype(BF16), w2b.astype(BF16)


def kernel(x, w_in, b_in, sinks, cmp_pe, cmp_w1, cmp_w2, w_out, ln1_g, ln1_b, router_w, router_b,
           w_gate_up, b_gate_up, w_down, b_down, ln2_g, ln2_b):
    depth = w_in.shape[0]
    bsz, s_len, d = x.shape
    n = bsz * s_len
    nk = n * TOP_K
    alpha = float((2.0 * depth) ** 0.25)
    assert s_len % SEL_KEYS == 0 and s_len // SEL_BLOCK <= MAX_SEL_BLOCKS and n % 512 == 0
    slopes_swa, slopes_nsa = _alibi()
    cols, scale = _inproj_layout()
    rows_f, rows_s, rows_n = _outproj_rows()
    nc = s_len // CMP_STRIDE
    ovt = jnp.asarray(_overlap_matrix_t(s_len, nc), BF16)
    nblk = -(-nk // MOE_ROWS) + N_EXPERTS
    p_rows = nblk * MOE_ROWS
    f = w_down.shape[2]

    x2 = x.reshape(n, d)
    for l in range(depth):
        w_ext = jnp.concatenate([w_in[l], jnp.zeros((d, 1), F32)], axis=1)
        b_ext = jnp.concatenate([b_in[l], jnp.zeros((1,), F32)])
        w_l = (jnp.take(w_ext, cols, axis=1) * scale).astype(BF16)
        b_l = (jnp.take(b_ext, cols) * scale).reshape(1, -1)
        hb, hf = _inproj(x2, w_l, b_l, 512)
        hb3 = hb.reshape(bsz, s_len, CB)
        hf3 = hf.reshape(bsz, s_len, CF)
        cg = _gates(hf3)
        o_fox = _fox(hb3, cg)
        o_swa = _swa(hb3, sinks[l], slopes_swa)
        grp = hf3[:, :, :2 * LANES].reshape(bsz, nc, CMP_STRIDE * 2 * LANES)
        pe2, w1b, w2b = _compress_weights(cmp_pe[l], cmp_w1[l], cmp_w2[l])
        kc, vc = _compress(grp, pe2, w1b, w2b)
        o_nsa = _nsa(hb3, kc, vc, cg, ovt, slopes_nsa)

        wo = w_out[l].astype(BF16)
        rw = jnp.concatenate([router_w[l], jnp.zeros((d, LANES - N_EXPERTS), F32)], axis=1).astype(BF16)
        rb = jnp.concatenate([router_b[l], jnp.zeros((LANES - N_EXPERTS,), F32)]).reshape(1, LANES)
        x1, ri, rwt, cnt = _mix(
            x2, o_fox.reshape(n, -1), o_swa.reshape(n, -1), o_nsa.reshape(n, -1),
            jnp.take(wo, rows_f, axis=0), jnp.take(wo, rows_s, axis=0), jnp.take(wo, rows_n, axis=0),
            ln1_g[l].reshape(1, d), ln1_b[l].reshape(1, d), rw, rb, alpha, 512)

        counts = cnt[0, :N_EXPERTS].astype(I32)
        padded = (counts + MOE_ROWS - 1) // MOE_ROWS * MOE_ROWS
        pad_end = jnp.cumsum(padded)
        pad_start = pad_end - padded
        dest = (jnp.take(pad_start, ri[:, :TOP_K]) + ri[:, TOP_K:2 * TOP_K]).reshape(nk)
        blk_start = jnp.arange(nblk, dtype=I32) * MOE_ROWS
        blk_e = jnp.minimum(jnp.sum((pad_end[None, :] <= blk_start[:, None]).astype(I32), axis=1),
                            N_EXPERTS - 1)
        n_used = (pad_end[-1:] // MOE_ROWS).astype(I32)

        xs = _dispatch(pad_end, padded, n_used, dest, x1, p_rows, 256, MOE_ROWS)
        w_gate, w_up = _prep_gate_up(w_gate_up, l)
        bgu = b_gate_up[l].reshape(N_EXPERTS, 1, f, 2)
        y = _ffn(blk_e, n_used, xs, w_gate, w_up,
                 bgu[..., 0], bgu[..., 1], w_down[l].astype(BF16), b_down[l].reshape(N_EXPERTS, 1, d),
                 MOE_ROWS)
        x2 = _combine(dest, x1, rwt, ln2_g[l].reshape(1, d), ln2_b[l].reshape(1, d), y, alpha, 128)
    return x2.reshape(bsz, s_len, d)
```

```python
import functools

import numpy as np
import jax
import jax.numpy as jnp
from jax import lax
from jax.experimental import pallas as pl
from jax.experimental.pallas import tpu as pltpu

F32 = jnp.float32
BF16 = jnp.bfloat16
I32 = jnp.int32

HEAD_DIM = 64
LANES = 128
N_HEADS = 16
SWA_Q_HEADS = 6
NSA_Q_HEADS = 6
FOX_HEADS = 4
KV_HEADS = 2
GQA_GROUP = 3
SWA_WINDOW = 128
CMP_BLOCK = 32
CMP_STRIDE = 16
SEL_BLOCK = 64
SEL_TOP_N = 8
NSA_WINDOW = 512
SEL_FORCE = 1.0e4
N_EXPERTS = 32
TOP_K = 4
SWIGLU_LIMIT = 7.0
SWIGLU_ALPHA = 1.702
LN_EPS = 1e-5
NEG = -1.0e30

Q_TILE = 256
FOX_TILE = 256
FOX_KEYS = 512
SEL_KEYS = 512
MAX_SEL_BLOCKS = 32
MOE_ROWS = 512
VMEM_LIMIT = 56 * 1024 * 1024

L_BLK = 0
L_QT = 32
L_KH = 35
L_KL = 38
L_PAD = 41
L_QB = 42
LOG2E = 1.4426950408889634

C_SQ, C_NQ, C_FQ, C_FK, C_FV = 0, 384, 768, 1024, 1280
C_SK, C_SV, C_NKS, C_NVS, C_NKW, C_NVW = 1536, 1664, 1792, 1920, 2048, 2176
CB = 2304
CF = 384


def _alibi():
    n = SWA_Q_HEADS + NSA_Q_HEADS
    s = (2.0 ** (-8.0 * np.arange(1, n + 1) / n)).astype(np.float32) * np.float32(LOG2E)
    return s[0::2].reshape(KV_HEADS, GQA_GROUP), s[1::2].reshape(KV_HEADS, GQA_GROUP)


def _query_tables(slopes, tq):
    out = np.zeros((KV_HEADS, GQA_GROUP, tq, LANES), np.float32)
    r = np.arange(tq, dtype=np.float32)
    for kv in range(KV_HEADS):
        a0 = _spare(kv)
        for g in range(GQA_GROUP):
            v = (r * -slopes[kv][g]).astype(np.float32)
            for c in range(3):
                p = v.astype(BF16).astype(np.float32)
                out[kv, g, :, a0 + L_QT + c] = p
                v = (v - p).astype(np.float32)
            for c, part in enumerate(_bf16_parts(slopes[kv][g])):
                out[kv, g, :, a0 + L_KH + c] = part
                out[kv, g, :, a0 + L_KL + c] = part
    return out


def _bf16_parts(v):
    v = np.float32(v)
    out = []
    for _ in range(3):
        p = np.float32(np.asarray(v, dtype=BF16))
        out.append(float(p))
        v = np.float32(v - p)
    return out


def _split3(x):
    hi = x.astype(BF16).astype(F32)
    r = x - hi
    mid = r.astype(BF16).astype(F32)
    lo = (r - mid).astype(BF16).astype(F32)
    return hi, mid, lo


def _lane_iota():
    return lax.broadcasted_iota(I32, (1, LANES), 1)


def _row_iota():
    return lax.broadcasted_iota(I32, (LANES, 1), 0)


def _spare(kv_half):
    return HEAD_DIM if kv_half == 0 else 0


def _dot_nt(a, b):
    return lax.dot_general(a, b, (((1,), (1,)), ((), ())), preferred_element_type=F32)


def _transposed(x):
    return x.astype(F32).T.astype(BF16)


def _params(*sem):
    return pltpu.CompilerParams(dimension_semantics=sem, vmem_limit_bytes=VMEM_LIMIT)


def _inproj_kernel(x_ref, w_ref, b_ref, hb_ref, hf_ref):
    acc = jnp.dot(x_ref[...].astype(BF16), w_ref[...], preferred_element_type=F32) + b_ref[...]
    hb_ref[...] = acc[:, :CB].astype(BF16)
    hf_ref[...] = acc[:, CB:]


def _inproj(x2, w, b, tm):
    n, d = x2.shape
    ct = w.shape[1]
    return pl.pallas_call(
        _inproj_kernel,
        grid=(n // tm,),
        in_specs=[pl.BlockSpec((tm, d), lambda i: (i, 0)),
                  pl.BlockSpec((d, ct), lambda i: (0, 0)),
                  pl.BlockSpec((1, ct), lambda i: (0, 0))],
        out_specs=[pl.BlockSpec((tm, CB), lambda i: (i, 0)),
                   pl.BlockSpec((tm, CF), lambda i: (i, 0))],
        out_shape=[jax.ShapeDtypeStruct((n, CB), BF16), jax.ShapeDtypeStruct((n, CF), F32)],
        compiler_params=_params("arbitrary"),
        name="inproj",
    )(x2, w, b)


def _gates_kernel(g_ref, o_ref, *, chunk):
    s = g_ref.shape[1]
    lane = _lane_iota()
    r = lax.broadcasted_iota(I32, (chunk, chunk), 0)
    c = lax.broadcasted_iota(I32, (chunk, chunk), 1)
    tri = jnp.where(c <= r, 1.0, 0.0).astype(BF16)

    def body(j, carry):
        x = g_ref[0, pl.ds(pl.multiple_of(j * chunk, chunk), chunk), :]
        ls = (jnp.minimum(x, 0.0) - jnp.log1p(jnp.exp(-jnp.abs(x)))) * LOG2E
        cs = carry
        for part in _split3(ls):
            cs = cs + jnp.dot(tri, part.astype(BF16), preferred_element_type=F32)
        sg = jax.nn.sigmoid(x)
        o_ref[0, pl.ds(pl.multiple_of(j * chunk, chunk), chunk), :] = jnp.where(lane < FOX_HEADS, cs, sg)
        return cs[chunk - 1:chunk, :]

    lax.fori_loop(0, s // chunk, body, jnp.zeros((1, LANES), F32))


def _gates(hf3):
    b, s, _ = hf3.shape
    chunk = min(256, s)
    return pl.pallas_call(
        functools.partial(_gates_kernel, chunk=chunk),
        grid=(b,),
        in_specs=[pl.BlockSpec((1, s, LANES), lambda i: (i, 0, 2))],
        out_specs=pl.BlockSpec((1, s, LANES), lambda i: (i, 0, 0)),
        out_shape=jax.ShapeDtypeStruct((b, s, LANES), F32),
        compiler_params=_params("arbitrary"),
        name="gates",
    )(hf3)


def _softmax_steps(carries, scores, values):
    stats = []
    for (m, l, _), sc in zip(carries, scores):
        m_new = jnp.maximum(m, jnp.max(sc, axis=0, keepdims=True))
        alpha = jnp.exp2(m - m_new)
        pr = jnp.exp2(sc - m_new)
        stats.append((m_new, alpha * l + jnp.sum(pr, axis=0, keepdims=True), alpha, pr.astype(BF16)))
    return tuple((m, l, alpha * acc + jnp.dot(vt, pr, preferred_element_type=F32))
                 for (m, l, alpha, pr), (_, _, acc), vt in zip(stats, carries, values))


def _softmax_init(queries):
    return (jnp.full((1, queries), NEG, F32), jnp.zeros((1, queries), F32), jnp.zeros((LANES, queries), F32))


def _fox_kernel(q_ref, k_ref, v_ref, cq_ref, ck_ref, o_ref, ka_ref, vt_ref, *, tile, keys):
    i = pl.program_id(1)
    s_len = k_ref.shape[1]
    lane = _lane_iota()
    npair = FOX_HEADS // 2

    @pl.when(i == 0)
    def _fill():
        for h in range(FOX_HEADS):
            p, half = divmod(h, 2)
            a0 = _spare(half)
            k = k_ref[0, :, p * LANES:(p + 1) * LANES].astype(F32)
            hi, mid, lo = _split3(-ck_ref[0, :, h:h + 1])
            ka = jnp.where((lane >= a0 + L_QT) & (lane < a0 + L_QT + 3), 1.0, k)
            ka = jnp.where(lane == a0 + L_KH, hi, ka)
            ka = jnp.where(lane == a0 + L_KH + 1, mid, ka)
            ka = jnp.where(lane == a0 + L_KH + 2, lo, ka)
            ka_ref[h] = ka.astype(BF16)
        for p in range(npair):
            for c in range(s_len // keys):
                vt_ref[p, c] = _transposed(v_ref[0, c * keys:(c + 1) * keys, p * LANES:(p + 1) * LANES])

    qas = []
    for h in range(FOX_HEADS):
        p, half = divmod(h, 2)
        a0 = _spare(half)
        q = q_ref[0, :, p * LANES:(p + 1) * LANES].astype(F32)
        in_half = (lane < HEAD_DIM) if half == 0 else (lane >= HEAD_DIM)
        hi, mid, lo = _split3(cq_ref[0, :, h:h + 1])
        qa = jnp.where(in_half, q, 0.0)
        qa = jnp.where(lane == a0 + L_QT, hi, qa)
        qa = jnp.where(lane == a0 + L_QT + 1, mid, qa)
        qa = jnp.where(lane == a0 + L_QT + 2, lo, qa)
        qas.append(jnp.where((lane >= a0 + L_KH) & (lane < a0 + L_KH + 3), 1.0, qa).astype(BF16))

    nfull = (i * tile) // keys
    key = lax.broadcasted_iota(I32, (keys, tile), 0) + nfull * keys
    qry = lax.broadcasted_iota(I32, (keys, tile), 1) + i * tile
    causal = jnp.where(key <= qry, 0.0, NEG)

    def step(j, carry, masked):
        start = pl.multiple_of(j * keys, keys)
        scores = [_dot_nt(ka_ref[h, pl.ds(start, keys), :], qas[h]) for h in range(FOX_HEADS)]
        if masked:
            scores = [sc + causal for sc in scores]
        return _softmax_steps(carry, scores, [vt_ref[h // 2, j] for h in range(FOX_HEADS)])

    init = tuple(_softmax_init(tile) for _ in range(FOX_HEADS))
    carry = lax.fori_loop(0, nfull, functools.partial(step, masked=False), init)
    carry = step(nfull, carry, True)
    row = _row_iota()
    for p in range(npair):
        (_, l0, a0_), (_, l1, a1_) = carry[2 * p], carry[2 * p + 1]
        o_ref[0, :, p * LANES:(p + 1) * LANES] = jnp.where(row < HEAD_DIM, a0_ / l0, a1_ / l1).T.astype(BF16)


def _fox(hb3, cg):
    b, s, _ = hb3.shape
    tile = min(FOX_TILE, s)
    keys = min(FOX_KEYS, s)
    assert keys % tile == 0 and s % keys == 0
    w = FOX_HEADS * HEAD_DIM
    return pl.pallas_call(
        functools.partial(_fox_kernel, tile=tile, keys=keys),
        grid=(b, s // tile),
        in_specs=[pl.BlockSpec((1, tile, w), lambda bi, i: (bi, i, C_FQ // w)),
                  pl.BlockSpec((1, s, w), lambda bi, i: (bi, 0, C_FK // w)),
                  pl.BlockSpec((1, s, w), lambda bi, i: (bi, 0, C_FV // w)),
                  pl.BlockSpec((1, tile, LANES), lambda bi, i: (bi, i, 0)),
                  pl.BlockSpec((1, s, LANES), lambda bi, i: (bi, 0, 0))],
        out_specs=pl.BlockSpec((1, tile, w), lambda bi, i: (bi, i, 0)),
        out_shape=jax.ShapeDtypeStruct((b, s, w), BF16),
        scratch_shapes=[pltpu.VMEM((FOX_HEADS, s, LANES), BF16),
                        pltpu.VMEM((FOX_HEADS // 2, s // keys, LANES, keys), BF16)],
        compiler_params=_params("arbitrary", "arbitrary"),
        name="fox",
    )(hb3, hb3, hb3, cg, cg)


def _fill_banded(k_ref, v_ref, kp_ref, vt_ref, window):
    s_len = k_ref.shape[1]
    lane = _lane_iota()
    k = k_ref[0].astype(F32)
    pos = lax.broadcasted_iota(I32, (s_len, 1), 0) + window
    pos_hi = ((pos >> 8) << 8).astype(F32)
    pos_lo = (pos & 255).astype(F32)
    for kv in range(KV_HEADS):
        a0 = _spare(kv)
        ka = jnp.where((lane >= a0 + L_QT) & (lane < a0 + L_QT + 3), 1.0, k)
        ka = jnp.where((lane >= a0 + L_QB) & (lane < a0 + L_QB + 3), 1.0, ka)
        ka = jnp.where((lane >= a0 + L_KH) & (lane < a0 + L_KH + 3), pos_hi, ka)
        ka = jnp.where((lane >= a0 + L_KL) & (lane < a0 + L_KL + 3), pos_lo, ka)
        ka = jnp.where(lane == a0 + L_PAD, 0.0, ka)
        kp_ref[kv, pl.ds(window, s_len), :] = ka.astype(BF16)
        pad = jnp.where(lane == a0 + L_PAD, NEG, 0.0) + jnp.zeros((window, LANES), F32)
        kp_ref[kv, pl.ds(0, window), :] = pad.astype(BF16)
    npad = window // LANES
    for c in range(npad):
        vt_ref[c] = jnp.zeros((LANES, LANES), BF16)
    for c in range(s_len // LANES):
        vt_ref[npad + c] = _transposed(v_ref[0, c * LANES:(c + 1) * LANES, :])


def _alibi_query(q, table, kv, t0, slope, lane, pad_lane):
    a0 = _spare(kv)
    in_half = (lane < HEAD_DIM) if kv == 0 else (lane >= HEAD_DIM)
    hi, mid, lo = _split3(t0 * (-float(slope)))
    qa = jnp.where(in_half, q.astype(F32), table)
    qa = jnp.where(lane == a0 + L_QB, hi, qa)
    qa = jnp.where(lane == a0 + L_QB + 1, mid, qa)
    qa = jnp.where(lane == a0 + L_QB + 2, lo, qa)
    if pad_lane:
        qa = jnp.where(lane == a0 + L_PAD, 1.0, qa)
    return qa


def _banded_attend(q, qt_ref, kp_ref, vt_ref, i, window, slopes, sink_ref):
    tq = q.shape[0]
    span = window + tq
    lane = _lane_iota()
    start = pl.multiple_of(i * tq, tq)
    t0 = jnp.full((1, LANES), i * tq + window, I32).astype(F32)
    key = lax.broadcasted_iota(I32, (span, tq), 0)
    qry = lax.broadcasted_iota(I32, (span, tq), 1)
    band = jnp.where((key > qry) & (key <= qry + window), 0.0, NEG)
    band = jnp.concatenate([band] * GQA_GROUP, axis=1)
    outs = []
    for kv in range(KV_HEADS):
        qst = jnp.concatenate(
            [_alibi_query(q[:, g * LANES:(g + 1) * LANES], qt_ref[kv, g], kv, t0, slopes[kv][g], lane,
                          True).astype(BF16)
             for g in range(GQA_GROUP)], axis=0)
        sc = _dot_nt(kp_ref[kv, pl.ds(start, span), :], qst) + band
        m = jnp.max(sc, axis=0, keepdims=True)
        if sink_ref is not None:
            sink = jnp.concatenate([jnp.full((1, tq), sink_ref[kv * GQA_GROUP + g], F32)
                                    for g in range(GQA_GROUP)], axis=1)
            m = jnp.maximum(m, sink)
        pr = jnp.exp2(sc - m)
        den = jnp.sum(pr, axis=0, keepdims=True)
        if sink_ref is not None:
            den = den + jnp.exp2(sink - m)
        pr = pr.astype(BF16)
        o = jnp.zeros((LANES, GQA_GROUP * tq), F32)
        base = i * (tq // LANES)
        for c in range(0, span // LANES, 2):
            wide = min(2, span // LANES - c)
            vt = jnp.concatenate([vt_ref[base + c + j] for j in range(wide)], axis=1)
            o = o + jnp.dot(vt, pr[c * LANES:(c + wide) * LANES, :], preferred_element_type=F32)
        outs.append(o / den)
    return outs


def _merge_kv(outs, g, tq, row):
    return jnp.where(row < HEAD_DIM, outs[0][:, g * tq:(g + 1) * tq], outs[1][:, g * tq:(g + 1) * tq])


def _swa_kernel(sink_ref, q_ref, k_ref, v_ref, qt_ref, o_ref, kp_ref, vt_ref, *, slopes):
    i = pl.program_id(1)
    tq = q_ref.shape[1]

    @pl.when(i == 0)
    def _fill():
        _fill_banded(k_ref, v_ref, kp_ref, vt_ref, SWA_WINDOW)

    outs = _banded_attend(q_ref[0], qt_ref, kp_ref, vt_ref, i, SWA_WINDOW, slopes, sink_ref)
    row = _row_iota()
    for g in range(GQA_GROUP):
        o_ref[0, :, g * LANES:(g + 1) * LANES] = _merge_kv(outs, g, tq, row).T.astype(BF16)


def _swa(hb3, sinks, slopes):
    b, s, _ = hb3.shape
    w = GQA_GROUP * LANES
    qt = jnp.asarray(_query_tables(slopes, Q_TILE))
    return pl.pallas_call(
        functools.partial(_swa_kernel, slopes=slopes),
        grid=(b, s // Q_TILE),
        in_specs=[pl.BlockSpec(memory_space=pltpu.SMEM),
                  pl.BlockSpec((1, Q_TILE, w), lambda bi, i: (bi, i, C_SQ // w)),
                  pl.BlockSpec((1, s, LANES), lambda bi, i: (bi, 0, C_SK // LANES)),
                  pl.BlockSpec((1, s, LANES), lambda bi, i: (bi, 0, C_SV // LANES)),
                  pl.BlockSpec(qt.shape, lambda bi, i: (0, 0, 0, 0))],
        out_specs=pl.BlockSpec((1, Q_TILE, w), lambda bi, i: (bi, i, 0)),
        out_shape=jax.ShapeDtypeStruct((b, s, w), BF16),
        scratch_shapes=[pltpu.VMEM((KV_HEADS, s + SWA_WINDOW, LANES), BF16),
                        pltpu.VMEM(((s + SWA_WINDOW) // LANES, LANES, LANES), BF16)],
        compiler_params=_params("arbitrary", "arbitrary"),
        name="swa",
    )(sinks, hb3, hb3, hb3, qt)


def _compress_kernel(g_ref, pe_ref, w1_ref, w2_ref, kc_ref, vc_ref):
    nc = g_ref.shape[1]
    for which, out_ref in ((0, kc_ref), (1, vc_ref)):
        ha = jnp.zeros((nc, LANES), F32)
        hb = jnp.zeros((nc, LANES), F32)
        for l in range(CMP_STRIDE):
            x = g_ref[0, :, (2 * l + which) * LANES:(2 * l + which + 1) * LANES]
            xa = (x + pe_ref[which, l:l + 1, :]).astype(BF16)
            xb = (x + pe_ref[which, CMP_STRIDE + l:CMP_STRIDE + l + 1, :]).astype(BF16)
            ha = ha + jnp.dot(xa, w1_ref[which, l], preferred_element_type=F32)
            hb = hb + jnp.dot(xb, w1_ref[which, CMP_STRIDE + l], preferred_element_type=F32)
        hid = ha + pltpu.roll(hb, nc - 1, 0)
        act = 0.5 * hid * (1.0 + jnp.tanh(np.sqrt(2.0 / np.pi).astype(np.float32) * (hid + 0.044715 * (hid * hid * hid))))
        out_ref[0] = jnp.dot(act.astype(BF16), w2_ref[which], preferred_element_type=F32).astype(BF16)


def _compress(g, pe2, w1b, w2b):
    b, nc, gw = g.shape
    return pl.pallas_call(
        _compress_kernel,
        grid=(b,),
        in_specs=[pl.BlockSpec((1, nc, gw), lambda i: (i, 0, 0)),
                  pl.BlockSpec(pe2.shape, lambda i: (0, 0, 0)),
                  pl.BlockSpec(w1b.shape, lambda i: (0, 0, 0, 0)),
                  pl.BlockSpec(w2b.shape, lambda i: (0, 0, 0))],
        out_specs=[pl.BlockSpec((1, nc, LANES), lambda i: (i, 0, 0)),
                   pl.BlockSpec((1, nc, LANES), lambda i: (i, 0, 0))],
        out_shape=[jax.ShapeDtypeStruct((b, nc, LANES), BF16), jax.ShapeDtypeStruct((b, nc, LANES), BF16)],
        compiler_params=_params("arbitrary"),
        name="compress",
    )(g, pe2, w1b, w2b)


def _nsa_kernel(q_ref, ks_ref, vs_ref, kw_ref, vw_ref, kc_ref, vc_ref, cg_ref, ovt_ref, qt_ref, o_ref,
                ksa_ref, vst_ref, kwp_ref, vwt_ref, kca_ref, vct_ref, *, slopes, n_sel):
    i = pl.program_id(1)
    tq = q_ref.shape[1]
    s_len = ks_ref.shape[1]
    nc = kc_ref.shape[1]
    ns = s_len // SEL_BLOCK
    lane = _lane_iota()
    row = _row_iota()

    @pl.when(i == 0)
    def _fill():
        _fill_banded(kw_ref, vw_ref, kwp_ref, vwt_ref, NSA_WINDOW)
        ks = ks_ref[0].astype(F32)
        pos = lax.broadcasted_iota(I32, (s_len, 1), 0)
        pos_hi = ((pos >> 8) << 8).astype(F32)
        pos_lo = (pos & 255).astype(F32)
        blk = pos // SEL_BLOCK
        kc = kc_ref[0].astype(F32)
        cend = lax.broadcasted_iota(I32, (nc, 1), 0) * CMP_STRIDE + (CMP_BLOCK - 1)
        cend_hi = ((cend >> 8) << 8).astype(F32)
        cend_lo = (cend & 255).astype(F32)
        for kv in range(KV_HEADS):
            a0 = _spare(kv)
            ka = jnp.where((lane >= a0 + L_BLK) & (lane < a0 + L_BLK + MAX_SEL_BLOCKS),
                           jnp.where(lane - (a0 + L_BLK) == blk, 1.0, 0.0), ks)
            ka = jnp.where((lane >= a0 + L_QT) & (lane < a0 + L_QT + 3), 1.0, ka)
            ka = jnp.where((lane >= a0 + L_QB) & (lane < a0 + L_QB + 3), 1.0, ka)
            ka = jnp.where((lane >= a0 + L_KH) & (lane < a0 + L_KH + 3), pos_hi, ka)
            ka = jnp.where((lane >= a0 + L_KL) & (lane < a0 + L_KL + 3), pos_lo, ka)
            ksa_ref[kv] = ka.astype(BF16)
            ca = jnp.where((lane >= a0 + L_QT) & (lane < a0 + L_QT + 3), 1.0, kc)
            ca = jnp.where((lane >= a0 + L_QB) & (lane < a0 + L_QB + 3), 1.0, ca)
            ca = jnp.where((lane >= a0 + L_KH) & (lane < a0 + L_KH + 3), cend_hi, ca)
            ca = jnp.where((lane >= a0 + L_KL) & (lane < a0 + L_KL + 3), cend_lo, ca)
            kca_ref[kv] = ca.astype(BF16)
        for c in range(s_len // SEL_KEYS):
            vst_ref[c] = _transposed(vs_ref[0, c * SEL_KEYS:(c + 1) * SEL_KEYS, :])
        vct_ref[...] = _transposed(vc_ref[0])

    q = q_ref[0]
    t0 = jnp.full((1, LANES), i * tq, I32).astype(F32)
    t_row = lax.broadcasted_iota(I32, (1, tq), 1) + i * tq
    cur = t_row // SEL_BLOCK
    cend = lax.broadcasted_iota(I32, (nc, tq), 0) * CMP_STRIDE + (CMP_BLOCK - 1)
    cvalid = jnp.concatenate([cend <= t_row] * GQA_GROUP, axis=1)
    nfull = (i * tq) // SEL_KEYS
    dkey = lax.broadcasted_iota(I32, (SEL_KEYS, tq), 0) + nfull * SEL_KEYS
    dcausal = jnp.where(dkey <= t_row, 0.0, NEG)
    dcausal = jnp.concatenate([dcausal] * GQA_GROUP, axis=1)

    o_cmp, qsels = [], []
    for kv in range(KV_HEADS):
        a0 = _spare(kv)
        qbase = [_alibi_query(q[:, g * LANES:(g + 1) * LANES], qt_ref[kv, g], kv, t0, slopes[kv][g], lane, False)
                 for g in range(GQA_GROUP)]

        qst = jnp.concatenate([x.astype(BF16) for x in qbase], axis=0)
        sc = jnp.where(cvalid, _dot_nt(kca_ref[kv], qst), NEG)
        m = jnp.max(sc, axis=0, keepdims=True)
        m = jnp.where(m > 0.5 * NEG, m, 0.0)
        e = jnp.where(cvalid, jnp.exp2(sc - m), 0.0)
        den = jnp.sum(e, axis=0, keepdims=True)
        pc = (e / jnp.where(den > 0.0, den, 1.0)).astype(BF16)
        o_cmp.append(jnp.dot(vct_ref[...], pc, preferred_element_type=F32))

        p_slc = jnp.zeros((LANES, tq), F32)
        for g in range(GQA_GROUP):
            p_slc = p_slc + jnp.dot(ovt_ref[kv], pc[:, g * tq:(g + 1) * tq], preferred_element_type=F32)
        r0 = a0 + L_BLK
        p_slc = p_slc[r0:r0 + MAX_SEL_BLOCKS, :]
        jb = lax.broadcasted_iota(I32, (MAX_SEL_BLOCKS, 1), 0)
        forced = (jb == 0) | (jb == cur) | (jb == cur - 1)
        score = jnp.where(jb > cur, -1.0, jnp.where(forced, SEL_FORCE, p_slc))
        score = jnp.where(jb < ns, score, -2.0)
        rank = jnp.zeros((MAX_SEL_BLOCKS, tq), F32)
        for b2 in range(ns):
            other = score[b2:b2 + 1, :]
            beats = (other > score) | ((other == score) & (jb > b2))
            rank = rank + jnp.where(beats, 1.0, 0.0)
        selq = jnp.where(rank < float(n_sel), 0.0, NEG)
        pieces = [selq]
        if r0 > 0:
            pieces.insert(0, jnp.zeros((r0, tq), F32))
        if LANES - r0 - MAX_SEL_BLOCKS > 0:
            pieces.append(jnp.zeros((LANES - r0 - MAX_SEL_BLOCKS, tq), F32))
        selq = jnp.concatenate(pieces, axis=0).T
        in_blk = (lane >= r0) & (lane < r0 + MAX_SEL_BLOCKS)
        qsels.append(jnp.concatenate([jnp.where(in_blk, selq, x).astype(BF16) for x in qbase], axis=0))

    def step(j, carry, bias):
        start = pl.multiple_of(j * SEL_KEYS, SEL_KEYS)
        scores = [_dot_nt(ksa_ref[kv, pl.ds(start, SEL_KEYS), :], qsels[kv]) for kv in range(KV_HEADS)]
        if bias is not None:
            scores = [sc + bias for sc in scores]
        return _softmax_steps(carry, scores, [vst_ref[j]] * KV_HEADS)

    init = tuple(_softmax_init(GQA_GROUP * tq) for _ in range(KV_HEADS))
    carry = lax.fori_loop(0, nfull, lambda j, c: step(j, c, None), init)
    carry = step(nfull, carry, dcausal)
    o_sel = [acc / l for (_, l, acc) in carry]

    o_win = _banded_attend(q, qt_ref, kwp_ref, vwt_ref, i, NSA_WINDOW, slopes, None)

    gates = cg_ref[0].T
    for g in range(GQA_GROUP):
        total = jnp.zeros((LANES, tq), F32)
        for r, branch in enumerate((o_cmp, o_sel, o_win)):
            c0 = FOX_HEADS + (r * GQA_GROUP + g) * KV_HEADS
            gate = jnp.where(row < HEAD_DIM, gates[c0:c0 + 1, :], gates[c0 + 1:c0 + 2, :])
            total = total + gate * _merge_kv(branch, g, tq, row)
        o_ref[0, :, g * LANES:(g + 1) * LANES] = total.T.astype(BF16)


def _nsa(hb3, kc, vc, cg, ovt, slopes):
    b, s, _ = hb3.shape
    nc = kc.shape[1]
    w = GQA_GROUP * LANES
    n_sel = min(SEL_TOP_N, s // SEL_BLOCK)
    qt = jnp.asarray(_query_tables(slopes, Q_TILE))
    kvspec = lambda c: pl.BlockSpec((1, s, LANES), lambda bi, i: (bi, 0, c // LANES))
    return pl.pallas_call(
        functools.partial(_nsa_kernel, slopes=slopes, n_sel=n_sel),
        grid=(b, s // Q_TILE),
        in_specs=[pl.BlockSpec((1, Q_TILE, w), lambda bi, i: (bi, i, C_NQ // w)),
                  kvspec(C_NKS), kvspec(C_NVS), kvspec(C_NKW), kvspec(C_NVW),
                  pl.BlockSpec((1, nc, LANES), lambda bi, i: (bi, 0, 0)),
                  pl.BlockSpec((1, nc, LANES), lambda bi, i: (bi, 0, 0)),
                  pl.BlockSpec((1, Q_TILE, LANES), lambda bi, i: (bi, i, 0)),
                  pl.BlockSpec(ovt.shape, lambda bi, i: (0, 0, 0)),
                  pl.BlockSpec(qt.shape, lambda bi, i: (0, 0, 0, 0))],
        out_specs=pl.BlockSpec((1, Q_TILE, w), lambda bi, i: (bi, i, 0)),
        out_shape=jax.ShapeDtypeStruct((b, s, w), BF16),
        scratch_shapes=[pltpu.VMEM((KV_HEADS, s, LANES), BF16),
                        pltpu.VMEM((s // SEL_KEYS, LANES, SEL_KEYS), BF16),
                        pltpu.VMEM((KV_HEADS, s + NSA_WINDOW, LANES), BF16),
                        pltpu.VMEM(((s + NSA_WINDOW) // LANES, LANES, LANES), BF16),
                        pltpu.VMEM((KV_HEADS, nc, LANES), BF16),
                        pltpu.VMEM((LANES, nc), BF16)],
        compiler_params=_params("arbitrary", "arbitrary"),
        name="nsa",
    )(hb3, hb3, hb3, hb3, hb3, kc, vc, cg, ovt, qt)


def _layer_norm(y, g, b):
    mu = jnp.mean(y, axis=-1, keepdims=True)
    var = jnp.mean(jnp.square(y - mu), axis=-1, keepdims=True)
    return (y - mu) * lax.rsqrt(var + LN_EPS) * g + b


def _mix_kernel(x_ref, of_ref, os_ref, on_ref, wf_ref, ws_ref, wn_ref, g_ref, b_ref, rw_ref, rb_ref,
                x1_ref, ri_ref, rwt_ref, cnt_ref, carry_ref, *, alpha):
    i = pl.program_id(0)
    tm = x_ref.shape[0]
    lane = _lane_iota()

    @pl.when(i == 0)
    def _init():
        carry_ref[...] = jnp.zeros_like(carry_ref)

    mix = jnp.dot(of_ref[...], wf_ref[...], preferred_element_type=F32)
    mix = mix + jnp.dot(os_ref[...], ws_ref[...], preferred_element_type=F32)
    mix = mix + jnp.dot(on_ref[...], wn_ref[...], preferred_element_type=F32)
    x1 = _layer_norm(alpha * x_ref[...] + mix, g_ref[...], b_ref[...])
    x1_ref[...] = x1

    logits = jnp.dot(x1.astype(BF16), rw_ref[...], preferred_element_type=F32) + rb_ref[...]
    v = jnp.where(lane < N_EXPERTS, logits, NEG)
    lane_f = lane.astype(F32)
    tops, hots = [], []
    for _ in range(TOP_K):
        mx = jnp.max(v, axis=-1, keepdims=True)
        idx = jnp.min(jnp.where(v == mx, lane_f, float(LANES)), axis=-1, keepdims=True)
        hot = lane_f == idx
        tops.append((mx, idx))
        hots.append(hot)
        v = jnp.where(hot, 2.0 * NEG, v)
    es = [jnp.exp(mx - tops[0][0]) for mx, _ in tops]
    den = es[0] + es[1] + es[2] + es[3]

    member = jnp.zeros((tm, LANES), F32)
    for hot in hots:
        member = member + jnp.where(hot, 1.0, 0.0)
    r = lax.broadcasted_iota(I32, (tm, tm), 0)
    c = lax.broadcasted_iota(I32, (tm, tm), 1)
    strict = jnp.where(c < r, 1.0, 0.0).astype(BF16)
    before = jnp.dot(strict, member.astype(BF16), preferred_element_type=F32) + carry_ref[0:1, :]
    ri = jnp.zeros((tm, LANES), F32)
    rwt = jnp.zeros((tm, LANES), F32)
    for k in range(TOP_K):
        rank = jnp.sum(jnp.where(hots[k], before, 0.0), axis=-1, keepdims=True)
        ri = jnp.where(lane == k, tops[k][1], ri)
        ri = jnp.where(lane == TOP_K + k, rank, ri)
        rwt = jnp.where(lane == k, es[k] / den, rwt)
    ri_ref[...] = ri.astype(I32)
    rwt_ref[...] = rwt
    new = carry_ref[0:1, :] + jnp.sum(member, axis=0, keepdims=True)
    carry_ref[...] = jnp.zeros_like(carry_ref) + new
    cnt_ref[...] = jnp.zeros_like(cnt_ref) + new


def _mix(x2, of, osw, on, wf, ws, wn, g, b, rw, rb, alpha, tm):
    n, d = x2.shape
    full = lambda a: pl.BlockSpec(a.shape, lambda i: (0,) * a.ndim)
    rows = lambda a: pl.BlockSpec((tm, a.shape[1]), lambda i: (i, 0))
    return pl.pallas_call(
        functools.partial(_mix_kernel, alpha=alpha),
        grid=(n // tm,),
        in_specs=[rows(x2), rows(of), rows(osw), rows(on), full(wf), full(ws), full(wn),
                  full(g), full(b), full(rw), full(rb)],
        out_specs=[pl.BlockSpec((tm, d), lambda i: (i, 0)),
                   pl.BlockSpec((tm, LANES), lambda i: (i, 0)),
                   pl.BlockSpec((tm, LANES), lambda i: (i, 0)),
                   pl.BlockSpec((8, LANES), lambda i: (0, 0))],
        out_shape=[jax.ShapeDtypeStruct((n, d), F32), jax.ShapeDtypeStruct((n, LANES), I32),
                   jax.ShapeDtypeStruct((n, LANES), F32), jax.ShapeDtypeStruct((8, LANES), F32)],
        scratch_shapes=[pltpu.VMEM((8, LANES), F32)],
        compiler_params=_params("arbitrary"),
        name="mix_ln_router",
    )(x2, of, osw, on, wf, ws, wn, g, b, rw, rb)


def _dispatch_kernel(pe_ref, pd_ref, nu_ref, dest_ref, x_ref, xs_hbm, zbuf, zsem, sem, *, rows):
    i = pl.program_id(0)
    tokens = x_ref.shape[0]
    nblk = xs_hbm.shape[0] // rows

    @pl.when(i == 0)
    def _zero_fill():
        zbuf[...] = jnp.zeros_like(zbuf)

        def block(start):
            return pltpu.make_async_copy(zbuf, xs_hbm.at[pl.ds(pl.multiple_of(start, rows), rows)], zsem)

        for phase in ("start", "wait"):
            for e in range(N_EXPERTS):
                @pl.when(pd_ref[e] > 0)
                def _(e=e, phase=phase):
                    getattr(block(pe_ref[e] - rows), phase)()

            def tail(j, _, phase=phase):
                getattr(block(j * rows), phase)()
                return 0

            lax.fori_loop(nu_ref[0], nblk, tail, 0)

    for t in range(tokens):
        for k in range(TOP_K):
            c = t * TOP_K + k
            pltpu.make_async_copy(x_ref.at[pl.ds(t, 1)],
                                  xs_hbm.at[pl.ds(dest_ref[c], 1)], sem).start(priority=c % 2)
    for k in range(TOP_K):
        pltpu.make_async_copy(x_ref, xs_hbm.at[pl.ds(0, tokens)], sem).wait()


def _dispatch(pad_end, padded, n_used, dest_flat, x1, rows_total, tokens, rows):
    n, d = x1.shape
    grid_spec = pltpu.PrefetchScalarGridSpec(
        num_scalar_prefetch=3,
        grid=(n // tokens,),
        in_specs=[pl.BlockSpec((tokens * TOP_K,), lambda i, *_: (i,), memory_space=pltpu.SMEM),
                  pl.BlockSpec((tokens, d), lambda i, *_: (i, 0))],
        out_specs=pl.BlockSpec(memory_space=pl.ANY),
        scratch_shapes=[pltpu.VMEM((rows, d), F32), pltpu.SemaphoreType.DMA(()), pltpu.SemaphoreType.DMA(())],
    )
    return pl.pallas_call(
        functools.partial(_dispatch_kernel, rows=rows),
        grid_spec=grid_spec,
        out_shape=jax.ShapeDtypeStruct((rows_total, d), F32),
        compiler_params=_params("arbitrary"),
        name="moe_dispatch",
    )(pad_end, padded, n_used, dest_flat, x1)


def _ffn_kernel(be_ref, nu_ref, x_ref, wg_ref, wu_ref, bg_ref, bu_ref, wd_ref, bd_ref, y_ref):
    i = pl.program_id(0)

    @pl.when(i < nu_ref[0])
    def _run():
        x = x_ref[...].astype(BF16)
        gate = jnp.dot(x, wg_ref[0], preferred_element_type=F32) + bg_ref[0]
        up = jnp.dot(x, wu_ref[0], preferred_element_type=F32) + bu_ref[0]
        gate = jnp.minimum(gate, SWIGLU_LIMIT)
        up = jnp.clip(up, -SWIGLU_LIMIT, SWIGLU_LIMIT)
        glu = gate * jax.nn.sigmoid(gate * SWIGLU_ALPHA)
        act = ((up + 1.0) * glu).astype(BF16)
        y_ref[...] = jnp.dot(act, wd_ref[0], preferred_element_type=F32) + bd_ref[0]

    @pl.when(i >= nu_ref[0])
    def _skip():
        y_ref[...] = jnp.zeros_like(y_ref)


def _ffn(blk_e, n_used, xs, wg, wu, bg, bu, wd, bd, rows):
    p, d = xs.shape
    wspec = lambda a: pl.BlockSpec((1,) + a.shape[1:], lambda i, be, nu: (be[i], 0, 0))
    grid_spec = pltpu.PrefetchScalarGridSpec(
        num_scalar_prefetch=2,
        grid=(p // rows,),
        in_specs=[pl.BlockSpec((rows, d), lambda i, be, nu: (i, 0)),
                  wspec(wg), wspec(wu), wspec(bg), wspec(bu), wspec(wd), wspec(bd)],
        out_specs=pl.BlockSpec((rows, d), lambda i, be, nu: (i, 0)),
    )
    return pl.pallas_call(
        _ffn_kernel,
        grid_spec=grid_spec,
        out_shape=jax.ShapeDtypeStruct((p, d), F32),
        compiler_params=_params("arbitrary"),
        name="moe_ffn",
    )(blk_e, n_used, xs, wg, wu, bg, bu, wd, bd)


def _combine_kernel(destc_ref, destn_ref, x1_ref, w_ref, g_ref, b_ref, y_hbm, o_ref, buf0, buf1, sem, *, alpha):
    i = pl.program_id(0)
    last = pl.num_programs(0) - 1
    tokens = x1_ref.shape[0]
    bufs = (buf0, buf1)

    def issue(dest_ref, dst):
        for t in range(tokens):
            for k in range(TOP_K):
                c = t * TOP_K + k
                pltpu.make_async_copy(y_hbm.at[pl.ds(dest_ref[c], 1)],
                                      bufs[dst].at[k, pl.ds(t, 1)], sem.at[dst]).start(priority=c % 2)

    @pl.when(i == 0)
    def _prologue():
        issue(destc_ref, 0)

    def run(slot):
        @pl.when(i < last)
        def _():
            issue(destn_ref, 1 - slot)

        for k in range(TOP_K):
            pltpu.make_async_copy(y_hbm.at[pl.ds(0, tokens)], bufs[slot].at[k], sem.at[slot]).wait()
        w = w_ref[...]
        ffn = jnp.zeros(x1_ref.shape, F32)
        for k in range(TOP_K):
            ffn = ffn + bufs[slot][k] * w[:, k:k + 1]
        o_ref[...] = _layer_norm(alpha * x1_ref[...] + ffn, g_ref[...], b_ref[...])

    for slot in range(2):
        pl.when(i % 2 == slot)(functools.partial(run, slot))


def _combine(dest_flat, x1, rwt, g, b, y, alpha, tokens):
    n, d = x1.shape
    steps = n // tokens
    full = lambda a: pl.BlockSpec(a.shape, lambda i: (0,) * a.ndim)
    return pl.pallas_call(
        functools.partial(_combine_kernel, alpha=alpha),
        grid=(steps,),
        in_specs=[pl.BlockSpec((tokens * TOP_K,), lambda i: (i,), memory_space=pltpu.SMEM),
                  pl.BlockSpec((tokens * TOP_K,), lambda i: (jnp.minimum(i + 1, steps - 1),),
                               memory_space=pltpu.SMEM),
                  pl.BlockSpec((tokens, d), lambda i: (i, 0)),
                  pl.BlockSpec((tokens, LANES), lambda i: (i, 0)),
                  full(g), full(b),
                  pl.BlockSpec(memory_space=pl.ANY)],
        out_specs=pl.BlockSpec((tokens, d), lambda i: (i, 0)),
        out_shape=jax.ShapeDtypeStruct((n, d), F32),
        scratch_shapes=[pltpu.VMEM((TOP_K, tokens, d), F32), pltpu.VMEM((TOP_K, tokens, d), F32),
                        pltpu.SemaphoreType.DMA((2,))],
        compiler_params=_params("arbitrary"),
        name="moe_combine_ln",
    )(dest_flat, dest_flat, x1, rwt, g, b, y)


def _prep_gate_up_kernel(w_ref, p_ref, g_ref, u_ref):
    groups = w_ref.shape[3] // (2 * LANES)
    for c in range(groups):
        blk = w_ref[0, 0, :, c * 2 * LANES:(c + 1) * 2 * LANES].astype(BF16)
        o = jnp.dot(blk, p_ref[...], preferred_element_type=F32)
        g_ref[0, :, c * LANES:(c + 1) * LANES] = o[:, :LANES].astype(BF16)
        u_ref[0, :, c * LANES:(c + 1) * LANES] = o[:, LANES:].astype(BF16)


def _prep_gate_up(w_gu_all, layer):
    _, e, d, f2 = w_gu_all.shape
    perm = np.zeros((2 * LANES, 2 * LANES), np.float32)
    perm[2 * np.arange(LANES), np.arange(LANES)] = 1.0
    perm[2 * np.arange(LANES) + 1, LANES + np.arange(LANES)] = 1.0
    return pl.pallas_call(
        _prep_gate_up_kernel,
        grid=(e,),
        in_specs=[pl.BlockSpec((1, 1, d, f2), lambda i: (layer, i, 0, 0)),
                  pl.BlockSpec((2 * LANES, 2 * LANES), lambda i: (0, 0))],
        out_specs=[pl.BlockSpec((1, d, f2 // 2), lambda i: (i, 0, 0)),
                   pl.BlockSpec((1, d, f2 // 2), lambda i: (i, 0, 0))],
        out_shape=[jax.ShapeDtypeStruct((e, d, f2 // 2), BF16), jax.ShapeDtypeStruct((e, d, f2 // 2), BF16)],
        compiler_params=_params("arbitrary"),
        name="prep_gate_up",
    )(w_gu_all, jnp.asarray(perm, BF16))


def _gqa_cols(base):
    idx = np.empty(GQA_GROUP * KV_HEADS * HEAD_DIM, np.int64)
    for g in range(GQA_GROUP):
        for kv in range(KV_HEADS):
            dst = (g * KV_HEADS + kv) * HEAD_DIM
            src = (kv * GQA_GROUP + g) * HEAD_DIM
            idx[dst:dst + HEAD_DIM] = base + src + np.arange(HEAD_DIM)
    return idx


def _inproj_layout():
    fw = FOX_HEADS * HEAD_DIM
    qw = SWA_Q_HEADS * HEAD_DIM
    kw = KV_HEADS * HEAD_DIM
    sizes = (fw, fw, fw, FOX_HEADS, qw, kw, kw, qw, kw, kw, kw, kw, kw, kw, 3 * NSA_Q_HEADS)
    off = np.concatenate([[0], np.cumsum(sizes)])
    (o_fq, o_fk, o_fv, o_ff, o_sq, o_sk, o_sv, o_nq, o_nkc, o_nvc, o_nks, o_nvs, o_nkw, o_nvw, o_ng, total) = off
    ar = np.arange
    cols = np.concatenate([
        _gqa_cols(o_sq), _gqa_cols(o_nq), o_fq + ar(fw), o_fk + ar(fw), o_fv + ar(fw),
        o_sk + ar(kw), o_sv + ar(kw), o_nks + ar(kw), o_nvs + ar(kw), o_nkw + ar(kw), o_nvw + ar(kw),
        o_nkc + ar(kw), o_nvc + ar(kw)])
    gate = np.full(LANES, total, np.int64)
    gate[:FOX_HEADS] = o_ff + ar(FOX_HEADS)
    for r in range(3):
        for g in range(GQA_GROUP):
            for kv in range(KV_HEADS):
                gate[FOX_HEADS + (r * GQA_GROUP + g) * KV_HEADS + kv] = o_ng + (kv * GQA_GROUP + g) * 3 + r
    cols = np.concatenate([cols, gate])
    scale = np.ones(cols.shape[0], np.float32)
    q_scale = np.float32(HEAD_DIM ** -0.5 * LOG2E)
    scale[C_SQ:C_SQ + qw] = q_scale
    scale[C_NQ:C_NQ + qw] = q_scale
    scale[C_FQ:C_FQ + fw] = q_scale
    return cols, scale


def _outproj_rows():
    fw = FOX_HEADS * HEAD_DIM
    qw = SWA_Q_HEADS * HEAD_DIM
    return np.arange(fw), _gqa_cols(fw), _gqa_cols(fw + qw)


def _overlap_matrix_t(s_len, nc):
    ns = s_len // SEL_BLOCK
    cmp_start = np.arange(nc) * CMP_STRIDE
    sel_start = np.arange(ns) * SEL_BLOCK
    ov = np.clip(np.minimum(cmp_start[:, None] + CMP_BLOCK, sel_start[None, :] + SEL_BLOCK)
                 - np.maximum(cmp_start[:, None], sel_start[None, :]), 0, None) / CMP_BLOCK
    ov[(s_len - CMP_BLOCK) // CMP_STRIDE + 1:] = 0.0
    out = np.zeros((KV_HEADS, LANES, nc), np.float32)
    for kv in range(KV_HEADS):
        r0 = _spare(kv) + L_BLK
        out[kv, r0:r0 + ns, :] = ov.T
    return out


def _compress_weights(pe, w1, w2):
    pe2 = jnp.concatenate([pe, pe], axis=-1)
    w1r = w1.reshape(2, CMP_BLOCK, HEAD_DIM, HEAD_DIM)
    z = jnp.zeros_like(w1r)
    w1b = jnp.concatenate([jnp.concatenate([w1r, z], axis=-1), jnp.concatenate([z, w1r], axis=-1)], axis=-2)
    z2 = jnp.zeros_like(w2)
    w2b = jnp.concatenate([jnp.concatenate([w2, z2], axis=-1), jnp.concatenate([z2, w2], axis=-1)], axis=-2)
    return pe2, w1b.astype(BF16), w2b.astype(BF16)


def kernel(x, w_in, b_in, sinks, cmp_pe, cmp_w1, cmp_w2, w_out, ln1_g, ln1_b, router_w, router_b,
           w_gate_up, b_gate_up, w_down, b_down, ln2_g, ln2_b):
    depth = w_in.shape[0]
    bsz, s_len, d = x.shape
    n = bsz * s_len
    nk = n * TOP_K
    alpha = float((2.0 * depth) ** 0.25)
    assert s_len % SEL_KEYS == 0 and s_len // SEL_BLOCK <= MAX_SEL_BLOCKS and n % 512 == 0
    slopes_swa, slopes_nsa = _alibi()
    cols, scale = _inproj_layout()
    rows_f, rows_s, rows_n = _outproj_rows()
    nc = s_len // CMP_STRIDE
    ovt = jnp.asarray(_overlap_matrix_t(s_len, nc), BF16)
    nblk = -(-nk // MOE_ROWS) + N_EXPERTS
    p_rows = nblk * MOE_ROWS
    f = w_down.shape[2]

    x2 = x.reshape(n, d)
    for l in range(depth):
        w_ext = jnp.concatenate([w_in[l], jnp.zeros((d, 1), F32)], axis=1)
        b_ext = jnp.concatenate([b_in[l], jnp.zeros((1,), F32)])
        w_l = (jnp.take(w_ext, cols, axis=1) * scale).astype(BF16)
        b_l = (jnp.take(b_ext, cols) * scale).reshape(1, -1)
        hb, hf = _inproj(x2, w_l, b_l, 512)
        hb3 = hb.reshape(bsz, s_len, CB)
        hf3 = hf.reshape(bsz, s_len, CF)
        cg = _gates(hf3)
        o_fox = _fox(hb3, cg)
        o_swa = _swa(hb3, sinks[l] * LOG2E, slopes_swa)
        grp = hf3[:, :, :2 * LANES].reshape(bsz, nc, CMP_STRIDE * 2 * LANES)
        pe2, w1b, w2b = _compress_weights(cmp_pe[l], cmp_w1[l], cmp_w2[l])
        kc, vc = _compress(grp, pe2, w1b, w2b)
        o_nsa = _nsa(hb3, kc, vc, cg, ovt, slopes_nsa)

        wo = w_out[l].astype(BF16)
        rw = jnp.concatenate([router_w[l], jnp.zeros((d, LANES - N_EXPERTS), F32)], axis=1).astype(BF16)
        rb = jnp.concatenate([router_b[l], jnp.zeros((LANES - N_EXPERTS,), F32)]).reshape(1, LANES)
        x1, ri, rwt, cnt = _mix(
            x2, o_fox.reshape(n, -1), o_swa.reshape(n, -1), o_nsa.reshape(n, -1),
            jnp.take(wo, rows_f, axis=0), jnp.take(wo, rows_s, axis=0), jnp.take(wo, rows_n, axis=0),
            ln1_g[l].reshape(1, d), ln1_b[l].reshape(1, d), rw, rb, alpha, 512)

        counts = cnt[0, :N_EXPERTS].astype(I32)
        padded = (counts + MOE_ROWS - 1) // MOE_ROWS * MOE_ROWS
        pad_end = jnp.cumsum(padded)
        pad_start = pad_end - padded
        dest = (jnp.take(pad_start, ri[:, :TOP_K]) + ri[:, TOP_K:2 * TOP_K]).reshape(nk)
        blk_start = jnp.arange(nblk, dtype=I32) * MOE_ROWS
        blk_e = jnp.minimum(jnp.sum((pad_end[None, :] <= blk_start[:, None]).astype(I32), axis=1),
                            N_EXPERTS - 1)
        n_used = (pad_end[-1:] // MOE_ROWS).astype(I32)

        xs = _dispatch(pad_end, padded, n_used, dest, x1, p_rows, 256, MOE_ROWS)
        w_gate, w_up = _prep_gate_up(w_gate_up, l)
        bgu = b_gate_up[l].reshape(N_EXPERTS, 1, f, 2)
        y = _ffn(blk_e, n_used, xs, w_gate, w_up,
                 bgu[..., 0], bgu[..., 1], w_down[l].astype(BF16), b_down[l].reshape(N_EXPERTS, 1, d),
                 MOE_ROWS)
        x2 = _combine(dest, x1, rwt, ln2_g[l].reshape(1, d), ln2_b[l].reshape(1, d), y, alpha, 128)
    return x2.reshape(bsz, s_len, d)
```

```python
import functools

import numpy as np
import jax
import jax.numpy as jnp
from jax import lax
from jax.experimental import pallas as pl
from jax.experimental.pallas import tpu as pltpu

F32 = jnp.float32
BF16 = jnp.bfloat16
I32 = jnp.int32

HEAD_DIM = 64
LANES = 128
N_HEADS = 16
SWA_Q_HEADS = 6
NSA_Q_HEADS = 6
FOX_HEADS = 4
KV_HEADS = 2
GQA_GROUP = 3
SWA_WINDOW = 128
CMP_BLOCK = 32
CMP_STRIDE = 16
SEL_BLOCK = 64
SEL_TOP_N = 8
NSA_WINDOW = 512
SEL_FORCE = 1.0e4
N_EXPERTS = 32
TOP_K = 4
SWIGLU_LIMIT = 7.0
SWIGLU_ALPHA = 1.702
LN_EPS = 1e-5
NEG = -1.0e30

Q_TILE = 256
SWA_TILE = 256
FOX_TILE = 512
FOX_KEYS = 512
SEL_KEYS = 512
MAX_SEL_BLOCKS = 32
MOE_ROWS = 512
VMEM_LIMIT = 56 * 1024 * 1024

L_BLK = 0
L_QT = 32
L_KH = 35
L_KL = 38
L_PAD = 41
L_QB = 42
LOG2E = 1.4426950408889634

C_SQ, C_NQ, C_FQ, C_FK, C_FV = 0, 384, 768, 1024, 1280
C_SK, C_SV, C_NKS, C_NVS, C_NKW, C_NVW = 1536, 1664, 1792, 1920, 2048, 2176
CB = 2304
CF = 384


def _alibi():
    n = SWA_Q_HEADS + NSA_Q_HEADS
    s = (2.0 ** (-8.0 * np.arange(1, n + 1) / n)).astype(np.float32) * np.float32(LOG2E)
    return s[0::2].reshape(KV_HEADS, GQA_GROUP), s[1::2].reshape(KV_HEADS, GQA_GROUP)


def _query_tables(slopes, tq):
    out = np.zeros((KV_HEADS, GQA_GROUP, tq, LANES), np.float32)
    r = np.arange(tq, dtype=np.float32)
    for kv in range(KV_HEADS):
        a0 = _spare(kv)
        for g in range(GQA_GROUP):
            v = (r * -slopes[kv][g]).astype(np.float32)
            for c in range(3):
                p = v.astype(BF16).astype(np.float32)
                out[kv, g, :, a0 + L_QT + c] = p
                v = (v - p).astype(np.float32)
            for c, part in enumerate(_bf16_parts(slopes[kv][g])):
                out[kv, g, :, a0 + L_KH + c] = part
                out[kv, g, :, a0 + L_KL + c] = part
    return out


def _bf16_parts(v):
    v = np.float32(v)
    out = []
    for _ in range(3):
        p = np.float32(np.asarray(v, dtype=BF16))
        out.append(float(p))
        v = np.float32(v - p)
    return out


def _split3(x):
    hi = x.astype(BF16).astype(F32)
    r = x - hi
    mid = r.astype(BF16).astype(F32)
    lo = (r - mid).astype(BF16).astype(F32)
    return hi, mid, lo


def _lane_iota():
    return lax.broadcasted_iota(I32, (1, LANES), 1)


def _row_iota():
    return lax.broadcasted_iota(I32, (LANES, 1), 0)


def _spare(kv_half):
    return HEAD_DIM if kv_half == 0 else 0


def _dot_nt(a, b):
    return lax.dot_general(a, b, (((1,), (1,)), ((), ())), preferred_element_type=F32)


def _transposed(x):
    return x.astype(F32).T.astype(BF16)


def _params(*sem):
    return pltpu.CompilerParams(dimension_semantics=sem, vmem_limit_bytes=VMEM_LIMIT)


def _inproj_kernel(x_ref, w_ref, b_ref, hb_ref, hf_ref):
    acc = jnp.dot(x_ref[...].astype(BF16), w_ref[...], preferred_element_type=F32) + b_ref[...]
    hb_ref[...] = acc[:, :CB].astype(BF16)
    hf_ref[...] = acc[:, CB:]


def _inproj(x2, w, b, tm):
    n, d = x2.shape
    ct = w.shape[1]
    return pl.pallas_call(
        _inproj_kernel,
        grid=(n // tm,),
        in_specs=[pl.BlockSpec((tm, d), lambda i: (i, 0)),
                  pl.BlockSpec((d, ct), lambda i: (0, 0)),
                  pl.BlockSpec((1, ct), lambda i: (0, 0))],
        out_specs=[pl.BlockSpec((tm, CB), lambda i: (i, 0)),
                   pl.BlockSpec((tm, CF), lambda i: (i, 0))],
        out_shape=[jax.ShapeDtypeStruct((n, CB), BF16), jax.ShapeDtypeStruct((n, CF), F32)],
        compiler_params=_params("arbitrary"),
        name="inproj",
    )(x2, w, b)


def _gates_kernel(g_ref, o_ref, *, chunk):
    s = g_ref.shape[1]
    lane = _lane_iota()
    r = lax.broadcasted_iota(I32, (chunk, chunk), 0)
    c = lax.broadcasted_iota(I32, (chunk, chunk), 1)
    tri = jnp.where(c <= r, 1.0, 0.0).astype(BF16)

    def body(j, carry):
        x = g_ref[0, pl.ds(pl.multiple_of(j * chunk, chunk), chunk), :]
        ls = (jnp.minimum(x, 0.0) - jnp.log1p(jnp.exp(-jnp.abs(x)))) * LOG2E
        cs = carry
        for part in _split3(ls):
            cs = cs + jnp.dot(tri, part.astype(BF16), preferred_element_type=F32)
        sg = jax.nn.sigmoid(x)
        o_ref[0, pl.ds(pl.multiple_of(j * chunk, chunk), chunk), :] = jnp.where(lane < FOX_HEADS, cs, sg)
        return cs[chunk - 1:chunk, :]

    lax.fori_loop(0, s // chunk, body, jnp.zeros((1, LANES), F32))


def _gates(hf3):
    b, s, _ = hf3.shape
    chunk = min(256, s)
    return pl.pallas_call(
        functools.partial(_gates_kernel, chunk=chunk),
        grid=(b,),
        in_specs=[pl.BlockSpec((1, s, LANES), lambda i: (i, 0, 2))],
        out_specs=pl.BlockSpec((1, s, LANES), lambda i: (i, 0, 0)),
        out_shape=jax.ShapeDtypeStruct((b, s, LANES), F32),
        compiler_params=_params("arbitrary"),
        name="gates",
    )(hf3)


def _softmax_steps(carries, scores, values):
    stats = []
    for (m, l, _), sc in zip(carries, scores):
        m_new = jnp.maximum(m, jnp.max(sc, axis=0, keepdims=True))
        alpha = jnp.exp2(m - m_new)
        pr = jnp.exp2(sc - m_new)
        stats.append((m_new, alpha * l + jnp.sum(pr, axis=0, keepdims=True), alpha, pr.astype(BF16)))
    return tuple((m, l, alpha * acc + jnp.dot(vt, pr, preferred_element_type=F32))
                 for (m, l, alpha, pr), (_, _, acc), vt in zip(stats, carries, values))


def _softmax_init(queries):
    return (jnp.full((1, queries), NEG, F32), jnp.zeros((1, queries), F32), jnp.zeros((LANES, queries), F32))


def _fox_kernel(q_ref, k_ref, v_ref, cq_ref, ck_ref, o_ref, ka_ref, vt_ref, *, tile, keys):
    i = pl.program_id(1)
    s_len = k_ref.shape[1]
    lane = _lane_iota()
    npair = FOX_HEADS // 2

    @pl.when(i == 0)
    def _fill():
        for h in range(FOX_HEADS):
            p, half = divmod(h, 2)
            a0 = _spare(half)
            k = k_ref[0, :, p * LANES:(p + 1) * LANES].astype(F32)
            hi, mid, lo = _split3(-ck_ref[0, :, h:h + 1])
            ka = jnp.where((lane >= a0 + L_QT) & (lane < a0 + L_QT + 3), 1.0, k)
            ka = jnp.where(lane == a0 + L_KH, hi, ka)
            ka = jnp.where(lane == a0 + L_KH + 1, mid, ka)
            ka = jnp.where(lane == a0 + L_KH + 2, lo, ka)
            ka_ref[h] = ka.astype(BF16)
        for p in range(npair):
            for c in range(s_len // keys):
                vt_ref[p, c] = _transposed(v_ref[0, c * keys:(c + 1) * keys, p * LANES:(p + 1) * LANES])

    qas = []
    for h in range(FOX_HEADS):
        p, half = divmod(h, 2)
        a0 = _spare(half)
        q = q_ref[0, :, p * LANES:(p + 1) * LANES].astype(F32)
        in_half = (lane < HEAD_DIM) if half == 0 else (lane >= HEAD_DIM)
        hi, mid, lo = _split3(cq_ref[0, :, h:h + 1])
        qa = jnp.where(in_half, q, 0.0)
        qa = jnp.where(lane == a0 + L_QT, hi, qa)
        qa = jnp.where(lane == a0 + L_QT + 1, mid, qa)
        qa = jnp.where(lane == a0 + L_QT + 2, lo, qa)
        qas.append(jnp.where((lane >= a0 + L_KH) & (lane < a0 + L_KH + 3), 1.0, qa).astype(BF16))

    nfull = (i * tile) // keys
    key = lax.broadcasted_iota(I32, (keys, tile), 0) + nfull * keys
    qry = lax.broadcasted_iota(I32, (keys, tile), 1) + i * tile
    causal = jnp.where(key <= qry, 0.0, NEG)

    def step(j, carry, masked):
        start = pl.multiple_of(j * keys, keys)
        scores = [_dot_nt(ka_ref[h, pl.ds(start, keys), :], qas[h]) for h in range(FOX_HEADS)]
        if masked:
            scores = [sc + causal for sc in scores]
        return _softmax_steps(carry, scores, [vt_ref[h // 2, j] for h in range(FOX_HEADS)])

    init = tuple(_softmax_init(tile) for _ in range(FOX_HEADS))
    carry = lax.fori_loop(0, nfull, functools.partial(step, masked=False), init)
    carry = step(nfull, carry, True)
    row = _row_iota()
    for p in range(npair):
        (_, l0, a0_), (_, l1, a1_) = carry[2 * p], carry[2 * p + 1]
        o_ref[0, :, p * LANES:(p + 1) * LANES] = jnp.where(row < HEAD_DIM, a0_ / l0, a1_ / l1).T.astype(BF16)


def _fox(hb3, cg):
    b, s, _ = hb3.shape
    tile = min(FOX_TILE, s)
    keys = min(FOX_KEYS, s)
    assert keys % tile == 0 and s % keys == 0
    w = FOX_HEADS * HEAD_DIM
    return pl.pallas_call(
        functools.partial(_fox_kernel, tile=tile, keys=keys),
        grid=(b, s // tile),
        in_specs=[pl.BlockSpec((1, tile, w), lambda bi, i: (bi, i, C_FQ // w)),
                  pl.BlockSpec((1, s, w), lambda bi, i: (bi, 0, C_FK // w)),
                  pl.BlockSpec((1, s, w), lambda bi, i: (bi, 0, C_FV // w)),
                  pl.BlockSpec((1, tile, LANES), lambda bi, i: (bi, i, 0)),
                  pl.BlockSpec((1, s, LANES), lambda bi, i: (bi, 0, 0))],
        out_specs=pl.BlockSpec((1, tile, w), lambda bi, i: (bi, i, 0)),
        out_shape=jax.ShapeDtypeStruct((b, s, w), BF16),
        scratch_shapes=[pltpu.VMEM((FOX_HEADS, s, LANES), BF16),
                        pltpu.VMEM((FOX_HEADS // 2, s // keys, LANES, keys), BF16)],
        compiler_params=_params("arbitrary", "arbitrary"),
        name="fox",
    )(hb3, hb3, hb3, cg, cg)


def _fill_banded(k_ref, v_ref, kp_ref, vt_ref, window):
    s_len = k_ref.shape[1]
    lane = _lane_iota()
    k = k_ref[0].astype(F32)
    pos = lax.broadcasted_iota(I32, (s_len, 1), 0) + window
    pos_hi = ((pos >> 8) << 8).astype(F32)
    pos_lo = (pos & 255).astype(F32)
    for kv in range(KV_HEADS):
        a0 = _spare(kv)
        ka = jnp.where((lane >= a0 + L_QT) & (lane < a0 + L_QT + 3), 1.0, k)
        ka = jnp.where((lane >= a0 + L_QB) & (lane < a0 + L_QB + 3), 1.0, ka)
        ka = jnp.where((lane >= a0 + L_KH) & (lane < a0 + L_KH + 3), pos_hi, ka)
        ka = jnp.where((lane >= a0 + L_KL) & (lane < a0 + L_KL + 3), pos_lo, ka)
        ka = jnp.where(lane == a0 + L_PAD, 0.0, ka)
        kp_ref[kv, pl.ds(window, s_len), :] = ka.astype(BF16)
        pad = jnp.where(lane == a0 + L_PAD, NEG, 0.0) + jnp.zeros((window, LANES), F32)
        kp_ref[kv, pl.ds(0, window), :] = pad.astype(BF16)
    npad = window // LANES
    for c in range(npad):
        vt_ref[c] = jnp.zeros((LANES, LANES), BF16)
    for c in range(s_len // LANES):
        vt_ref[npad + c] = _transposed(v_ref[0, c * LANES:(c + 1) * LANES, :])


def _alibi_query(q, table, kv, t0, slope, lane, pad_lane):
    a0 = _spare(kv)
    in_half = (lane < HEAD_DIM) if kv == 0 else (lane >= HEAD_DIM)
    hi, mid, lo = _split3(t0 * (-float(slope)))
    qa = jnp.where(in_half, q.astype(F32), table)
    qa = jnp.where(lane == a0 + L_QB, hi, qa)
    qa = jnp.where(lane == a0 + L_QB + 1, mid, qa)
    qa = jnp.where(lane == a0 + L_QB + 2, lo, qa)
    if pad_lane:
        qa = jnp.where(lane == a0 + L_PAD, 1.0, qa)
    return qa


def _banded_attend(q, qt_ref, kp_ref, vt_ref, i, window, slopes, sink_ref):
    tq = q.shape[0]
    span = window + tq
    lane = _lane_iota()
    start = pl.multiple_of(i * tq, tq)
    t0 = jnp.full((1, LANES), i * tq + window, I32).astype(F32)
    key = lax.broadcasted_iota(I32, (span, tq), 0)
    qry = lax.broadcasted_iota(I32, (span, tq), 1)
    band = jnp.where((key > qry) & (key <= qry + window), 0.0, NEG)
    band = jnp.concatenate([band] * GQA_GROUP, axis=1)
    outs = []
    for kv in range(KV_HEADS):
        qst = jnp.concatenate(
            [_alibi_query(q[:, g * LANES:(g + 1) * LANES], qt_ref[kv, g], kv, t0, slopes[kv][g], lane,
                          True).astype(BF16)
             for g in range(GQA_GROUP)], axis=0)
        sc = _dot_nt(kp_ref[kv, pl.ds(start, span), :], qst) + band
        m = jnp.max(sc, axis=0, keepdims=True)
        if sink_ref is not None:
            sink = jnp.concatenate([jnp.full((1, tq), sink_ref[kv * GQA_GROUP + g], F32)
                                    for g in range(GQA_GROUP)], axis=1)
            m = jnp.maximum(m, sink)
        pr = jnp.exp2(sc - m)
        den = jnp.sum(pr, axis=0, keepdims=True)
        if sink_ref is not None:
            den = den + jnp.exp2(sink - m)
        pr = pr.astype(BF16)
        o = jnp.zeros((LANES, GQA_GROUP * tq), F32)
        base = i * (tq // LANES)
        for c in range(0, span // LANES, 2):
            wide = min(2, span // LANES - c)
            vt = jnp.concatenate([vt_ref[base + c + j] for j in range(wide)], axis=1)
            o = o + jnp.dot(vt, pr[c * LANES:(c + wide) * LANES, :], preferred_element_type=F32)
        outs.append(o / den)
    return outs


def _merge_kv(outs, g, tq, row):
    return jnp.where(row < HEAD_DIM, outs[0][:, g * tq:(g + 1) * tq], outs[1][:, g * tq:(g + 1) * tq])


def _swa_kernel(sink_ref, q_ref, k_ref, v_ref, qt_ref, o_ref, kp_ref, vt_ref, *, slopes):
    i = pl.program_id(1)
    tq = q_ref.shape[1]

    @pl.when(i == 0)
    def _fill():
        _fill_banded(k_ref, v_ref, kp_ref, vt_ref, SWA_WINDOW)

    outs = _banded_attend(q_ref[0], qt_ref, kp_ref, vt_ref, i, SWA_WINDOW, slopes, sink_ref)
    row = _row_iota()
    for g in range(GQA_GROUP):
        o_ref[0, :, g * LANES:(g + 1) * LANES] = _merge_kv(outs, g, tq, row).T.astype(BF16)


def _swa(hb3, sinks, slopes):
    b, s, _ = hb3.shape
    w = GQA_GROUP * LANES
    tq = min(SWA_TILE, s)
    qt = jnp.asarray(_query_tables(slopes, tq))
    return pl.pallas_call(
        functools.partial(_swa_kernel, slopes=slopes),
        grid=(b, s // tq),
        in_specs=[pl.BlockSpec(memory_space=pltpu.SMEM),
                  pl.BlockSpec((1, tq, w), lambda bi, i: (bi, i, C_SQ // w)),
                  pl.BlockSpec((1, s, LANES), lambda bi, i: (bi, 0, C_SK // LANES)),
                  pl.BlockSpec((1, s, LANES), lambda bi, i: (bi, 0, C_SV // LANES)),
                  pl.BlockSpec(qt.shape, lambda bi, i: (0, 0, 0, 0))],
        out_specs=pl.BlockSpec((1, tq, w), lambda bi, i: (bi, i, 0)),
        out_shape=jax.ShapeDtypeStruct((b, s, w), BF16),
        scratch_shapes=[pltpu.VMEM((KV_HEADS, s + SWA_WINDOW, LANES), BF16),
                        pltpu.VMEM(((s + SWA_WINDOW) // LANES, LANES, LANES), BF16)],
        compiler_params=_params("arbitrary", "arbitrary"),
        name="swa",
    )(sinks, hb3, hb3, hb3, qt)


def _compress_kernel(g_ref, pe_ref, w1_ref, w2_ref, kc_ref, vc_ref):
    nc = g_ref.shape[1]
    for which, out_ref in ((0, kc_ref), (1, vc_ref)):
        ha = jnp.zeros((nc, LANES), F32)
        hb = jnp.zeros((nc, LANES), F32)
        for l in range(CMP_STRIDE):
            x = g_ref[0, :, (2 * l + which) * LANES:(2 * l + which + 1) * LANES]
            xa = (x + pe_ref[which, l:l + 1, :]).astype(BF16)
            xb = (x + pe_ref[which, CMP_STRIDE + l:CMP_STRIDE + l + 1, :]).astype(BF16)
            ha = ha + jnp.dot(xa, w1_ref[which, l], preferred_element_type=F32)
            hb = hb + jnp.dot(xb, w1_ref[which, CMP_STRIDE + l], preferred_element_type=F32)
        hid = ha + pltpu.roll(hb, nc - 1, 0)
        act = 0.5 * hid * (1.0 + jnp.tanh(np.sqrt(2.0 / np.pi).astype(np.float32) * (hid + 0.044715 * (hid * hid * hid))))
        out_ref[0] = jnp.dot(act.astype(BF16), w2_ref[which], preferred_element_type=F32).astype(BF16)


def _compress(g, pe2, w1b, w2b):
    b, nc, gw = g.shape
    return pl.pallas_call(
        _compress_kernel,
        grid=(b,),
        in_specs=[pl.BlockSpec((1, nc, gw), lambda i: (i, 0, 0)),
                  pl.BlockSpec(pe2.shape, lambda i: (0, 0, 0)),
                  pl.BlockSpec(w1b.shape, lambda i: (0, 0, 0, 0)),
                  pl.BlockSpec(w2b.shape, lambda i: (0, 0, 0))],
        out_specs=[pl.BlockSpec((1, nc, LANES), lambda i: (i, 0, 0)),
                   pl.BlockSpec((1, nc, LANES), lambda i: (i, 0, 0))],
        out_shape=[jax.ShapeDtypeStruct((b, nc, LANES), BF16), jax.ShapeDtypeStruct((b, nc, LANES), BF16)],
        compiler_params=_params("arbitrary"),
        name="compress",
    )(g, pe2, w1b, w2b)


def _nsa_kernel(q_ref, ks_ref, vs_ref, kw_ref, vw_ref, kc_ref, vc_ref, cg_ref, ovt_ref, qt_ref, o_ref,
                ksa_ref, vst_ref, kwp_ref, vwt_ref, kca_ref, vct_ref, *, slopes, n_sel):
    i = pl.program_id(1)
    tq = q_ref.shape[1]
    s_len = ks_ref.shape[1]
    nc = kc_ref.shape[1]
    ns = s_len // SEL_BLOCK
    lane = _lane_iota()
    row = _row_iota()

    @pl.when(i == 0)
    def _fill():
        _fill_banded(kw_ref, vw_ref, kwp_ref, vwt_ref, NSA_WINDOW)
        ks = ks_ref[0].astype(F32)
        pos = lax.broadcasted_iota(I32, (s_len, 1), 0)
        pos_hi = ((pos >> 8) << 8).astype(F32)
        pos_lo = (pos & 255).astype(F32)
        blk = pos // SEL_BLOCK
        kc = kc_ref[0].astype(F32)
        cend = lax.broadcasted_iota(I32, (nc, 1), 0) * CMP_STRIDE + (CMP_BLOCK - 1)
        cend_hi = ((cend >> 8) << 8).astype(F32)
        cend_lo = (cend & 255).astype(F32)
        for kv in range(KV_HEADS):
            a0 = _spare(kv)
            ka = jnp.where((lane >= a0 + L_BLK) & (lane < a0 + L_BLK + MAX_SEL_BLOCKS),
                           jnp.where(lane - (a0 + L_BLK) == blk, 1.0, 0.0), ks)
            ka = jnp.where((lane >= a0 + L_QT) & (lane < a0 + L_QT + 3), 1.0, ka)
            ka = jnp.where((lane >= a0 + L_QB) & (lane < a0 + L_QB + 3), 1.0, ka)
            ka = jnp.where((lane >= a0 + L_KH) & (lane < a0 + L_KH + 3), pos_hi, ka)
            ka = jnp.where((lane >= a0 + L_KL) & (lane < a0 + L_KL + 3), pos_lo, ka)
            ksa_ref[kv] = ka.astype(BF16)
            ca = jnp.where((lane >= a0 + L_QT) & (lane < a0 + L_QT + 3), 1.0, kc)
            ca = jnp.where((lane >= a0 + L_QB) & (lane < a0 + L_QB + 3), 1.0, ca)
            ca = jnp.where((lane >= a0 + L_KH) & (lane < a0 + L_KH + 3), cend_hi, ca)
            ca = jnp.where((lane >= a0 + L_KL) & (lane < a0 + L_KL + 3), cend_lo, ca)
            kca_ref[kv] = ca.astype(BF16)
        for c in range(s_len // SEL_KEYS):
            vst_ref[c] = _transposed(vs_ref[0, c * SEL_KEYS:(c + 1) * SEL_KEYS, :])
        vct_ref[...] = _transposed(vc_ref[0])

    q = q_ref[0]
    t0 = jnp.full((1, LANES), i * tq, I32).astype(F32)
    t_row = lax.broadcasted_iota(I32, (1, tq), 1) + i * tq
    cur = t_row // SEL_BLOCK
    cend = lax.broadcasted_iota(I32, (nc, tq), 0) * CMP_STRIDE + (CMP_BLOCK - 1)
    cvalid = jnp.concatenate([cend <= t_row] * GQA_GROUP, axis=1)
    nfull = (i * tq) // SEL_KEYS
    dkey = lax.broadcasted_iota(I32, (SEL_KEYS, tq), 0) + nfull * SEL_KEYS
    dcausal = jnp.where(dkey <= t_row, 0.0, NEG)
    dcausal = jnp.concatenate([dcausal] * GQA_GROUP, axis=1)

    o_cmp, qsels = [], []
    for kv in range(KV_HEADS):
        a0 = _spare(kv)
        qbase = [_alibi_query(q[:, g * LANES:(g + 1) * LANES], qt_ref[kv, g], kv, t0, slopes[kv][g], lane, False)
                 for g in range(GQA_GROUP)]

        qst = jnp.concatenate([x.astype(BF16) for x in qbase], axis=0)
        sc = jnp.where(cvalid, _dot_nt(kca_ref[kv], qst), NEG)
        m = jnp.max(sc, axis=0, keepdims=True)
        m = jnp.where(m > 0.5 * NEG, m, 0.0)
        e = jnp.where(cvalid, jnp.exp2(sc - m), 0.0)
        den = jnp.sum(e, axis=0, keepdims=True)
        pc = (e / jnp.where(den > 0.0, den, 1.0)).astype(BF16)
        o_cmp.append(jnp.dot(vct_ref[...], pc, preferred_element_type=F32))

        p_slc = jnp.zeros((LANES, tq), F32)
        for g in range(GQA_GROUP):
            p_slc = p_slc + jnp.dot(ovt_ref[kv], pc[:, g * tq:(g + 1) * tq], preferred_element_type=F32)
        r0 = a0 + L_BLK
        p_slc = p_slc[r0:r0 + MAX_SEL_BLOCKS, :]
        jb = lax.broadcasted_iota(I32, (MAX_SEL_BLOCKS, 1), 0)
        forced = (jb == 0) | (jb == cur) | (jb == cur - 1)
        score = jnp.where(jb > cur, -1.0, jnp.where(forced, SEL_FORCE, p_slc))
        score = jnp.where(jb < ns, score, -2.0)
        rank = jnp.zeros((MAX_SEL_BLOCKS, tq), F32)
        for b2 in range(ns):
            other = score[b2:b2 + 1, :]
            beats = (other > score) | ((other == score) & (jb > b2))
            rank = rank + jnp.where(beats, 1.0, 0.0)
        selq = jnp.where(rank < float(n_sel), 0.0, NEG)
        pieces = [selq]
        if r0 > 0:
            pieces.insert(0, jnp.zeros((r0, tq), F32))
        if LANES - r0 - MAX_SEL_BLOCKS > 0:
            pieces.append(jnp.zeros((LANES - r0 - MAX_SEL_BLOCKS, tq), F32))
        selq = jnp.concatenate(pieces, axis=0).T
        in_blk = (lane >= r0) & (lane < r0 + MAX_SEL_BLOCKS)
        qsels.append(jnp.concatenate([jnp.where(in_blk, selq, x).astype(BF16) for x in qbase], axis=0))

    def step(j, carry, bias):
        start = pl.multiple_of(j * SEL_KEYS, SEL_KEYS)
        scores = [_dot_nt(ksa_ref[kv, pl.ds(start, SEL_KEYS), :], qsels[kv]) for kv in range(KV_HEADS)]
        if bias is not None:
            scores = [sc + bias for sc in scores]
        return _softmax_steps(carry, scores, [vst_ref[j]] * KV_HEADS)

    init = tuple(_softmax_init(GQA_GROUP * tq) for _ in range(KV_HEADS))
    carry = lax.fori_loop(0, nfull, lambda j, c: step(j, c, None), init)
    carry = step(nfull, carry, dcausal)
    o_sel = [acc / l for (_, l, acc) in carry]

    o_win = _banded_attend(q, qt_ref, kwp_ref, vwt_ref, i, NSA_WINDOW, slopes, None)

    gates = cg_ref[0].T
    for g in range(GQA_GROUP):
        total = jnp.zeros((LANES, tq), F32)
        for r, branch in enumerate((o_cmp, o_sel, o_win)):
            c0 = FOX_HEADS + (r * GQA_GROUP + g) * KV_HEADS
            gate = jnp.where(row < HEAD_DIM, gates[c0:c0 + 1, :], gates[c0 + 1:c0 + 2, :])
            total = total + gate * _merge_kv(branch, g, tq, row)
        o_ref[0, :, g * LANES:(g + 1) * LANES] = total.T.astype(BF16)


def _nsa(hb3, kc, vc, cg, ovt, slopes):
    b, s, _ = hb3.shape
    nc = kc.shape[1]
    w = GQA_GROUP * LANES
    n_sel = min(SEL_TOP_N, s // SEL_BLOCK)
    qt = jnp.asarray(_query_tables(slopes, Q_TILE))
    kvspec = lambda c: pl.BlockSpec((1, s, LANES), lambda bi, i: (bi, 0, c // LANES))
    return pl.pallas_call(
        functools.partial(_nsa_kernel, slopes=slopes, n_sel=n_sel),
        grid=(b, s // Q_TILE),
        in_specs=[pl.BlockSpec((1, Q_TILE, w), lambda bi, i: (bi, i, C_NQ // w)),
                  kvspec(C_NKS), kvspec(C_NVS), kvspec(C_NKW), kvspec(C_NVW),
                  pl.BlockSpec((1, nc, LANES), lambda bi, i: (bi, 0, 0)),
                  pl.BlockSpec((1, nc, LANES), lambda bi, i: (bi, 0, 0)),
                  pl.BlockSpec((1, Q_TILE, LANES), lambda bi, i: (bi, i, 0)),
                  pl.BlockSpec(ovt.shape, lambda bi, i: (0, 0, 0)),
                  pl.BlockSpec(qt.shape, lambda bi, i: (0, 0, 0, 0))],
        out_specs=pl.BlockSpec((1, Q_TILE, w), lambda bi, i: (bi, i, 0)),
        out_shape=jax.ShapeDtypeStruct((b, s, w), BF16),
        scratch_shapes=[pltpu.VMEM((KV_HEADS, s, LANES), BF16),
                        pltpu.VMEM((s // SEL_KEYS, LANES, SEL_KEYS), BF16),
                        pltpu.VMEM((KV_HEADS, s + NSA_WINDOW, LANES), BF16),
                        pltpu.VMEM(((s + NSA_WINDOW) // LANES, LANES, LANES), BF16),
                        pltpu.VMEM((KV_HEADS, nc, LANES), BF16),
                        pltpu.VMEM((LANES, nc), BF16)],
        compiler_params=_params("arbitrary", "arbitrary"),
        name="nsa",
    )(hb3, hb3, hb3, hb3, hb3, kc, vc, cg, ovt, qt)


def _layer_norm(y, g, b):
    mu = jnp.mean(y, axis=-1, keepdims=True)
    var = jnp.mean(jnp.square(y - mu), axis=-1, keepdims=True)
    return (y - mu) * lax.rsqrt(var + LN_EPS) * g + b


def _mix_kernel(x_ref, of_ref, os_ref, on_ref, wf_ref, ws_ref, wn_ref, g_ref, b_ref, rw_ref, rb_ref,
                x1_ref, ri_ref, rwt_ref, cnt_ref, carry_ref, *, alpha):
    i = pl.program_id(0)
    tm = x_ref.shape[0]
    lane = _lane_iota()

    @pl.when(i == 0)
    def _init():
        carry_ref[...] = jnp.zeros_like(carry_ref)

    mix = jnp.dot(of_ref[...], wf_ref[...], preferred_element_type=F32)
    mix = mix + jnp.dot(os_ref[...], ws_ref[...], preferred_element_type=F32)
    mix = mix + jnp.dot(on_ref[...], wn_ref[...], preferred_element_type=F32)
    x1 = _layer_norm(alpha * x_ref[...] + mix, g_ref[...], b_ref[...])
    x1_ref[...] = x1

    logits = jnp.dot(x1.astype(BF16), rw_ref[...], preferred_element_type=F32) + rb_ref[...]
    v = jnp.where(lane < N_EXPERTS, logits, NEG)
    lane_f = lane.astype(F32)
    tops, hots = [], []
    for _ in range(TOP_K):
        mx = jnp.max(v, axis=-1, keepdims=True)
        idx = jnp.min(jnp.where(v == mx, lane_f, float(LANES)), axis=-1, keepdims=True)
        hot = lane_f == idx
        tops.append((mx, idx))
        hots.append(hot)
        v = jnp.where(hot, 2.0 * NEG, v)
    es = [jnp.exp(mx - tops[0][0]) for mx, _ in tops]
    den = es[0] + es[1] + es[2] + es[3]

    member = jnp.zeros((tm, LANES), F32)
    for hot in hots:
        member = member + jnp.where(hot, 1.0, 0.0)
    r = lax.broadcasted_iota(I32, (tm, tm), 0)
    c = lax.broadcasted_iota(I32, (tm, tm), 1)
    strict = jnp.where(c < r, 1.0, 0.0).astype(BF16)
    before = jnp.dot(strict, member.astype(BF16), preferred_element_type=F32) + carry_ref[0:1, :]
    ri = jnp.zeros((tm, LANES), F32)
    rwt = jnp.zeros((tm, LANES), F32)
    for k in range(TOP_K):
        rank = jnp.sum(jnp.where(hots[k], before, 0.0), axis=-1, keepdims=True)
        ri = jnp.where(lane == k, tops[k][1], ri)
        ri = jnp.where(lane == TOP_K + k, rank, ri)
        rwt = jnp.where(lane == k, es[k] / den, rwt)
    ri_ref[...] = ri.astype(I32)
    rwt_ref[...] = rwt
    new = carry_ref[0:1, :] + jnp.sum(member, axis=0, keepdims=True)
    carry_ref[...] = jnp.zeros_like(carry_ref) + new
    cnt_ref[...] = jnp.zeros_like(cnt_ref) + new


def _mix(x2, of, osw, on, wf, ws, wn, g, b, rw, rb, alpha, tm):
    n, d = x2.shape
    full = lambda a: pl.BlockSpec(a.shape, lambda i: (0,) * a.ndim)
    rows = lambda a: pl.BlockSpec((tm, a.shape[1]), lambda i: (i, 0))
    return pl.pallas_call(
        functools.partial(_mix_kernel, alpha=alpha),
        grid=(n // tm,),
        in_specs=[rows(x2), rows(of), rows(osw), rows(on), full(wf), full(ws), full(wn),
                  full(g), full(b), full(rw), full(rb)],
        out_specs=[pl.BlockSpec((tm, d), lambda i: (i, 0)),
                   pl.BlockSpec((tm, LANES), lambda i: (i, 0)),
                   pl.BlockSpec((tm, LANES), lambda i: (i, 0)),
                   pl.BlockSpec((8, LANES), lambda i: (0, 0))],
        out_shape=[jax.ShapeDtypeStruct((n, d), F32), jax.ShapeDtypeStruct((n, LANES), I32),
                   jax.ShapeDtypeStruct((n, LANES), F32), jax.ShapeDtypeStruct((8, LANES), F32)],
        scratch_shapes=[pltpu.VMEM((8, LANES), F32)],
        compiler_params=_params("arbitrary"),
        name="mix_ln_router",
    )(x2, of, osw, on, wf, ws, wn, g, b, rw, rb)


def _dispatch_kernel(pe_ref, pd_ref, nu_ref, dest_ref, x_ref, xs_hbm, zbuf, zsem, sem, *, rows):
    i = pl.program_id(0)
    tokens = x_ref.shape[0]
    nblk = xs_hbm.shape[0] // rows

    @pl.when(i == 0)
    def _zero_fill():
        zbuf[...] = jnp.zeros_like(zbuf)

        def block(start):
            return pltpu.make_async_copy(zbuf, xs_hbm.at[pl.ds(pl.multiple_of(start, rows), rows)], zsem)

        for phase in ("start", "wait"):
            for e in range(N_EXPERTS):
                @pl.when(pd_ref[e] > 0)
                def _(e=e, phase=phase):
                    getattr(block(pe_ref[e] - rows), phase)()

            def tail(j, _, phase=phase):
                getattr(block(j * rows), phase)()
                return 0

            lax.fori_loop(nu_ref[0], nblk, tail, 0)

    for t in range(tokens):
        for k in range(TOP_K):
            c = t * TOP_K + k
            pltpu.make_async_copy(x_ref.at[pl.ds(t, 1)],
                                  xs_hbm.at[pl.ds(dest_ref[c], 1)], sem).start(priority=c % 2)
    for k in range(TOP_K):
        pltpu.make_async_copy(x_ref, xs_hbm.at[pl.ds(0, tokens)], sem).wait()


def _dispatch(pad_end, padded, n_used, dest_flat, x1, rows_total, tokens, rows):
    n, d = x1.shape
    grid_spec = pltpu.PrefetchScalarGridSpec(
        num_scalar_prefetch=3,
        grid=(n // tokens,),
        in_specs=[pl.BlockSpec((tokens * TOP_K,), lambda i, *_: (i,), memory_space=pltpu.SMEM),
                  pl.BlockSpec((tokens, d), lambda i, *_: (i, 0))],
        out_specs=pl.BlockSpec(memory_space=pl.ANY),
        scratch_shapes=[pltpu.VMEM((rows, d), F32), pltpu.SemaphoreType.DMA(()), pltpu.SemaphoreType.DMA(())],
    )
    return pl.pallas_call(
        functools.partial(_dispatch_kernel, rows=rows),
        grid_spec=grid_spec,
        out_shape=jax.ShapeDtypeStruct((rows_total, d), F32),
        compiler_params=_params("arbitrary"),
        name="moe_dispatch",
    )(pad_end, padded, n_used, dest_flat, x1)


def _ffn_kernel(be_ref, nu_ref, x_ref, wg_ref, wu_ref, bg_ref, bu_ref, wd_ref, bd_ref, y_ref):
    i = pl.program_id(0)

    @pl.when(i < nu_ref[0])
    def _run():
        x = x_ref[...].astype(BF16)
        gate = jnp.dot(x, wg_ref[0], preferred_element_type=F32) + bg_ref[0]
        up = jnp.dot(x, wu_ref[0], preferred_element_type=F32) + bu_ref[0]
        gate = jnp.minimum(gate, SWIGLU_LIMIT)
        up = jnp.clip(up, -SWIGLU_LIMIT, SWIGLU_LIMIT)
        glu = gate * jax.nn.sigmoid(gate * SWIGLU_ALPHA)
        act = ((up + 1.0) * glu).astype(BF16)
        y_ref[...] = jnp.dot(act, wd_ref[0, 0].astype(BF16), preferred_element_type=F32) + bd_ref[0]

    @pl.when(i >= nu_ref[0])
    def _skip():
        y_ref[...] = jnp.zeros_like(y_ref)


def _ffn(blk_e, n_used, xs, wg, wu, bg, bu, wd_all, layer, bd, rows):
    p, d = xs.shape
    wspec = lambda a: pl.BlockSpec((1,) + a.shape[1:], lambda i, be, nu: (be[i], 0, 0))
    grid_spec = pltpu.PrefetchScalarGridSpec(
        num_scalar_prefetch=2,
        grid=(p // rows,),
        in_specs=[pl.BlockSpec((rows, d), lambda i, be, nu: (i, 0)),
                  wspec(wg), wspec(wu), wspec(bg), wspec(bu),
                  pl.BlockSpec((1, 1) + wd_all.shape[2:], lambda i, be, nu: (layer, be[i], 0, 0)),
                  wspec(bd)],
        out_specs=pl.BlockSpec((rows, d), lambda i, be, nu: (i, 0)),
    )
    return pl.pallas_call(
        _ffn_kernel,
        grid_spec=grid_spec,
        out_shape=jax.ShapeDtypeStruct((p, d), F32),
        compiler_params=_params("arbitrary"),
        name="moe_ffn",
    )(blk_e, n_used, xs, wg, wu, bg, bu, wd_all, bd)


def _combine_kernel(destc_ref, destn_ref, x1_ref, w_ref, g_ref, b_ref, y_hbm, o_ref, buf0, buf1, sem, *, alpha):
    i = pl.program_id(0)
    last = pl.num_programs(0) - 1
    tokens = x1_ref.shape[0]
    bufs = (buf0, buf1)

    def issue(dest_ref, dst):
        for t in range(tokens):
            for k in range(TOP_K):
                c = t * TOP_K + k
                pltpu.make_async_copy(y_hbm.at[pl.ds(dest_ref[c], 1)],
                                      bufs[dst].at[k, pl.ds(t, 1)], sem.at[dst]).start(priority=c % 2)

    @pl.when(i == 0)
    def _prologue():
        issue(destc_ref, 0)

    def run(slot):
        @pl.when(i < last)
        def _():
            issue(destn_ref, 1 - slot)

        for k in range(TOP_K):
            pltpu.make_async_copy(y_hbm.at[pl.ds(0, tokens)], bufs[slot].at[k], sem.at[slot]).wait()
        w = w_ref[...]
        ffn = jnp.zeros(x1_ref.shape, F32)
        for k in range(TOP_K):
            ffn = ffn + bufs[slot][k] * w[:, k:k + 1]
        o_ref[...] = _layer_norm(alpha * x1_ref[...] + ffn, g_ref[...], b_ref[...])

    for slot in range(2):
        pl.when(i % 2 == slot)(functools.partial(run, slot))


def _combine(dest_flat, x1, rwt, g, b, y, alpha, tokens):
    n, d = x1.shape
    steps = n // tokens
    full = lambda a: pl.BlockSpec(a.shape, lambda i: (0,) * a.ndim)
    return pl.pallas_call(
        functools.partial(_combine_kernel, alpha=alpha),
        grid=(steps,),
        in_specs=[pl.BlockSpec((tokens * TOP_K,), lambda i: (i,), memory_space=pltpu.SMEM),
                  pl.BlockSpec((tokens * TOP_K,), lambda i: (jnp.minimum(i + 1, steps - 1),),
                               memory_space=pltpu.SMEM),
                  pl.BlockSpec((tokens, d), lambda i: (i, 0)),
                  pl.BlockSpec((tokens, LANES), lambda i: (i, 0)),
                  full(g), full(b),
                  pl.BlockSpec(memory_space=pl.ANY)],
        out_specs=pl.BlockSpec((tokens, d), lambda i: (i, 0)),
        out_shape=jax.ShapeDtypeStruct((n, d), F32),
        scratch_shapes=[pltpu.VMEM((TOP_K, tokens, d), F32), pltpu.VMEM((TOP_K, tokens, d), F32),
                        pltpu.SemaphoreType.DMA((2,))],
        compiler_params=_params("arbitrary"),
        name="moe_combine_ln",
    )(dest_flat, dest_flat, x1, rwt, g, b, y)


def _prep_gate_up_kernel(w_ref, p_ref, g_ref, u_ref):
    groups = w_ref.shape[3] // (2 * LANES)
    for c in range(groups):
        blk = w_ref[0, 0, :, c * 2 * LANES:(c + 1) * 2 * LANES].astype(BF16)
        o = jnp.dot(blk, p_ref[...], preferred_element_type=F32)
        g_ref[0, :, c * LANES:(c + 1) * LANES] = o[:, :LANES].astype(BF16)
        u_ref[0, :, c * LANES:(c + 1) * LANES] = o[:, LANES:].astype(BF16)


def _prep_gate_up(w_gu_all, layer):
    _, e, d, f2 = w_gu_all.shape
    perm = np.zeros((2 * LANES, 2 * LANES), np.float32)
    perm[2 * np.arange(LANES), np.arange(LANES)] = 1.0
    perm[2 * np.arange(LANES) + 1, LANES + np.arange(LANES)] = 1.0
    return pl.pallas_call(
        _prep_gate_up_kernel,
        grid=(e,),
        in_specs=[pl.BlockSpec((1, 1, d, f2), lambda i: (layer, i, 0, 0)),
                  pl.BlockSpec((2 * LANES, 2 * LANES), lambda i: (0, 0))],
        out_specs=[pl.BlockSpec((1, d, f2 // 2), lambda i: (i, 0, 0)),
                   pl.BlockSpec((1, d, f2 // 2), lambda i: (i, 0, 0))],
        out_shape=[jax.ShapeDtypeStruct((e, d, f2 // 2), BF16), jax.ShapeDtypeStruct((e, d, f2 // 2), BF16)],
        compiler_params=_params("arbitrary"),
        name="prep_gate_up",
    )(w_gu_all, jnp.asarray(perm, BF16))


def _gqa_cols(base):
    idx = np.empty(GQA_GROUP * KV_HEADS * HEAD_DIM, np.int64)
    for g in range(GQA_GROUP):
        for kv in range(KV_HEADS):
            dst = (g * KV_HEADS + kv) * HEAD_DIM
            src = (kv * GQA_GROUP + g) * HEAD_DIM
            idx[dst:dst + HEAD_DIM] = base + src + np.arange(HEAD_DIM)
    return idx


def _inproj_layout():
    fw = FOX_HEADS * HEAD_DIM
    qw = SWA_Q_HEADS * HEAD_DIM
    kw = KV_HEADS * HEAD_DIM
    sizes = (fw, fw, fw, FOX_HEADS, qw, kw, kw, qw, kw, kw, kw, kw, kw, kw, 3 * NSA_Q_HEADS)
    off = np.concatenate([[0], np.cumsum(sizes)])
    (o_fq, o_fk, o_fv, o_ff, o_sq, o_sk, o_sv, o_nq, o_nkc, o_nvc, o_nks, o_nvs, o_nkw, o_nvw, o_ng, total) = off
    ar = np.arange
    cols = np.concatenate([
        _gqa_cols(o_sq), _gqa_cols(o_nq), o_fq + ar(fw), o_fk + ar(fw), o_fv + ar(fw),
        o_sk + ar(kw), o_sv + ar(kw), o_nks + ar(kw), o_nvs + ar(kw), o_nkw + ar(kw), o_nvw + ar(kw),
        o_nkc + ar(kw), o_nvc + ar(kw)])
    gate = np.full(LANES, total, np.int64)
    gate[:FOX_HEADS] = o_ff + ar(FOX_HEADS)
    for r in range(3):
        for g in range(GQA_GROUP):
            for kv in range(KV_HEADS):
                gate[FOX_HEADS + (r * GQA_GROUP + g) * KV_HEADS + kv] = o_ng + (kv * GQA_GROUP + g) * 3 + r
    cols = np.concatenate([cols, gate])
    scale = np.ones(cols.shape[0], np.float32)
    q_scale = np.float32(HEAD_DIM ** -0.5 * LOG2E)
    scale[C_SQ:C_SQ + qw] = q_scale
    scale[C_NQ:C_NQ + qw] = q_scale
    scale[C_FQ:C_FQ + fw] = q_scale
    return cols, scale


def _outproj_rows():
    fw = FOX_HEADS * HEAD_DIM
    qw = SWA_Q_HEADS * HEAD_DIM
    return np.arange(fw), _gqa_cols(fw), _gqa_cols(fw + qw)


def _overlap_matrix_t(s_len, nc):
    ns = s_len // SEL_BLOCK
    cmp_start = np.arange(nc) * CMP_STRIDE
    sel_start = np.arange(ns) * SEL_BLOCK
    ov = np.clip(np.minimum(cmp_start[:, None] + CMP_BLOCK, sel_start[None, :] + SEL_BLOCK)
                 - np.maximum(cmp_start[:, None], sel_start[None, :]), 0, None) / CMP_BLOCK
    ov[(s_len - CMP_BLOCK) // CMP_STRIDE + 1:] = 0.0
    out = np.zeros((KV_HEADS, LANES, nc), np.float32)
    for kv in range(KV_HEADS):
        r0 = _spare(kv) + L_BLK
        out[kv, r0:r0 + ns, :] = ov.T
    return out


def _compress_weights(pe, w1, w2):
    pe2 = jnp.concatenate([pe, pe], axis=-1)
    w1r = w1.reshape(2, CMP_BLOCK, HEAD_DIM, HEAD_DIM)
    z = jnp.zeros_like(w1r)
    w1b = jnp.concatenate([jnp.concatenate([w1r, z], axis=-1), jnp.concatenate([z, w1r], axis=-1)], axis=-2)
    z2 = jnp.zeros_like(w2)
    w2b = jnp.concatenate([jnp.concatenate([w2, z2], axis=-1), jnp.concatenate([z2, w2], axis=-1)], axis=-2)
    return pe2, w1b.astype(BF16), w2b.astype(BF16)


def kernel(x, w_in, b_in, sinks, cmp_pe, cmp_w1, cmp_w2, w_out, ln1_g, ln1_b, router_w, router_b,
           w_gate_up, b_gate_up, w_down, b_down, ln2_g, ln2_b):
    depth = w_in.shape[0]
    bsz, s_len, d = x.shape
    n = bsz * s_len
    nk = n * TOP_K
    alpha = float((2.0 * depth) ** 0.25)
    assert s_len % SEL_KEYS == 0 and s_len // SEL_BLOCK <= MAX_SEL_BLOCKS and n % 512 == 0
    slopes_swa, slopes_nsa = _alibi()
    cols, scale = _inproj_layout()
    rows_f, rows_s, rows_n = _outproj_rows()
    nc = s_len // CMP_STRIDE
    ovt = jnp.asarray(_overlap_matrix_t(s_len, nc), BF16)
    nblk = -(-nk // MOE_ROWS) + N_EXPERTS
    p_rows = nblk * MOE_ROWS
    f = w_down.shape[2]

    x2 = x.reshape(n, d)
    for l in range(depth):
        w_ext = jnp.concatenate([w_in[l], jnp.zeros((d, 1), F32)], axis=1)
        b_ext = jnp.concatenate([b_in[l], jnp.zeros((1,), F32)])
        w_l = (jnp.take(w_ext, cols, axis=1) * scale).astype(BF16)
        b_l = (jnp.take(b_ext, cols) * scale).reshape(1, -1)
        hb, hf = _inproj(x2, w_l, b_l, 512)
        hb3 = hb.reshape(bsz, s_len, CB)
        hf3 = hf.reshape(bsz, s_len, CF)
        cg = _gates(hf3)
        o_fox = _fox(hb3, cg)
        o_swa = _swa(hb3, sinks[l] * LOG2E, slopes_swa)
        grp = hf3[:, :, :2 * LANES].reshape(bsz, nc, CMP_STRIDE * 2 * LANES)
        pe2, w1b, w2b = _compress_weights(cmp_pe[l], cmp_w1[l], cmp_w2[l])
        kc, vc = _compress(grp, pe2, w1b, w2b)
        o_nsa = _nsa(hb3, kc, vc, cg, ovt, slopes_nsa)

        wo = w_out[l].astype(BF16)
        rw = jnp.concatenate([router_w[l], jnp.zeros((d, LANES - N_EXPERTS), F32)], axis=1).astype(BF16)
        rb = jnp.concatenate([router_b[l], jnp.zeros((LANES - N_EXPERTS,), F32)]).reshape(1, LANES)
        x1, ri, rwt, cnt = _mix(
            x2, o_fox.reshape(n, -1), o_swa.reshape(n, -1), o_nsa.reshape(n, -1),
            jnp.take(wo, rows_f, axis=0), jnp.take(wo, rows_s, axis=0), jnp.take(wo, rows_n, axis=0),
            ln1_g[l].reshape(1, d), ln1_b[l].reshape(1, d), rw, rb, alpha, 512)

        counts = cnt[0, :N_EXPERTS].astype(I32)
        padded = (counts + MOE_ROWS - 1) // MOE_ROWS * MOE_ROWS
        pad_end = jnp.cumsum(padded)
        pad_start = pad_end - padded
        dest = (jnp.take(pad_start, ri[:, :TOP_K]) + ri[:, TOP_K:2 * TOP_K]).reshape(nk)
        blk_start = jnp.arange(nblk, dtype=I32) * MOE_ROWS
        blk_e = jnp.minimum(jnp.sum((pad_end[None, :] <= blk_start[:, None]).astype(I32), axis=1),
                            N_EXPERTS - 1)
        n_used = (pad_end[-1:] // MOE_ROWS).astype(I32)

        xs = _dispatch(pad_end, padded, n_used, dest, x1, p_rows, 512, MOE_ROWS)
        w_gate, w_up = _prep_gate_up(w_gate_up, l)
        bgu = b_gate_up[l].reshape(N_EXPERTS, 1, f, 2)
        y = _ffn(blk_e, n_used, xs, w_gate, w_up,
                 bgu[..., 0], bgu[..., 1], w_down, l, b_down[l].reshape(N_EXPERTS, 1, d),
                 MOE_ROWS)
        x2 = _combine(dest, x1, rwt, ln2_g[l].reshape(1, d), ln2_b[l].reshape(1, d), y, alpha, 256)
    return x2.reshape(bsz, s_len, d)
```

```python
import functools

import numpy as np
import jax
import jax.numpy as jnp
from jax import lax
from jax.experimental import pallas as pl
from jax.experimental.pallas import tpu as pltpu

F32 = jnp.float32
BF16 = jnp.bfloat16
I32 = jnp.int32

HEAD_DIM = 64
LANES = 128
N_HEADS = 16
SWA_Q_HEADS = 6
NSA_Q_HEADS = 6
FOX_HEADS = 4
KV_HEADS = 2
GQA_GROUP = 3
SWA_WINDOW = 128
CMP_BLOCK = 32
CMP_STRIDE = 16
SEL_BLOCK = 64
SEL_TOP_N = 8
NSA_WINDOW = 512
SEL_FORCE = 1.0e4
N_EXPERTS = 32
TOP_K = 4
SWIGLU_LIMIT = 7.0
SWIGLU_ALPHA = 1.702
LN_EPS = 1e-5
NEG = -1.0e30

Q_TILE = 256
SWA_TILE = 256
FOX_TILE = 512
FOX_KEYS = 512
SEL_KEYS = 512
MAX_SEL_BLOCKS = 32
MOE_ROWS = 512
VMEM_LIMIT = 56 * 1024 * 1024

L_BLK = 0
L_QT = 32
L_KH = 35
L_KL = 38
L_PAD = 41
L_QB = 42
LOG2E = 1.4426950408889634

C_SQ, C_NQ, C_FQ, C_FK, C_FV = 0, 384, 768, 1024, 1280
C_SK, C_SV, C_NKS, C_NVS, C_NKW, C_NVW = 1536, 1664, 1792, 1920, 2048, 2176
CB = 2304
CF = 384


def _alibi():
    n = SWA_Q_HEADS + NSA_Q_HEADS
    s = (2.0 ** (-8.0 * np.arange(1, n + 1) / n)).astype(np.float32) * np.float32(LOG2E)
    return s[0::2].reshape(KV_HEADS, GQA_GROUP), s[1::2].reshape(KV_HEADS, GQA_GROUP)


def _query_tables(slopes, tq):
    out = np.zeros((KV_HEADS, GQA_GROUP, tq, LANES), np.float32)
    r = np.arange(tq, dtype=np.float32)
    for kv in range(KV_HEADS):
        a0 = _spare(kv)
        for g in range(GQA_GROUP):
            v = (r * -slopes[kv][g]).astype(np.float32)
            for c in range(3):
                p = v.astype(BF16).astype(np.float32)
                out[kv, g, :, a0 + L_QT + c] = p
                v = (v - p).astype(np.float32)
            for c, part in enumerate(_bf16_parts(slopes[kv][g])):
                out[kv, g, :, a0 + L_KH + c] = part
                out[kv, g, :, a0 + L_KL + c] = part
    return out


def _bf16_parts(v):
    v = np.float32(v)
    out = []
    for _ in range(3):
        p = np.float32(np.asarray(v, dtype=BF16))
        out.append(float(p))
        v = np.float32(v - p)
    return out


def _split3(x):
    hi = x.astype(BF16).astype(F32)
    r = x - hi
    mid = r.astype(BF16).astype(F32)
    lo = (r - mid).astype(BF16).astype(F32)
    return hi, mid, lo


def _lane_iota():
    return lax.broadcasted_iota(I32, (1, LANES), 1)


def _row_iota():
    return lax.broadcasted_iota(I32, (LANES, 1), 0)


def _spare(kv_half):
    return HEAD_DIM if kv_half == 0 else 0


def _dot_nt(a, b):
    return lax.dot_general(a, b, (((1,), (1,)), ((), ())), preferred_element_type=F32)


def _transposed(x):
    return x.astype(F32).T.astype(BF16)


def _params(*sem):
    return pltpu.CompilerParams(dimension_semantics=sem, vmem_limit_bytes=VMEM_LIMIT)


def _inproj_kernel(x_ref, w_ref, b_ref, hb_ref, hf_ref):
    acc = jnp.dot(x_ref[...].astype(BF16), w_ref[...], preferred_element_type=F32) + b_ref[...]
    hb_ref[...] = acc[:, :CB].astype(BF16)
    hf_ref[...] = acc[:, CB:]


def _inproj(x2, w, b, tm):
    n, d = x2.shape
    ct = w.shape[1]
    return pl.pallas_call(
        _inproj_kernel,
        grid=(n // tm,),
        in_specs=[pl.BlockSpec((tm, d), lambda i: (i, 0)),
                  pl.BlockSpec((d, ct), lambda i: (0, 0)),
                  pl.BlockSpec((1, ct), lambda i: (0, 0))],
        out_specs=[pl.BlockSpec((tm, CB), lambda i: (i, 0)),
                   pl.BlockSpec((tm, CF), lambda i: (i, 0))],
        out_shape=[jax.ShapeDtypeStruct((n, CB), BF16), jax.ShapeDtypeStruct((n, CF), F32)],
        compiler_params=_params("arbitrary"),
        name="inproj",
    )(x2, w, b)


def _gates_kernel(g_ref, o_ref, *, chunk):
    s = g_ref.shape[1]
    lane = _lane_iota()
    r = lax.broadcasted_iota(I32, (chunk, chunk), 0)
    c = lax.broadcasted_iota(I32, (chunk, chunk), 1)
    tri = jnp.where(c <= r, 1.0, 0.0).astype(BF16)

    def body(j, carry):
        x = g_ref[0, pl.ds(pl.multiple_of(j * chunk, chunk), chunk), :]
        ls = (jnp.minimum(x, 0.0) - jnp.log1p(jnp.exp(-jnp.abs(x)))) * LOG2E
        cs = carry
        for part in _split3(ls):
            cs = cs + jnp.dot(tri, part.astype(BF16), preferred_element_type=F32)
        sg = jax.nn.sigmoid(x)
        o_ref[0, pl.ds(pl.multiple_of(j * chunk, chunk), chunk), :] = jnp.where(lane < FOX_HEADS, cs, sg)
        return cs[chunk - 1:chunk, :]

    lax.fori_loop(0, s // chunk, body, jnp.zeros((1, LANES), F32))


def _gates(hf3):
    b, s, _ = hf3.shape
    chunk = min(256, s)
    return pl.pallas_call(
        functools.partial(_gates_kernel, chunk=chunk),
        grid=(b,),
        in_specs=[pl.BlockSpec((1, s, LANES), lambda i: (i, 0, 2))],
        out_specs=pl.BlockSpec((1, s, LANES), lambda i: (i, 0, 0)),
        out_shape=jax.ShapeDtypeStruct((b, s, LANES), F32),
        compiler_params=_params("arbitrary"),
        name="gates",
    )(hf3)


def _softmax_steps(carries, scores, values):
    stats = []
    for (m, l, _), sc in zip(carries, scores):
        m_new = jnp.maximum(m, jnp.max(sc, axis=0, keepdims=True))
        alpha = jnp.exp2(m - m_new)
        pr = jnp.exp2(sc - m_new)
        stats.append((m_new, alpha * l + jnp.sum(pr, axis=0, keepdims=True), alpha, pr.astype(BF16)))
    return tuple((m, l, alpha * acc + jnp.dot(vt, pr, preferred_element_type=F32))
                 for (m, l, alpha, pr), (_, _, acc), vt in zip(stats, carries, values))


def _softmax_init(queries):
    return (jnp.full((1, queries), NEG, F32), jnp.zeros((1, queries), F32), jnp.zeros((LANES, queries), F32))


def _fox_kernel(q_ref, k_ref, v_ref, cq_ref, ck_ref, o_ref, ka_ref, vt_ref, *, tile, keys):
    i = pl.program_id(1)
    s_len = k_ref.shape[1]
    lane = _lane_iota()
    npair = FOX_HEADS // 2

    @pl.when(i == 0)
    def _fill():
        for h in range(FOX_HEADS):
            p, half = divmod(h, 2)
            a0 = _spare(half)
            k = k_ref[0, :, p * LANES:(p + 1) * LANES].astype(F32)
            hi, mid, lo = _split3(-ck_ref[0, :, h:h + 1])
            ka = jnp.where((lane >= a0 + L_QT) & (lane < a0 + L_QT + 3), 1.0, k)
            ka = jnp.where(lane == a0 + L_KH, hi, ka)
            ka = jnp.where(lane == a0 + L_KH + 1, mid, ka)
            ka = jnp.where(lane == a0 + L_KH + 2, lo, ka)
            ka_ref[h] = ka.astype(BF16)
        for p in range(npair):
            for c in range(s_len // keys):
                vt_ref[p, c] = _transposed(v_ref[0, c * keys:(c + 1) * keys, p * LANES:(p + 1) * LANES])

    qas = []
    for h in range(FOX_HEADS):
        p, half = divmod(h, 2)
        a0 = _spare(half)
        q = q_ref[0, :, p * LANES:(p + 1) * LANES].astype(F32)
        in_half = (lane < HEAD_DIM) if half == 0 else (lane >= HEAD_DIM)
        hi, mid, lo = _split3(cq_ref[0, :, h:h + 1])
        qa = jnp.where(in_half, q, 0.0)
        qa = jnp.where(lane == a0 + L_QT, hi, qa)
        qa = jnp.where(lane == a0 + L_QT + 1, mid, qa)
        qa = jnp.where(lane == a0 + L_QT + 2, lo, qa)
        qas.append(jnp.where((lane >= a0 + L_KH) & (lane < a0 + L_KH + 3), 1.0, qa).astype(BF16))

    nfull = (i * tile) // keys
    key = lax.broadcasted_iota(I32, (keys, tile), 0) + nfull * keys
    qry = lax.broadcasted_iota(I32, (keys, tile), 1) + i * tile
    causal = jnp.where(key <= qry, 0.0, NEG)

    def step(j, carry, masked):
        start = pl.multiple_of(j * keys, keys)
        scores = [_dot_nt(ka_ref[h, pl.ds(start, keys), :], qas[h]) for h in range(FOX_HEADS)]
        if masked:
            scores = [sc + causal for sc in scores]
        return _softmax_steps(carry, scores, [vt_ref[h // 2, j] for h in range(FOX_HEADS)])

    init = tuple(_softmax_init(tile) for _ in range(FOX_HEADS))
    carry = lax.fori_loop(0, nfull, functools.partial(step, masked=False), init)
    carry = step(nfull, carry, True)
    row = _row_iota()
    for p in range(npair):
        (_, l0, a0_), (_, l1, a1_) = carry[2 * p], carry[2 * p + 1]
        o_ref[0, :, p * LANES:(p + 1) * LANES] = jnp.where(row < HEAD_DIM, a0_ / l0, a1_ / l1).T.astype(BF16)


def _fox(hb3, cg):
    b, s, _ = hb3.shape
    tile = min(FOX_TILE, s)
    keys = min(FOX_KEYS, s)
    assert keys % tile == 0 and s % keys == 0
    w = FOX_HEADS * HEAD_DIM
    return pl.pallas_call(
        functools.partial(_fox_kernel, tile=tile, keys=keys),
        grid=(b, s // tile),
        in_specs=[pl.BlockSpec((1, tile, w), lambda bi, i: (bi, i, C_FQ // w)),
                  pl.BlockSpec((1, s, w), lambda bi, i: (bi, 0, C_FK // w)),
                  pl.BlockSpec((1, s, w), lambda bi, i: (bi, 0, C_FV // w)),
                  pl.BlockSpec((1, tile, LANES), lambda bi, i: (bi, i, 0)),
                  pl.BlockSpec((1, s, LANES), lambda bi, i: (bi, 0, 0))],
        out_specs=pl.BlockSpec((1, tile, w), lambda bi, i: (bi, i, 0)),
        out_shape=jax.ShapeDtypeStruct((b, s, w), BF16),
        scratch_shapes=[pltpu.VMEM((FOX_HEADS, s, LANES), BF16),
                        pltpu.VMEM((FOX_HEADS // 2, s // keys, LANES, keys), BF16)],
        compiler_params=_params("arbitrary", "arbitrary"),
        name="fox",
    )(hb3, hb3, hb3, cg, cg)


def _fill_banded(k_ref, v_ref, kp_ref, vt_ref, window):
    s_len = k_ref.shape[1]
    lane = _lane_iota()
    k = k_ref[0].astype(F32)
    pos = lax.broadcasted_iota(I32, (s_len, 1), 0) + window
    pos_hi = ((pos >> 8) << 8).astype(F32)
    pos_lo = (pos & 255).astype(F32)
    for kv in range(KV_HEADS):
        a0 = _spare(kv)
        ka = jnp.where((lane >= a0 + L_QT) & (lane < a0 + L_QT + 3), 1.0, k)
        ka = jnp.where((lane >= a0 + L_QB) & (lane < a0 + L_QB + 3), 1.0, ka)
        ka = jnp.where((lane >= a0 + L_KH) & (lane < a0 + L_KH + 3), pos_hi, ka)
        ka = jnp.where((lane >= a0 + L_KL) & (lane < a0 + L_KL + 3), pos_lo, ka)
        ka = jnp.where(lane == a0 + L_PAD, 0.0, ka)
        kp_ref[kv, pl.ds(window, s_len), :] = ka.astype(BF16)
        pad = jnp.where(lane == a0 + L_PAD, NEG, 0.0) + jnp.zeros((window, LANES), F32)
        kp_ref[kv, pl.ds(0, window), :] = pad.astype(BF16)
    npad = window // LANES
    for c in range(npad):
        vt_ref[c] = jnp.zeros((LANES, LANES), BF16)
    for c in range(s_len // LANES):
        vt_ref[npad + c] = _transposed(v_ref[0, c * LANES:(c + 1) * LANES, :])


def _alibi_query(q, table, kv, t0, slope, lane, pad_lane):
    a0 = _spare(kv)
    in_half = (lane < HEAD_DIM) if kv == 0 else (lane >= HEAD_DIM)
    hi, mid, lo = _split3(t0 * (-float(slope)))
    qa = jnp.where(in_half, q.astype(F32), table)
    qa = jnp.where(lane == a0 + L_QB, hi, qa)
    qa = jnp.where(lane == a0 + L_QB + 1, mid, qa)
    qa = jnp.where(lane == a0 + L_QB + 2, lo, qa)
    if pad_lane:
        qa = jnp.where(lane == a0 + L_PAD, 1.0, qa)
    return qa


def _banded_attend(q, qt_ref, kp_ref, vt_ref, i, window, slopes, sink_ref):
    tq = q.shape[0]
    span = window + tq
    lane = _lane_iota()
    start = pl.multiple_of(i * tq, tq)
    t0 = jnp.full((1, LANES), i * tq + window, I32).astype(F32)
    key = lax.broadcasted_iota(I32, (span, tq), 0)
    qry = lax.broadcasted_iota(I32, (span, tq), 1)
    band = jnp.where((key > qry) & (key <= qry + window), 0.0, NEG)
    band = jnp.concatenate([band] * GQA_GROUP, axis=1)
    outs = []
    for kv in range(KV_HEADS):
        qst = jnp.concatenate(
            [_alibi_query(q[:, g * LANES:(g + 1) * LANES], qt_ref[kv, g], kv, t0, slopes[kv][g], lane,
                          True).astype(BF16)
             for g in range(GQA_GROUP)], axis=0)
        sc = _dot_nt(kp_ref[kv, pl.ds(start, span), :], qst) + band
        m = jnp.max(sc, axis=0, keepdims=True)
        if sink_ref is not None:
            sink = jnp.concatenate([jnp.full((1, tq), sink_ref[kv * GQA_GROUP + g], F32)
                                    for g in range(GQA_GROUP)], axis=1)
            m = jnp.maximum(m, sink)
        pr = jnp.exp2(sc - m)
        den = jnp.sum(pr, axis=0, keepdims=True)
        if sink_ref is not None:
            den = den + jnp.exp2(sink - m)
        pr = pr.astype(BF16)
        o = jnp.zeros((LANES, GQA_GROUP * tq), F32)
        base = i * (tq // LANES)
        for c in range(0, span // LANES, 2):
            wide = min(2, span // LANES - c)
            vt = jnp.concatenate([vt_ref[base + c + j] for j in range(wide)], axis=1)
            o = o + jnp.dot(vt, pr[c * LANES:(c + wide) * LANES, :], preferred_element_type=F32)
        outs.append(o / den)
    return outs


def _merge_kv(outs, g, tq, row):
    return jnp.where(row < HEAD_DIM, outs[0][:, g * tq:(g + 1) * tq], outs[1][:, g * tq:(g + 1) * tq])


def _swa_kernel(sink_ref, q_ref, k_ref, v_ref, qt_ref, o_ref, kp_ref, vt_ref, *, slopes):
    i = pl.program_id(1)
    tq = q_ref.shape[1]

    @pl.when(i == 0)
    def _fill():
        _fill_banded(k_ref, v_ref, kp_ref, vt_ref, SWA_WINDOW)

    outs = _banded_attend(q_ref[0], qt_ref, kp_ref, vt_ref, i, SWA_WINDOW, slopes, sink_ref)
    row = _row_iota()
    for g in range(GQA_GROUP):
        o_ref[0, :, g * LANES:(g + 1) * LANES] = _merge_kv(outs, g, tq, row).T.astype(BF16)


def _swa(hb3, sinks, slopes):
    b, s, _ = hb3.shape
    w = GQA_GROUP * LANES
    tq = min(SWA_TILE, s)
    qt = jnp.asarray(_query_tables(slopes, tq))
    return pl.pallas_call(
        functools.partial(_swa_kernel, slopes=slopes),
        grid=(b, s // tq),
        in_specs=[pl.BlockSpec(memory_space=pltpu.SMEM),
                  pl.BlockSpec((1, tq, w), lambda bi, i: (bi, i, C_SQ // w)),
                  pl.BlockSpec((1, s, LANES), lambda bi, i: (bi, 0, C_SK // LANES)),
                  pl.BlockSpec((1, s, LANES), lambda bi, i: (bi, 0, C_SV // LANES)),
                  pl.BlockSpec(qt.shape, lambda bi, i: (0, 0, 0, 0))],
        out_specs=pl.BlockSpec((1, tq, w), lambda bi, i: (bi, i, 0)),
        out_shape=jax.ShapeDtypeStruct((b, s, w), BF16),
        scratch_shapes=[pltpu.VMEM((KV_HEADS, s + SWA_WINDOW, LANES), BF16),
                        pltpu.VMEM(((s + SWA_WINDOW) // LANES, LANES, LANES), BF16)],
        compiler_params=_params("arbitrary", "arbitrary"),
        name="swa",
    )(sinks, hb3, hb3, hb3, qt)


def _compress_kernel(g_ref, pe_ref, w1_ref, w2_ref, kc_ref, vc_ref):
    nc = g_ref.shape[1]
    for which, out_ref in ((0, kc_ref), (1, vc_ref)):
        ha = jnp.zeros((nc, LANES), F32)
        hb = jnp.zeros((nc, LANES), F32)
        for l in range(CMP_STRIDE):
            x = g_ref[0, :, (2 * l + which) * LANES:(2 * l + which + 1) * LANES]
            xa = (x + pe_ref[which, l:l + 1, :]).astype(BF16)
            xb = (x + pe_ref[which, CMP_STRIDE + l:CMP_STRIDE + l + 1, :]).astype(BF16)
            ha = ha + jnp.dot(xa, w1_ref[which, l], preferred_element_type=F32)
            hb = hb + jnp.dot(xb, w1_ref[which, CMP_STRIDE + l], preferred_element_type=F32)
        hid = ha + pltpu.roll(hb, nc - 1, 0)
        act = 0.5 * hid * (1.0 + jnp.tanh(np.sqrt(2.0 / np.pi).astype(np.float32) * (hid + 0.044715 * (hid * hid * hid))))
        out_ref[0] = jnp.dot(act.astype(BF16), w2_ref[which], preferred_element_type=F32).astype(BF16)


def _compress(g, pe2, w1b, w2b):
    b, nc, gw = g.shape
    return pl.pallas_call(
        _compress_kernel,
        grid=(b,),
        in_specs=[pl.BlockSpec((1, nc, gw), lambda i: (i, 0, 0)),
                  pl.BlockSpec(pe2.shape, lambda i: (0, 0, 0)),
                  pl.BlockSpec(w1b.shape, lambda i: (0, 0, 0, 0)),
                  pl.BlockSpec(w2b.shape, lambda i: (0, 0, 0))],
        out_specs=[pl.BlockSpec((1, nc, LANES), lambda i: (i, 0, 0)),
                   pl.BlockSpec((1, nc, LANES), lambda i: (i, 0, 0))],
        out_shape=[jax.ShapeDtypeStruct((b, nc, LANES), BF16), jax.ShapeDtypeStruct((b, nc, LANES), BF16)],
        compiler_params=_params("arbitrary"),
        name="compress",
    )(g, pe2, w1b, w2b)


def _nsa_kernel(q_ref, ks_ref, vs_ref, kw_ref, vw_ref, kc_ref, vc_ref, cg_ref, ovt_ref, qt_ref, o_ref,
                ksa_ref, vst_ref, kwp_ref, vwt_ref, kca_ref, vct_ref, *, slopes, n_sel):
    i = pl.program_id(1)
    tq = q_ref.shape[1]
    s_len = ks_ref.shape[1]
    nc = kc_ref.shape[1]
    ns = s_len // SEL_BLOCK
    lane = _lane_iota()
    row = _row_iota()

    @pl.when(i == 0)
    def _fill():
        _fill_banded(kw_ref, vw_ref, kwp_ref, vwt_ref, NSA_WINDOW)
        ks = ks_ref[0].astype(F32)
        pos = lax.broadcasted_iota(I32, (s_len, 1), 0)
        pos_hi = ((pos >> 8) << 8).astype(F32)
        pos_lo = (pos & 255).astype(F32)
        blk = pos // SEL_BLOCK
        kc = kc_ref[0].astype(F32)
        cend = lax.broadcasted_iota(I32, (nc, 1), 0) * CMP_STRIDE + (CMP_BLOCK - 1)
        cend_hi = ((cend >> 8) << 8).astype(F32)
        cend_lo = (cend & 255).astype(F32)
        for kv in range(KV_HEADS):
            a0 = _spare(kv)
            ka = jnp.where((lane >= a0 + L_BLK) & (lane < a0 + L_BLK + MAX_SEL_BLOCKS),
                           jnp.where(lane - (a0 + L_BLK) == blk, 1.0, 0.0), ks)
            ka = jnp.where((lane >= a0 + L_QT) & (lane < a0 + L_QT + 3), 1.0, ka)
            ka = jnp.where((lane >= a0 + L_QB) & (lane < a0 + L_QB + 3), 1.0, ka)
            ka = jnp.where((lane >= a0 + L_KH) & (lane < a0 + L_KH + 3), pos_hi, ka)
            ka = jnp.where((lane >= a0 + L_KL) & (lane < a0 + L_KL + 3), pos_lo, ka)
            ksa_ref[kv] = ka.astype(BF16)
            ca = jnp.where((lane >= a0 + L_QT) & (lane < a0 + L_QT + 3), 1.0, kc)
            ca = jnp.where((lane >= a0 + L_QB) & (lane < a0 + L_QB + 3), 1.0, ca)
            ca = jnp.where((lane >= a0 + L_KH) & (lane < a0 + L_KH + 3), cend_hi, ca)
            ca = jnp.where((lane >= a0 + L_KL) & (lane < a0 + L_KL + 3), cend_lo, ca)
            kca_ref[kv] = ca.astype(BF16)
        for c in range(s_len // SEL_KEYS):
            vst_ref[c] = _transposed(vs_ref[0, c * SEL_KEYS:(c + 1) * SEL_KEYS, :])
        vct_ref[...] = _transposed(vc_ref[0])

    q = q_ref[0]
    t0 = jnp.full((1, LANES), i * tq, I32).astype(F32)
    t_row = lax.broadcasted_iota(I32, (1, tq), 1) + i * tq
    cur = t_row // SEL_BLOCK
    cend = lax.broadcasted_iota(I32, (nc, tq), 0) * CMP_STRIDE + (CMP_BLOCK - 1)
    cvalid = jnp.concatenate([cend <= t_row] * GQA_GROUP, axis=1)
    nfull = (i * tq) // SEL_KEYS
    dkey = lax.broadcasted_iota(I32, (SEL_KEYS, tq), 0) + nfull * SEL_KEYS
    dcausal = jnp.where(dkey <= t_row, 0.0, NEG)
    dcausal = jnp.concatenate([dcausal] * GQA_GROUP, axis=1)

    o_cmp, qsels = [], []
    for kv in range(KV_HEADS):
        a0 = _spare(kv)
        qbase = [_alibi_query(q[:, g * LANES:(g + 1) * LANES], qt_ref[kv, g], kv, t0, slopes[kv][g], lane, False)
                 for g in range(GQA_GROUP)]

        qst = jnp.concatenate([x.astype(BF16) for x in qbase], axis=0)
        sc = jnp.where(cvalid, _dot_nt(kca_ref[kv], qst), NEG)
        m = jnp.max(sc, axis=0, keepdims=True)
        m = jnp.where(m > 0.5 * NEG, m, 0.0)
        e = jnp.where(cvalid, jnp.exp2(sc - m), 0.0)
        den = jnp.sum(e, axis=0, keepdims=True)
        pc = (e / jnp.where(den > 0.0, den, 1.0)).astype(BF16)
        o_cmp.append(jnp.dot(vct_ref[...], pc, preferred_element_type=F32))

        p_slc = jnp.zeros((LANES, tq), F32)
        for g in range(GQA_GROUP):
            p_slc = p_slc + jnp.dot(ovt_ref[kv], pc[:, g * tq:(g + 1) * tq], preferred_element_type=F32)
        r0 = a0 + L_BLK
        p_slc = p_slc[r0:r0 + MAX_SEL_BLOCKS, :]
        jb = lax.broadcasted_iota(I32, (MAX_SEL_BLOCKS, 1), 0)
        forced = (jb == 0) | (jb == cur) | (jb == cur - 1)
        score = jnp.where(jb > cur, -1.0, jnp.where(forced, SEL_FORCE, p_slc))
        score = jnp.where(jb < ns, score, -2.0)
        rank = jnp.zeros((MAX_SEL_BLOCKS, tq), F32)
        for b2 in range(ns):
            other = score[b2:b2 + 1, :]
            beats = (other > score) | ((other == score) & (jb > b2))
            rank = rank + jnp.where(beats, 1.0, 0.0)
        selq = jnp.where(rank < float(n_sel), 0.0, NEG)
        pieces = [selq]
        if r0 > 0:
            pieces.insert(0, jnp.zeros((r0, tq), F32))
        if LANES - r0 - MAX_SEL_BLOCKS > 0:
            pieces.append(jnp.zeros((LANES - r0 - MAX_SEL_BLOCKS, tq), F32))
        selq = jnp.concatenate(pieces, axis=0).T
        in_blk = (lane >= r0) & (lane < r0 + MAX_SEL_BLOCKS)
        qsels.append(jnp.concatenate([jnp.where(in_blk, selq, x).astype(BF16) for x in qbase], axis=0))

    def step(j, carry, bias):
        start = pl.multiple_of(j * SEL_KEYS, SEL_KEYS)
        scores = [_dot_nt(ksa_ref[kv, pl.ds(start, SEL_KEYS), :], qsels[kv]) for kv in range(KV_HEADS)]
        if bias is not None:
            scores = [sc + bias for sc in scores]
        return _softmax_steps(carry, scores, [vst_ref[j]] * KV_HEADS)

    init = tuple(_softmax_init(GQA_GROUP * tq) for _ in range(KV_HEADS))
    carry = lax.fori_loop(0, nfull, lambda j, c: step(j, c, None), init)
    carry = step(nfull, carry, dcausal)
    o_sel = [acc / l for (_, l, acc) in carry]

    o_win = _banded_attend(q, qt_ref, kwp_ref, vwt_ref, i, NSA_WINDOW, slopes, None)

    gates = cg_ref[0].T
    for g in range(GQA_GROUP):
        total = jnp.zeros((LANES, tq), F32)
        for r, branch in enumerate((o_cmp, o_sel, o_win)):
            c0 = FOX_HEADS + (r * GQA_GROUP + g) * KV_HEADS
            gate = jnp.where(row < HEAD_DIM, gates[c0:c0 + 1, :], gates[c0 + 1:c0 + 2, :])
            total = total + gate * _merge_kv(branch, g, tq, row)
        o_ref[0, :, g * LANES:(g + 1) * LANES] = total.T.astype(BF16)


def _nsa(hb3, kc, vc, cg, ovt, slopes):
    b, s, _ = hb3.shape
    nc = kc.shape[1]
    w = GQA_GROUP * LANES
    n_sel = min(SEL_TOP_N, s // SEL_BLOCK)
    qt = jnp.asarray(_query_tables(slopes, Q_TILE))
    kvspec = lambda c: pl.BlockSpec((1, s, LANES), lambda bi, i: (bi, 0, c // LANES))
    return pl.pallas_call(
        functools.partial(_nsa_kernel, slopes=slopes, n_sel=n_sel),
        grid=(b, s // Q_TILE),
        in_specs=[pl.BlockSpec((1, Q_TILE, w), lambda bi, i: (bi, i, C_NQ // w)),
                  kvspec(C_NKS), kvspec(C_NVS), kvspec(C_NKW), kvspec(C_NVW),
                  pl.BlockSpec((1, nc, LANES), lambda bi, i: (bi, 0, 0)),
                  pl.BlockSpec((1, nc, LANES), lambda bi, i: (bi, 0, 0)),
                  pl.BlockSpec((1, Q_TILE, LANES), lambda bi, i: (bi, i, 0)),
                  pl.BlockSpec(ovt.shape, lambda bi, i: (0, 0, 0)),
                  pl.BlockSpec(qt.shape, lambda bi, i: (0, 0, 0, 0))],
        out_specs=pl.BlockSpec((1, Q_TILE, w), lambda bi, i: (bi, i, 0)),
        out_shape=jax.ShapeDtypeStruct((b, s, w), BF16),
        scratch_shapes=[pltpu.VMEM((KV_HEADS, s, LANES), BF16),
                        pltpu.VMEM((s // SEL_KEYS, LANES, SEL_KEYS), BF16),
                        pltpu.VMEM((KV_HEADS, s + NSA_WINDOW, LANES), BF16),
                        pltpu.VMEM(((s + NSA_WINDOW) // LANES, LANES, LANES), BF16),
                        pltpu.VMEM((KV_HEADS, nc, LANES), BF16),
                        pltpu.VMEM((LANES, nc), BF16)],
        compiler_params=_params("arbitrary", "arbitrary"),
        name="nsa",
    )(hb3, hb3, hb3, hb3, hb3, kc, vc, cg, ovt, qt)


def _layer_norm(y, g, b):
    mu = jnp.mean(y, axis=-1, keepdims=True)
    var = jnp.mean(jnp.square(y - mu), axis=-1, keepdims=True)
    return (y - mu) * lax.rsqrt(var + LN_EPS) * g + b


def _mix_kernel(x_ref, of_ref, os_ref, on_ref, wf_ref, ws_ref, wn_ref, g_ref, b_ref, rw_ref, rb_ref,
                x1_ref, ri_ref, rwt_ref, cnt_ref, carry_ref, *, alpha):
    i = pl.program_id(0)
    tm = x_ref.shape[0]
    lane = _lane_iota()

    @pl.when(i == 0)
    def _init():
        carry_ref[...] = jnp.zeros_like(carry_ref)

    mix = jnp.dot(of_ref[...], wf_ref[...], preferred_element_type=F32)
    mix = mix + jnp.dot(os_ref[...], ws_ref[...], preferred_element_type=F32)
    mix = mix + jnp.dot(on_ref[...], wn_ref[...], preferred_element_type=F32)
    x1 = _layer_norm(alpha * x_ref[...] + mix, g_ref[...], b_ref[...])
    x1_ref[...] = x1

    logits = jnp.dot(x1.astype(BF16), rw_ref[...], preferred_element_type=F32) + rb_ref[...]
    v = jnp.where(lane < N_EXPERTS, logits, NEG)
    lane_f = lane.astype(F32)
    tops, hots = [], []
    for _ in range(TOP_K):
        mx = jnp.max(v, axis=-1, keepdims=True)
        idx = jnp.min(jnp.where(v == mx, lane_f, float(LANES)), axis=-1, keepdims=True)
        hot = lane_f == idx
        tops.append((mx, idx))
        hots.append(hot)
        v = jnp.where(hot, 2.0 * NEG, v)
    es = [jnp.exp(mx - tops[0][0]) for mx, _ in tops]
    den = es[0] + es[1] + es[2] + es[3]

    member = jnp.zeros((tm, LANES), F32)
    for hot in hots:
        member = member + jnp.where(hot, 1.0, 0.0)
    r = lax.broadcasted_iota(I32, (tm, tm), 0)
    c = lax.broadcasted_iota(I32, (tm, tm), 1)
    strict = jnp.where(c < r, 1.0, 0.0).astype(BF16)
    before = jnp.dot(strict, member.astype(BF16), preferred_element_type=F32) + carry_ref[0:1, :]
    ri = jnp.zeros((tm, LANES), F32)
    rwt = jnp.zeros((tm, LANES), F32)
    for k in range(TOP_K):
        rank = jnp.sum(jnp.where(hots[k], before, 0.0), axis=-1, keepdims=True)
        ri = jnp.where(lane == k, tops[k][1], ri)
        ri = jnp.where(lane == TOP_K + k, rank, ri)
        rwt = jnp.where(lane == k, es[k] / den, rwt)
    ri_ref[...] = ri.astype(I32)
    rwt_ref[...] = rwt
    new = carry_ref[0:1, :] + jnp.sum(member, axis=0, keepdims=True)
    carry_ref[...] = jnp.zeros_like(carry_ref) + new
    cnt_ref[...] = jnp.zeros_like(cnt_ref) + new


def _mix(x2, of, osw, on, wf, ws, wn, g, b, rw, rb, alpha, tm):
    n, d = x2.shape
    full = lambda a: pl.BlockSpec(a.shape, lambda i: (0,) * a.ndim)
    rows = lambda a: pl.BlockSpec((tm, a.shape[1]), lambda i: (i, 0))
    return pl.pallas_call(
        functools.partial(_mix_kernel, alpha=alpha),
        grid=(n // tm,),
        in_specs=[rows(x2), rows(of), rows(osw), rows(on), full(wf), full(ws), full(wn),
                  full(g), full(b), full(rw), full(rb)],
        out_specs=[pl.BlockSpec((tm, d), lambda i: (i, 0)),
                   pl.BlockSpec((tm, LANES), lambda i: (i, 0)),
                   pl.BlockSpec((tm, LANES), lambda i: (i, 0)),
                   pl.BlockSpec((8, LANES), lambda i: (0, 0))],
        out_shape=[jax.ShapeDtypeStruct((n, d), F32), jax.ShapeDtypeStruct((n, LANES), I32),
                   jax.ShapeDtypeStruct((n, LANES), F32), jax.ShapeDtypeStruct((8, LANES), F32)],
        scratch_shapes=[pltpu.VMEM((8, LANES), F32)],
        compiler_params=_params("arbitrary"),
        name="mix_ln_router",
    )(x2, of, osw, on, wf, ws, wn, g, b, rw, rb)


def _dispatch_kernel(pe_ref, pd_ref, nu_ref, dest_ref, x_ref, xs_hbm, zbuf, zsem, sem, *, rows):
    i = pl.program_id(0)
    tokens = x_ref.shape[0]
    nblk = xs_hbm.shape[0] // rows

    @pl.when(i == 0)
    def _zero_fill():
        zbuf[...] = jnp.zeros_like(zbuf)

        def block(start):
            return pltpu.make_async_copy(zbuf, xs_hbm.at[pl.ds(pl.multiple_of(start, rows), rows)], zsem)

        for phase in ("start", "wait"):
            for e in range(N_EXPERTS):
                @pl.when(pd_ref[e] > 0)
                def _(e=e, phase=phase):
                    getattr(block(pe_ref[e] - rows), phase)()

            def tail(j, _, phase=phase):
                getattr(block(j * rows), phase)()
                return 0

            lax.fori_loop(nu_ref[0], nblk, tail, 0)

    for t in range(tokens):
        for k in range(TOP_K):
            c = t * TOP_K + k
            pltpu.make_async_copy(x_ref.at[pl.ds(t, 1)],
                                  xs_hbm.at[pl.ds(dest_ref[c], 1)], sem).start(priority=c % 2)
    for k in range(TOP_K):
        pltpu.make_async_copy(x_ref, xs_hbm.at[pl.ds(0, tokens)], sem).wait()


def _dispatch(pad_end, padded, n_used, dest_flat, x1, rows_total, tokens, rows):
    n, d = x1.shape
    grid_spec = pltpu.PrefetchScalarGridSpec(
        num_scalar_prefetch=3,
        grid=(n // tokens,),
        in_specs=[pl.BlockSpec((tokens * TOP_K,), lambda i, *_: (i,), memory_space=pltpu.SMEM),
                  pl.BlockSpec((tokens, d), lambda i, *_: (i, 0))],
        out_specs=pl.BlockSpec(memory_space=pl.ANY),
        scratch_shapes=[pltpu.VMEM((rows, d), F32), pltpu.SemaphoreType.DMA(()), pltpu.SemaphoreType.DMA(())],
    )
    return pl.pallas_call(
        functools.partial(_dispatch_kernel, rows=rows),
        grid_spec=grid_spec,
        out_shape=jax.ShapeDtypeStruct((rows_total, d), F32),
        compiler_params=_params("arbitrary"),
        name="moe_dispatch",
    )(pad_end, padded, n_used, dest_flat, x1)


def _ffn_kernel(be_ref, nu_ref, x_ref, wg_ref, wu_ref, bg_ref, bu_ref, wd_ref, bd_ref, y_ref):
    i = pl.program_id(0)

    @pl.when(i < nu_ref[0])
    def _run():
        x = x_ref[...].astype(BF16)
        gate = jnp.dot(x, wg_ref[0], preferred_element_type=F32) + bg_ref[0]
        up = jnp.dot(x, wu_ref[0], preferred_element_type=F32) + bu_ref[0]
        gate = jnp.minimum(gate, SWIGLU_LIMIT)
        up = jnp.clip(up, -SWIGLU_LIMIT, SWIGLU_LIMIT)
        glu = gate * jax.nn.sigmoid(gate * SWIGLU_ALPHA)
        act = ((up + 1.0) * glu).astype(BF16)
        y_ref[...] = jnp.dot(act, wd_ref[0, 0].astype(BF16), preferred_element_type=F32) + bd_ref[0]

    @pl.when(i >= nu_ref[0])
    def _skip():
        y_ref[...] = jnp.zeros_like(y_ref)


def _ffn(blk_e, n_used, xs, wg, wu, bg, bu, wd_all, layer, bd, rows):
    p, d = xs.shape
    wspec = lambda a: pl.BlockSpec((1,) + a.shape[1:], lambda i, be, nu: (be[i], 0, 0))
    grid_spec = pltpu.PrefetchScalarGridSpec(
        num_scalar_prefetch=2,
        grid=(p // rows,),
        in_specs=[pl.BlockSpec((rows, d), lambda i, be, nu: (i, 0)),
                  wspec(wg), wspec(wu), wspec(bg), wspec(bu),
                  pl.BlockSpec((1, 1) + wd_all.shape[2:], lambda i, be, nu: (layer, be[i], 0, 0)),
                  wspec(bd)],
        out_specs=pl.BlockSpec((rows, d), lambda i, be, nu: (i, 0)),
    )
    return pl.pallas_call(
        _ffn_kernel,
        grid_spec=grid_spec,
        out_shape=jax.ShapeDtypeStruct((p, d), F32),
        compiler_params=_params("arbitrary"),
        name="moe_ffn",
    )(blk_e, n_used, xs, wg, wu, bg, bu, wd_all, bd)


def _combine_kernel(destc_ref, destn_ref, x1_ref, w_ref, g_ref, b_ref, y_hbm, o_ref, buf0, buf1, sem, *, alpha):
    i = pl.program_id(0)
    last = pl.num_programs(0) - 1
    tokens = x1_ref.shape[0]
    bufs = (buf0, buf1)

    def issue(dest_ref, dst):
        for t in range(tokens):
            for k in range(TOP_K):
                c = t * TOP_K + k
                pltpu.make_async_copy(y_hbm.at[pl.ds(dest_ref[c], 1)],
                                      bufs[dst].at[k, pl.ds(t, 1)], sem.at[dst]).start(priority=c % 2)

    @pl.when(i == 0)
    def _prologue():
        issue(destc_ref, 0)

    def run(slot):
        @pl.when(i < last)
        def _():
            issue(destn_ref, 1 - slot)

        for k in range(TOP_K):
            pltpu.make_async_copy(y_hbm.at[pl.ds(0, tokens)], bufs[slot].at[k], sem.at[slot]).wait()
        w = w_ref[...]
        ffn = jnp.zeros(x1_ref.shape, F32)
        for k in range(TOP_K):
            ffn = ffn + bufs[slot][k] * w[:, k:k + 1]
        o_ref[...] = _layer_norm(alpha * x1_ref[...] + ffn, g_ref[...], b_ref[...])

    for slot in range(2):
        pl.when(i % 2 == slot)(functools.partial(run, slot))


def _combine(dest_flat, x1, rwt, g, b, y, alpha, tokens):
    n, d = x1.shape
    steps = n // tokens
    full = lambda a: pl.BlockSpec(a.shape, lambda i: (0,) * a.ndim)
    return pl.pallas_call(
        functools.partial(_combine_kernel, alpha=alpha),
        grid=(steps,),
        in_specs=[pl.BlockSpec((tokens * TOP_K,), lambda i: (i,), memory_space=pltpu.SMEM),
                  pl.BlockSpec((tokens * TOP_K,), lambda i: (jnp.minimum(i + 1, steps - 1),),
                               memory_space=pltpu.SMEM),
                  pl.BlockSpec((tokens, d), lambda i: (i, 0)),
                  pl.BlockSpec((tokens, LANES), lambda i: (i, 0)),
                  full(g), full(b),
                  pl.BlockSpec(memory_space=pl.ANY)],
        out_specs=pl.BlockSpec((tokens, d), lambda i: (i, 0)),
        out_shape=jax.ShapeDtypeStruct((n, d), F32),
        scratch_shapes=[pltpu.VMEM((TOP_K, tokens, d), F32), pltpu.VMEM((TOP_K, tokens, d), F32),
                        pltpu.SemaphoreType.DMA((2,))],
        compiler_params=_params("arbitrary"),
        name="moe_combine_ln",
    )(dest_flat, dest_flat, x1, rwt, g, b, y)


def _prep_gate_up_kernel(w_ref, p_ref, g_ref, u_ref):
    groups = w_ref.shape[3] // (2 * LANES)
    for c in range(groups):
        blk = w_ref[0, 0, :, c * 2 * LANES:(c + 1) * 2 * LANES].astype(BF16)
        o = jnp.dot(blk, p_ref[...], preferred_element_type=F32)
        g_ref[0, :, c * LANES:(c + 1) * LANES] = o[:, :LANES].astype(BF16)
        u_ref[0, :, c * LANES:(c + 1) * LANES] = o[:, LANES:].astype(BF16)


def _prep_gate_up(w_gu_all, layer):
    _, e, d, f2 = w_gu_all.shape
    perm = np.zeros((2 * LANES, 2 * LANES), np.float32)
    perm[2 * np.arange(LANES), np.arange(LANES)] = 1.0
    perm[2 * np.arange(LANES) + 1, LANES + np.arange(LANES)] = 1.0
    return pl.pallas_call(
        _prep_gate_up_kernel,
        grid=(e,),
        in_specs=[pl.BlockSpec((1, 1, d, f2), lambda i: (layer, i, 0, 0)),
                  pl.BlockSpec((2 * LANES, 2 * LANES), lambda i: (0, 0))],
        out_specs=[pl.BlockSpec((1, d, f2 // 2), lambda i: (i, 0, 0)),
                   pl.BlockSpec((1, d, f2 // 2), lambda i: (i, 0, 0))],
        out_shape=[jax.ShapeDtypeStruct((e, d, f2 // 2), BF16), jax.ShapeDtypeStruct((e, d, f2 // 2), BF16)],
        compiler_params=_params("arbitrary"),
        name="prep_gate_up",
    )(w_gu_all, jnp.asarray(perm, BF16))


def _gqa_cols(base):
    idx = np.empty(GQA_GROUP * KV_HEADS * HEAD_DIM, np.int64)
    for g in range(GQA_GROUP):
        for kv in range(KV_HEADS):
            dst = (g * KV_HEADS + kv) * HEAD_DIM
            src = (kv * GQA_GROUP + g) * HEAD_DIM
            idx[dst:dst + HEAD_DIM] = base + src + np.arange(HEAD_DIM)
    return idx


def _inproj_layout():
    fw = FOX_HEADS * HEAD_DIM
    qw = SWA_Q_HEADS * HEAD_DIM
    kw = KV_HEADS * HEAD_DIM
    sizes = (fw, fw, fw, FOX_HEADS, qw, kw, kw, qw, kw, kw, kw, kw, kw, kw, 3 * NSA_Q_HEADS)
    off = np.concatenate([[0], np.cumsum(sizes)])
    (o_fq, o_fk, o_fv, o_ff, o_sq, o_sk, o_sv, o_nq, o_nkc, o_nvc, o_nks, o_nvs, o_nkw, o_nvw, o_ng, total) = off
    ar = np.arange
    cols = np.concatenate([
        _gqa_cols(o_sq), _gqa_cols(o_nq), o_fq + ar(fw), o_fk + ar(fw), o_fv + ar(fw),
        o_sk + ar(kw), o_sv + ar(kw), o_nks + ar(kw), o_nvs + ar(kw), o_nkw + ar(kw), o_nvw + ar(kw),
        o_nkc + ar(kw), o_nvc + ar(kw)])
    gate = np.full(LANES, total, np.int64)
    gate[:FOX_HEADS] = o_ff + ar(FOX_HEADS)
    for r in range(3):
        for g in range(GQA_GROUP):
            for kv in range(KV_HEADS):
                gate[FOX_HEADS + (r * GQA_GROUP + g) * KV_HEADS + kv] = o_ng + (kv * GQA_GROUP + g) * 3 + r
    cols = np.concatenate([cols, gate])
    scale = np.ones(cols.shape[0], np.float32)
    q_scale = np.float32(HEAD_DIM ** -0.5 * LOG2E)
    scale[C_SQ:C_SQ + qw] = q_scale
    scale[C_NQ:C_NQ + qw] = q_scale
    scale[C_FQ:C_FQ + fw] = q_scale
    return cols, scale


def _outproj_rows():
    fw = FOX_HEADS * HEAD_DIM
    qw = SWA_Q_HEADS * HEAD_DIM
    return np.arange(fw), _gqa_cols(fw), _gqa_cols(fw + qw)


def _overlap_matrix_t(s_len, nc):
    ns = s_len // SEL_BLOCK
    cmp_start = np.arange(nc) * CMP_STRIDE
    sel_start = np.arange(ns) * SEL_BLOCK
    ov = np.clip(np.minimum(cmp_start[:, None] + CMP_BLOCK, sel_start[None, :] + SEL_BLOCK)
                 - np.maximum(cmp_start[:, None], sel_start[None, :]), 0, None) / CMP_BLOCK
    ov[(s_len - CMP_BLOCK) // CMP_STRIDE + 1:] = 0.0
    out = np.zeros((KV_HEADS, LANES, nc), np.float32)
    for kv in range(KV_HEADS):
        r0 = _spare(kv) + L_BLK
        out[kv, r0:r0 + ns, :] = ov.T
    return out


def _compress_weights(pe, w1, w2):
    pe2 = jnp.concatenate([pe, pe], axis=-1)
    w1r = w1.reshape(2, CMP_BLOCK, HEAD_DIM, HEAD_DIM)
    z = jnp.zeros_like(w1r)
    w1b = jnp.concatenate([jnp.concatenate([w1r, z], axis=-1), jnp.concatenate([z, w1r], axis=-1)], axis=-2)
    z2 = jnp.zeros_like(w2)
    w2b = jnp.concatenate([jnp.concatenate([w2, z2], axis=-1), jnp.concatenate([z2, w2], axis=-1)], axis=-2)
    return pe2, w1b.astype(BF16), w2b.astype(BF16)


def kernel(x, w_in, b_in, sinks, cmp_pe, cmp_w1, cmp_w2, w_out, ln1_g, ln1_b, router_w, router_b,
           w_gate_up, b_gate_up, w_down, b_down, ln2_g, ln2_b):
    depth = w_in.shape[0]
    bsz, s_len, d = x.shape
    n = bsz * s_len
    nk = n * TOP_K
    alpha = float((2.0 * depth) ** 0.25)
    assert s_len % SEL_KEYS == 0 and s_len // SEL_BLOCK <= MAX_SEL_BLOCKS and n % 512 == 0
    slopes_swa, slopes_nsa = _alibi()
    cols, scale = _inproj_layout()
    rows_f, rows_s, rows_n = _outproj_rows()
    nc = s_len // CMP_STRIDE
    ovt = jnp.asarray(_overlap_matrix_t(s_len, nc), BF16)
    nblk = -(-nk // MOE_ROWS) + N_EXPERTS
    p_rows = nblk * MOE_ROWS
    f = w_down.shape[2]

    x2 = x.reshape(n, d)
    for l in range(depth):
        w_ext = jnp.concatenate([w_in[l], jnp.zeros((d, 1), F32)], axis=1)
        b_ext = jnp.concatenate([b_in[l], jnp.zeros((1,), F32)])
        w_l = (jnp.take(w_ext, cols, axis=1) * scale).astype(BF16)
        b_l = (jnp.take(b_ext, cols) * scale).reshape(1, -1)
        hb, hf = _inproj(x2, w_l, b_l, 512)
        hb3 = hb.reshape(bsz, s_len, CB)
        hf3 = hf.reshape(bsz, s_len, CF)
        cg = _gates(hf3)
        o_fox = _fox(hb3, cg)
        o_swa = _swa(hb3, sinks[l] * LOG2E, slopes_swa)
        grp = hf3[:, :, :2 * LANES].reshape(bsz, nc, CMP_STRIDE * 2 * LANES)
        pe2, w1b, w2b = _compress_weights(cmp_pe[l], cmp_w1[l], cmp_w2[l])
        kc, vc = _compress(grp, pe2, w1b, w2b)
        o_nsa = _nsa(hb3, kc, vc, cg, ovt, slopes_nsa)

        wo = w_out[l].astype(BF16)
        rw = jnp.concatenate([router_w[l], jnp.zeros((d, LANES - N_EXPERTS), F32)], axis=1).astype(BF16)
        rb = jnp.concatenate([router_b[l], jnp.zeros((LANES - N_EXPERTS,), F32)]).reshape(1, LANES)
        x1, ri, rwt, cnt = _mix(
            x2, o_fox.reshape(n, -1), o_swa.reshape(n, -1), o_nsa.reshape(n, -1),
            jnp.take(wo, rows_f, axis=0), jnp.take(wo, rows_s, axis=0), jnp.take(wo, rows_n, axis=0),
            ln1_g[l].reshape(1, d), ln1_b[l].reshape(1, d), rw, rb, alpha, 1024)

        counts = cnt[0, :N_EXPERTS].astype(I32)
        padded = (counts + MOE_ROWS - 1) // MOE_ROWS * MOE_ROWS
        pad_end = jnp.cumsum(padded)
        pad_start = pad_end - padded
        dest = (jnp.take(pad_start, ri[:, :TOP_K]) + ri[:, TOP_K:2 * TOP_K]).reshape(nk)
        blk_start = jnp.arange(nblk, dtype=I32) * MOE_ROWS
        blk_e = jnp.minimum(jnp.sum((pad_end[None, :] <= blk_start[:, None]).astype(I32), axis=1),
                            N_EXPERTS - 1)
        n_used = (pad_end[-1:] // MOE_ROWS).astype(I32)

        xs = _dispatch(pad_end, padded, n_used, dest, x1, p_rows, 512, MOE_ROWS)
        w_gate, w_up = _prep_gate_up(w_gate_up, l)
        bgu = b_gate_up[l].reshape(N_EXPERTS, 1, f, 2)
        y = _ffn(blk_e, n_used, xs, w_gate, w_up,
                 bgu[..., 0], bgu[..., 1], w_down, l, b_down[l].reshape(N_EXPERTS, 1, d),
                 MOE_ROWS)
        x2 = _combine(dest, x1, rwt, ln2_g[l].reshape(1, d), ln2_b[l].reshape(1, d), y, alpha, 256)
    return x2.reshape(bsz, s_len, d)
```
